```python
import jax
import jax.numpy as jnp
from jax import lax
import numpy as np

D_MODEL = 2048
BATCH = 2
SEQ = 4096
DEPTH = 2

D_MIX = D_MODEL
HEAD_DIM = 128
MLSTM_HEADS = D_MIX // 4 // HEAD_DIM
FOX_HEADS = D_MIX // 2 // HEAD_DIM
GDN_HEADS = D_MIX // 4 // HEAD_DIM
MLSTM_W = MLSTM_HEADS * HEAD_DIM
FOX_W = FOX_HEADS * HEAD_DIM
GDN_W = GDN_HEADS * HEAD_DIM
D_FF = -(-8 * D_MODEL // (3 * 256)) * 256
MLSTM_CHUNK = 64
FOX_BLOCK = 128
GDN_CHUNK = 64
CONV_WIDTH = 4
GATE_SOFTCAP = 15.0
NORM_EPS = 1e-6
SPLITS = (MLSTM_W, MLSTM_W, MLSTM_W, MLSTM_W, MLSTM_HEADS, MLSTM_HEADS,
          FOX_W, FOX_W, FOX_W, FOX_HEADS,
          GDN_W, GDN_W, GDN_W, GDN_W, GDN_HEADS, GDN_HEADS)
N_IN = 4 * MLSTM_W + 2 * MLSTM_HEADS + 3 * FOX_W + FOX_HEADS + 4 * GDN_W + 2 * GDN_HEADS

kernel_name = 'hymba_style_mlstm_fox_gdn_hybrid'


def rms_norm(x, w):
    xf = x.astype(jnp.float32)
    y = xf * lax.rsqrt(jnp.mean(xf * xf, axis=-1, keepdims=True) + NORM_EPS)
    return (y * w.astype(jnp.float32)).astype(x.dtype)


def split_columns(t, sizes):
    out = []
    off = 0
    for s in sizes:
        out.append(t[..., off:off + s])
        off += s
    return out


def to_heads(t, n_heads):
    b, s, _ = t.shape
    return t.reshape(b, s, n_heads, -1).transpose(0, 2, 1, 3)


def merge_heads(t):
    b, h, s, d = t.shape
    return t.transpose(0, 2, 1, 3).reshape(b, s, h * d)


def to_chunks(t, size):
    b, h, s = t.shape[:3]
    t = t.reshape(b, h, s // size, size, *t.shape[3:])
    return jnp.moveaxis(t, 2, 0)


def from_chunks(t):
    t = jnp.moveaxis(t, 0, 2)
    b, h, n, l, d = t.shape
    return t.reshape(b, h, n * l, d)


def head_rms_norm(t, w):
    y = t * lax.rsqrt(jnp.mean(t * t, axis=-1, keepdims=True) + NORM_EPS)
    return merge_heads(y) * w.astype(jnp.float32)


def head_layer_norm(t, w):
    mu = jnp.mean(t, axis=-1, keepdims=True)
    c = t - mu
    var = jnp.mean(c * c, axis=-1, keepdims=True)
    return merge_heads(c * lax.rsqrt(var + NORM_EPS)) * w.astype(jnp.float32)


def l2_normalize(t):
    return t * lax.rsqrt(jnp.sum(t * t, axis=-1, keepdims=True) + NORM_EPS)


def soft_cap(t):
    return GATE_SOFTCAP * jnp.tanh(t / GATE_SOFTCAP)


def causal_depthwise_conv(x, w):
    k, c = w.shape
    return lax.conv_general_dilated(x, w[:, None, :], window_strides=(1,), padding=[(k - 1, 0)],
                                    dimension_numbers=('NWC', 'WIO', 'NWC'), feature_group_count=c)


def mlstm_chunkwise(q, k, v, i_pre, f_pre):
    b, h, s, d = q.shape
    L = MLSTM_CHUNK
    k = k * (d ** -0.5)
    logf = jax.nn.log_sigmoid(f_pre)
    qc, kc, vc = to_chunks(q, L), to_chunks(k, L), to_chunks(v, L)
    ic = to_chunks(i_pre, L)
    bc = jnp.cumsum(to_chunks(logf, L), axis=-1)
    causal = jnp.tril(jnp.ones((L, L), dtype=bool))

    def step(carry, xs):
        C, nv, m = carry
        q_, k_, v_, i_, b_ = xs
        log_intra = jnp.where(causal, b_[..., :, None] - b_[..., None, :] + i_[..., None, :], -jnp.inf)
        log_inter = b_ + m[..., None]
        m_t = jnp.maximum(log_inter, jnp.max(log_intra, axis=-1))
        w_inter = jnp.exp(log_inter - m_t)
        p = jnp.exp(log_intra - m_t[..., None]) * jnp.einsum('bhtd,bhsd->bhts', q_, k_)
        num = w_inter[..., None] * jnp.einsum('bhtd,bhde->bhte', q_, C) + jnp.einsum('bhts,bhse->bhte', p, v_)
        den = w_inter * jnp.einsum('bhtd,bhd->bht', q_, nv) + jnp.sum(p, axis=-1)
        h_out = num / jnp.maximum(jnp.abs(den), jnp.exp(-m_t))[..., None]
        b_last = b_[..., -1]
        log_end = b_last[..., None] - b_ + i_
        m_new = jnp.maximum(b_last + m, jnp.max(log_end, axis=-1))
        w_state = jnp.exp(b_last + m - m_new)
        w_end = jnp.exp(log_end - m_new[..., None])
        C = w_state[..., None, None] * C + jnp.einsum('bhs,bhsd,bhse->bhde', w_end, k_, v_)
        nv = w_state[..., None] * nv + jnp.einsum('bhs,bhsd->bhd', w_end, k_)
        return (C, nv, m_new), h_out

    init = (jnp.zeros((b, h, d, d), q.dtype), jnp.zeros((b, h, d), q.dtype), jnp.zeros((b, h), q.dtype))
    _, hs = lax.scan(step, init, (qc, kc, vc, ic, bc))
    return from_chunks(hs)


def forgetting_attention(q, k, v, f_pre):
    b, h, s, d = q.shape
    nb = s // FOX_BLOCK
    c = jnp.cumsum(jax.nn.log_sigmoid(f_pre), axis=-1)
    qb = to_chunks(q * (d ** -0.5), FOX_BLOCK)
    cb = to_chunks(c, FOX_BLOCK)
    starts = jnp.arange(nb, dtype=jnp.int32) * FOX_BLOCK
    kpos = jnp.arange(s, dtype=jnp.int32)

    def block(args):
        q_, c_, start = args
        qpos = start + jnp.arange(FOX_BLOCK, dtype=jnp.int32)
        logits = jnp.einsum('bhqd,bhkd->bhqk', q_, k) + (c_[..., :, None] - c[:, :, None, :])
        logits = jnp.where(kpos[None, :] <= qpos[:, None], logits, -jnp.inf)
        p = jax.nn.softmax(logits, axis=-1)
        return jnp.einsum('bhqk,bhkd->bhqd', p, v)

    return from_chunks(lax.map(block, (qb, cb, starts)))


def gated_delta_rule(q, k, v, g, beta):
    b, h, s, d = q.shape
    L = GDN_CHUNK
    qc, kc, vc = to_chunks(q * (d ** -0.5), L), to_chunks(k, L), to_chunks(v, L)
    gc = jnp.cumsum(to_chunks(g, L), axis=-1)
    bc = to_chunks(beta, L)
    incl = jnp.tril(jnp.ones((L, L), dtype=bool))
    strict = jnp.tril(jnp.ones((L, L), dtype=bool), -1)
    diff = gc[..., :, None] - gc[..., None, :]
    decay = jnp.where(incl, jnp.exp(jnp.where(incl, diff, 0.0)), 0.0)
    kb = kc * bc[..., None]
    a_mat = jnp.where(strict, jnp.einsum('nbhtd,nbhsd->nbhts', kb, kc) * decay, 0.0)
    lhs = a_mat + jnp.eye(L, dtype=a_mat.dtype)
    w_val = lax.linalg.triangular_solve(lhs, vc * bc[..., None], left_side=True, lower=True, unit_diagonal=True)
    w_key = lax.linalg.triangular_solve(lhs, kb * jnp.exp(gc)[..., None], left_side=True, lower=True, unit_diagonal=True)
    attn = jnp.einsum('nbhtd,nbhsd->nbhts', qc, kc) * decay
    q_dec = qc * jnp.exp(gc)[..., None]
    g_last = gc[..., -1]
    k_end = kc * jnp.exp(g_last[..., None] - gc)[..., None]

    def step(state, xs):
        wv, wk, at, qd, ke, gl = xs
        u = wv - jnp.einsum('bhtd,bhde->bhte', wk, state)
        o = jnp.einsum('bhtd,bhde->bhte', qd, state) + jnp.einsum('bhts,bhse->bhte', at, u)
        state = jnp.exp(gl)[..., None, None] * state + jnp.einsum('bhsd,bhse->bhde', ke, u)
        return state, o

    _, os_ = lax.scan(step, jnp.zeros((b, h, d, d), q.dtype), (w_val, w_key, attn, q_dec, k_end, g_last))
    return from_chunks(os_)


def hybrid_mixer(h, w_in, mlstm_i_bias, mlstm_f_bias, fox_f_bias, gdn_conv_w, gdn_a_log, gdn_dt_bias,
                 mlstm_out_norm_w, fox_out_norm_w, gdn_out_norm_w, w_out):
    f32 = jnp.float32
    proj = (h @ w_in).astype(f32)
    (mq, mk, mv, mo, mi, mf, fq, fk, fv, ff, gq, gk, gv, gz, ga, gb) = split_columns(proj, SPLITS)

    mi = soft_cap(mi + mlstm_i_bias.astype(f32)).transpose(0, 2, 1)
    mf = soft_cap(mf + mlstm_f_bias.astype(f32)).transpose(0, 2, 1)
    hm = mlstm_chunkwise(to_heads(mq, MLSTM_HEADS), to_heads(mk, MLSTM_HEADS), to_heads(mv, MLSTM_HEADS), mi, mf)
    y_m = head_layer_norm(hm, mlstm_out_norm_w) * jax.nn.sigmoid(mo)

    hf = forgetting_attention(to_heads(fq, FOX_HEADS), to_heads(fk, FOX_HEADS), to_heads(fv, FOX_HEADS),
                              (ff + fox_f_bias.astype(f32)).transpose(0, 2, 1))
    y_f = head_rms_norm(hf, fox_out_norm_w)

    qkv = jax.nn.silu(causal_depthwise_conv(jnp.concatenate([gq, gk, gv], axis=-1), gdn_conv_w.astype(f32)))
    gq, gk, gv = split_columns(qkv, (GDN_W, GDN_W, GDN_W))
    g = (-jnp.exp(gdn_a_log.astype(f32)) * jax.nn.softplus(ga + gdn_dt_bias.astype(f32))).transpose(0, 2, 1)
    beta = jax.nn.sigmoid(gb).transpose(0, 2, 1)
    hg = gated_delta_rule(l2_normalize(to_heads(gq, GDN_HEADS)), l2_normalize(to_heads(gk, GDN_HEADS)),
                          to_heads(gv, GDN_HEADS), g, beta)
    y_g = head_rms_norm(hg, gdn_out_norm_w) * jax.nn.silu(gz)

    y = jnp.concatenate([y_m, y_f, y_g], axis=-1).astype(h.dtype)
    return y @ w_out


def swiglu(h, w_gate, w_up, w_down):
    return (jax.nn.silu(h @ w_gate) * (h @ w_up)) @ w_down


def setup_inputs(seed: int = 0) -> dict:
    key = jax.random.key(seed)
    ks = jax.random.split(key, 18)
    f32 = jnp.float32
    nrm = lambda k, shape, scale: jax.random.normal(k, shape, f32) * scale
    gain = lambda k, shape: 1.0 + 0.02 * jax.random.normal(k, shape, f32)
    dt = jnp.exp(jax.random.uniform(ks[7], (DEPTH, GDN_HEADS), f32, np.log(1e-3), np.log(1e-1)))
    return {
        'x': jax.random.normal(ks[0], (BATCH, SEQ, D_MODEL), f32),
        'mix_norm_w': gain(ks[1], (DEPTH, D_MODEL)),
        'w_in': nrm(ks[2], (DEPTH, D_MODEL, N_IN), D_MODEL ** -0.5),
        'mlstm_i_bias': nrm(ks[3], (DEPTH, MLSTM_HEADS), 0.1),
        'mlstm_f_bias': jax.random.uniform(ks[4], (DEPTH, MLSTM_HEADS), f32, 3.0, 6.0),
        'fox_f_bias': jax.random.uniform(ks[5], (DEPTH, FOX_HEADS), f32, 2.0, 5.0),
        'gdn_conv_w': nrm(ks[6], (DEPTH, CONV_WIDTH, 3 * GDN_W), CONV_WIDTH ** -0.5),
        'gdn_a_log': jnp.log(jax.random.uniform(ks[8], (DEPTH, GDN_HEADS), f32, 1.0, 16.0)),
        'gdn_dt_bias': dt + jnp.log(-jnp.expm1(-dt)),
        'mlstm_out_norm_w': gain(ks[9], (DEPTH, MLSTM_W)),
        'fox_out_norm_w': gain(ks[10], (DEPTH, FOX_W)),
        'gdn_out_norm_w': gain(ks[11], (DEPTH, GDN_W)),
        'w_out': nrm(ks[12], (DEPTH, D_MIX, D_MODEL), D_MIX ** -0.5),
        'ffn_norm_w': gain(ks[13], (DEPTH, D_MODEL)),
        'w_gate': nrm(ks[14], (DEPTH, D_MODEL, D_FF), D_MODEL ** -0.5),
        'w_up': nrm(ks[15], (DEPTH, D_MODEL, D_FF), D_MODEL ** -0.5),
        'w_down': nrm(ks[16], (DEPTH, D_FF, D_MODEL), D_FF ** -0.5),
        'final_norm_w': gain(ks[17], (D_MODEL,)),
    }


def reference(x, mix_norm_w, w_in, mlstm_i_bias, mlstm_f_bias, fox_f_bias, gdn_conv_w, gdn_a_log, gdn_dt_bias,
              mlstm_out_norm_w, fox_out_norm_w, gdn_out_norm_w, w_out, ffn_norm_w, w_gate, w_up, w_down,
              final_norm_w):
    for l in range(DEPTH):
        h = rms_norm(x, mix_norm_w[l])
        x = x + hybrid_mixer(h, w_in[l], mlstm_i_bias[l], mlstm_f_bias[l], fox_f_bias[l], gdn_conv_w[l],
                             gdn_a_log[l], gdn_dt_bias[l], mlstm_out_norm_w[l], fox_out_norm_w[l],
                             gdn_out_norm_w[l], w_out[l]).astype(x.dtype)
        h = rms_norm(x, ffn_norm_w[l])
        x = x + swiglu(h, w_gate[l], w_up[l], w_down[l]).astype(x.dtype)
    return rms_norm(x, final_norm_w)
```

```python
import functools

import jax
import jax.numpy as jnp
from jax import lax
from jax.experimental import pallas as pl
from jax.experimental.pallas import tpu as pltpu

F32 = jnp.float32
BF16 = jnp.bfloat16

D_MODEL = 2048
HEAD_DIM = 128
MLSTM_HEADS = 4
FOX_HEADS = 8
GDN_HEADS = 4
MLSTM_W = MLSTM_HEADS * HEAD_DIM
FOX_W = FOX_HEADS * HEAD_DIM
GDN_W = GDN_HEADS * HEAD_DIM
D_FF = 5632
CONV_WIDTH = 4
GATE_SOFTCAP = 15.0
NORM_EPS = 1e-6
QK_SCALE = HEAD_DIM ** -0.5

PROJ_W = 4 * MLSTM_W + 3 * FOX_W + 4 * GDN_W
GATE_ROWS = 32

LANES = 128
SUBLANES = 8
V7X_VMEM_BYTES = 64 * 1024 * 1024
VMEM_LIMIT = 56 * 1024 * 1024

CHUNK = 128
FOX_BLOCK = 512
GDN_STEP = 256

NT_DIMS = (((1,), (1,)), ((), ()))
TN_DIMS = (((0,), (0,)), ((), ()))


def _params(sem):
    return pltpu.CompilerParams(dimension_semantics=sem, vmem_limit_bytes=VMEM_LIMIT)


def _sigmoid(z):
    return 1.0 / (1.0 + jnp.exp(-z))


def _log_sigmoid(z):
    return jnp.minimum(z, 0.0) - jnp.log1p(jnp.exp(-jnp.abs(z)))


def _softplus(z):
    return jnp.maximum(z, 0.0) + jnp.log1p(jnp.exp(-jnp.abs(z)))


def _rms(t):
    return t * lax.rsqrt(jnp.mean(t * t, axis=-1, keepdims=True) + NORM_EPS)


def _inproj_body(x_ref, nw_ref, w_ref, wg_ref, proj_ref, gt_ref, h_scr, *, row_chunk):
    @pl.when(pl.program_id(1) == 0)
    def _():
        for r in range(x_ref.shape[0] // row_chunk):
            rows = slice(r * row_chunk, (r + 1) * row_chunk)
            h_scr[rows, :] = (_rms(x_ref[rows, :]) * nw_ref[...]).astype(BF16)
        gt_ref[...] = lax.dot_general(wg_ref[...], h_scr[...], NT_DIMS, preferred_element_type=F32)

    proj_ref[...] = jnp.dot(h_scr[...], w_ref[...], preferred_element_type=F32).astype(BF16)


def _inproj(x, norm_w, w_big, w_gate_t, *, tm=1024, tn=1024):
    m = x.shape[0]
    return pl.pallas_call(
        functools.partial(_inproj_body, row_chunk=256),
        out_shape=(jax.ShapeDtypeStruct((m, PROJ_W), BF16), jax.ShapeDtypeStruct((GATE_ROWS, m), F32)),
        grid=(m // tm, PROJ_W // tn),
        in_specs=[
            pl.BlockSpec((tm, D_MODEL), lambda i, j: (i, 0)),
            pl.BlockSpec((1, D_MODEL), lambda i, j: (0, 0)),
            pl.BlockSpec((D_MODEL, tn), lambda i, j: (0, j)),
            pl.BlockSpec((GATE_ROWS, D_MODEL), lambda i, j: (0, 0)),
        ],
        out_specs=(
            pl.BlockSpec((tm, tn), lambda i, j: (i, j)),
            pl.BlockSpec((GATE_ROWS, tm), lambda i, j: (0, i)),
        ),
        scratch_shapes=[pltpu.VMEM((tm, D_MODEL), BF16)],
        compiler_params=_params(("parallel", "arbitrary")),
        name="inproj",
    )(x, norm_w, w_big, w_gate_t)


def _lane_cumsum(v, seg):
    pos = lax.broadcasted_iota(jnp.int32, v.shape, 1) & (seg - 1)
    shift = 1
    while shift < seg:
        v = v + jnp.where(pos >= shift, pltpu.roll(v, shift, 1), 0.0)
        shift *= 2
    return v


def _gates_body(gt_ref, bias_ref, alog_ref, out_ref):
    seq = gt_ref.shape[1]
    first4 = lax.broadcasted_iota(jnp.int32, (SUBLANES, LANES), 0) < 4
    neg_a = -jnp.exp(alog_ref[16:24, :])
    carry = jnp.zeros((SUBLANES, 1), F32)
    for blk in range(seq // LANES):
        lanes = slice(blk * LANES, (blk + 1) * LANES)
        z = gt_ref[0:8, lanes] + bias_ref[0:8, :]
        z = GATE_SOFTCAP * jnp.tanh(z / GATE_SOFTCAP)
        out_ref[0:8, lanes] = jnp.where(first4, z, _lane_cumsum(_log_sigmoid(z), CHUNK))

        fox = _lane_cumsum(_log_sigmoid(gt_ref[8:16, lanes] + bias_ref[8:16, :]), LANES) + carry
        out_ref[8:16, lanes] = fox
        carry = fox[:, LANES - 1:LANES]

        raw = gt_ref[16:24, lanes]
        decay = _lane_cumsum(neg_a * _softplus(raw + bias_ref[16:24, :]), CHUNK)
        out_ref[16:24, lanes] = jnp.where(first4, decay, _sigmoid(raw))
        out_ref[24:32, lanes] = jnp.zeros((SUBLANES, LANES), F32)


def _gates(gt, bias_col, alog_col, *, batch, seq):
    return pl.pallas_call(
        _gates_body,
        out_shape=jax.ShapeDtypeStruct(gt.shape, F32),
        grid=(batch,),
        in_specs=[
            pl.BlockSpec((GATE_ROWS, seq), lambda b: (0, b)),
            pl.BlockSpec((GATE_ROWS, 1), lambda b: (0, 0)),
            pl.BlockSpec((GATE_ROWS, 1), lambda b: (0, 0)),
        ],
        out_specs=pl.BlockSpec((GATE_ROWS, seq), lambda b: (0, b)),
        compiler_params=_params(("parallel",)),
        name="gates",
    )(gt, bias_col, alog_col)


def _mlstm_body(q_ref, k_ref, v_ref, o_ref, gr_ref, gc_ref, nw_ref, y_ref, c_scr, n_scr, m_scr):
    @pl.when(pl.program_id(1) == 0)
    def _():
        c_scr[...] = jnp.zeros(c_scr.shape, F32)
        n_scr[...] = jnp.zeros(n_scr.shape, F32)
        m_scr[...] = jnp.zeros(m_scr.shape, F32)

    size = q_ref.shape[0]
    causal = (lax.broadcasted_iota(jnp.int32, (size, size), 0)
              >= lax.broadcasted_iota(jnp.int32, (size, size), 1))
    for h in range(MLSTM_HEADS):
        cols = slice(h * HEAD_DIM, (h + 1) * HEAD_DIM)
        q = q_ref[:, cols]
        k = k_ref[:, cols]
        v = v_ref[:, cols]
        i_row = gr_ref[h:h + 1, :]
        b_row = gr_ref[4 + h:5 + h, :]
        i_col = gc_ref[:, h:h + 1]
        b_col = gc_ref[:, 4 + h:5 + h]
        m_prev = m_scr[h:h + 1, 0:1]
        c_prev = c_scr[h]
        n_prev = n_scr[h:h + 1, :]

        log_intra = jnp.where(causal, b_col - b_row + i_row, -jnp.inf)
        log_inter = b_col + m_prev
        m_t = jnp.maximum(log_inter, jnp.max(log_intra, axis=-1, keepdims=True))
        w_inter = jnp.exp(log_inter - m_t)
        qk = lax.dot_general(q, k, NT_DIMS, preferred_element_type=F32) * QK_SCALE
        p = jnp.exp(log_intra - m_t) * qk
        num = (w_inter * jnp.dot(q, c_prev.astype(BF16), preferred_element_type=F32)
               + jnp.dot(p.astype(BF16), v, preferred_element_type=F32))
        den = (w_inter * jnp.sum(q.astype(F32) * n_prev, axis=-1, keepdims=True)
               + jnp.sum(p, axis=-1, keepdims=True))
        h_out = num / jnp.maximum(jnp.abs(den), jnp.exp(-m_t))

        centered = h_out - jnp.mean(h_out, axis=-1, keepdims=True)
        var = jnp.mean(centered * centered, axis=-1, keepdims=True)
        y = centered * lax.rsqrt(var + NORM_EPS) * nw_ref[:, cols] * _sigmoid(o_ref[:, cols].astype(F32))
        y_ref[:, cols] = y.astype(BF16)

        b_last = b_row[:, size - 1:size]
        m_new = jnp.maximum(b_last + m_prev, jnp.max(b_last - b_row + i_row, axis=-1, keepdims=True))
        w_state = jnp.exp(b_last + m_prev - m_new)
        kw = k.astype(F32) * (jnp.exp(b_last - b_col + i_col - m_new) * QK_SCALE)
        c_scr[h] = w_state * c_prev + lax.dot_general(kw.astype(BF16), v, TN_DIMS, preferred_element_type=F32)
        n_scr[h:h + 1, :] = w_state * n_prev + jnp.sum(kw, axis=0, keepdims=True)
        m_scr[h:h + 1, :] = jnp.broadcast_to(m_new, (1, LANES))


def _mlstm(proj, gr, gcol, norm_w, *, batch, seq):
    m = proj.shape[0]
    nc = seq // CHUNK
    row = lambda b, c: b * nc + c
    return pl.pallas_call(
        _mlstm_body,
        out_shape=jax.ShapeDtypeStruct((m, MLSTM_W), BF16),
        grid=(batch, nc),
        in_specs=[
            pl.BlockSpec((CHUNK, MLSTM_W), lambda b, c: (row(b, c), 0)),
            pl.BlockSpec((CHUNK, MLSTM_W), lambda b, c: (row(b, c), 1)),
            pl.BlockSpec((CHUNK, MLSTM_W), lambda b, c: (row(b, c), 2)),
            pl.BlockSpec((CHUNK, MLSTM_W), lambda b, c: (row(b, c), 3)),
            pl.BlockSpec((SUBLANES, CHUNK), lambda b, c: (0, row(b, c))),
            pl.BlockSpec((CHUNK, GATE_ROWS), lambda b, c: (row(b, c), 0)),
            pl.BlockSpec((1, MLSTM_W), lambda b, c: (0, 0)),
        ],
        out_specs=pl.BlockSpec((CHUNK, MLSTM_W), lambda b, c: (row(b, c), 0)),
        scratch_shapes=[
            pltpu.VMEM((MLSTM_HEADS, HEAD_DIM, HEAD_DIM), F32),
            pltpu.VMEM((SUBLANES, HEAD_DIM), F32),
            pltpu.VMEM((SUBLANES, LANES), F32),
        ],
        compiler_params=_params(("parallel", "arbitrary")),
        name="mlstm",
    )(proj, proj, proj, proj, gr, gcol, norm_w)


def _fox_body(q_ref, k_ref, v_ref, c_ref, nw_ref, y_ref, acc_scr, m_scr, l_scr):
    qi = pl.program_id(2)
    tq = q_ref.shape[0]
    q = (q_ref[...].astype(F32) * QK_SCALE).astype(BF16)

    c_q_row = c_ref[qi]
    eye = (lax.broadcasted_iota(jnp.int32, (LANES, LANES), 0)
           == lax.broadcasted_iota(jnp.int32, (LANES, LANES), 1))
    c_q = jnp.concatenate(
        [jnp.sum(jnp.where(eye, c_q_row[:, j * LANES:(j + 1) * LANES], 0.0), axis=-1, keepdims=True)
         for j in range(tq // LANES)], axis=0)

    m_scr[...] = jnp.full(m_scr.shape, -jnp.inf, F32)
    l_scr[...] = jnp.zeros(l_scr.shape, F32)
    acc_scr[...] = jnp.zeros(acc_scr.shape, F32)

    def block(kb, masked):
        rows = pl.ds(pl.multiple_of(kb * tq, tq), tq)
        k = k_ref[rows, :]
        v = v_ref[rows, :]
        z = lax.dot_general(q, k, NT_DIMS, preferred_element_type=F32) - c_ref[kb]
        if masked:
            keep = (lax.broadcasted_iota(jnp.int32, (tq, tq), 0)
                    >= lax.broadcasted_iota(jnp.int32, (tq, tq), 1))
            z = jnp.where(keep, z, -jnp.inf)
        m_prev = m_scr[...]
        m_new = jnp.maximum(m_prev, jnp.max(z, axis=-1, keepdims=True) + c_q)
        alpha = jnp.exp(m_prev - m_new)
        p = jnp.exp(z - (m_new - c_q))
        l_scr[...] = alpha * l_scr[...] + jnp.sum(p, axis=-1, keepdims=True)
        acc_scr[...] = alpha * acc_scr[...] + jnp.dot(p.astype(BF16), v, preferred_element_type=F32)
        m_scr[...] = m_new

    def off_diagonal(kb, carry):
        block(kb, False)
        return carry

    lax.fori_loop(0, qi, off_diagonal, 0)
    block(qi, True)
    y_ref[...] = (_rms(acc_scr[...] / l_scr[...]) * nw_ref[...]).astype(BF16)


def _fox(proj, c_rows, norm_w, *, batch, seq):
    m = proj.shape[0]
    nq = seq // FOX_BLOCK
    q_col = (4 * MLSTM_W) // HEAD_DIM
    k_col = q_col + FOX_HEADS
    v_col = k_col + FOX_HEADS
    return pl.pallas_call(
        _fox_body,
        out_shape=jax.ShapeDtypeStruct((m, FOX_W), BF16),
        grid=(batch, FOX_HEADS, nq),
        in_specs=[
            pl.BlockSpec((FOX_BLOCK, HEAD_DIM), lambda b, h, i: (b * nq + i, q_col + h)),
            pl.BlockSpec((seq, HEAD_DIM), lambda b, h, i: (b, k_col + h)),
            pl.BlockSpec((seq, HEAD_DIM), lambda b, h, i: (b, v_col + h)),
            pl.BlockSpec((None, nq, 1, FOX_BLOCK), lambda b, h, i: (h, b, 0, 0)),
            pl.BlockSpec((1, HEAD_DIM), lambda b, h, i: (0, h)),
        ],
        out_specs=pl.BlockSpec((FOX_BLOCK, HEAD_DIM), lambda b, h, i: (b * nq + i, h)),
        scratch_shapes=[
            pltpu.VMEM((FOX_BLOCK, HEAD_DIM), F32),
            pltpu.VMEM((FOX_BLOCK, 1), F32),
            pltpu.VMEM((FOX_BLOCK, 1), F32),
        ],
        compiler_params=_params(("parallel", "parallel", "arbitrary")),
        name="fox",
    )(proj, proj, proj, c_rows, norm_w)


def _unit_lower_inverse(a):
    size = a.shape[0]
    ri = lax.broadcasted_iota(jnp.int32, (size, size), 0)
    ci = lax.broadcasted_iota(jnp.int32, (size, size), 1)

    def same_block(bits):
        return (ri >> bits) == (ci >> bits)

    def mm(x, y):
        return jnp.dot(x.astype(BF16), y.astype(BF16), preferred_element_type=F32)

    neg = jnp.where(same_block(3), -a, 0.0)
    inv = jnp.where(ri == ci, 1.0, 0.0) + neg
    for _ in range(2):
        neg = mm(neg, neg)
        inv = inv + mm(inv, neg)
    bits = 3
    while (1 << bits) < size:
        off = jnp.where(same_block(bits + 1) & jnp.logical_not(same_block(bits)), a, 0.0)
        inv = inv - mm(inv, mm(off, inv))
        bits += 1
    return inv


def _gdn_body(q_ref, k_ref, v_ref, z_ref, cw_ref, gr_ref, gc_ref, nw_ref, y_ref, s_scr, ext_scr):
    step = q_ref.shape[0]
    halo = SUBLANES

    @pl.when(pl.program_id(1) == 0)
    def _():
        s_scr[...] = jnp.zeros(s_scr.shape, F32)
        ext_scr[0:halo, :] = jnp.zeros((halo, 3 * GDN_W), F32)

    ext_scr[halo:halo + step, 0:GDN_W] = q_ref[...].astype(F32)
    ext_scr[halo:halo + step, GDN_W:2 * GDN_W] = k_ref[...].astype(F32)
    ext_scr[halo:halo + step, 2 * GDN_W:3 * GDN_W] = v_ref[...].astype(F32)
    conv = jnp.zeros((step, 3 * GDN_W), F32)
    for j in range(CONV_WIDTH):
        start = halo - (CONV_WIDTH - 1) + j
        conv = conv + ext_scr[start:start + step, :] * cw_ref[j:j + 1, :]
    ext_scr[0:halo, :] = ext_scr[step:step + halo, :]
    qkv = conv * _sigmoid(conv)

    incl = (lax.broadcasted_iota(jnp.int32, (CHUNK, CHUNK), 0)
            >= lax.broadcasted_iota(jnp.int32, (CHUNK, CHUNK), 1))
    strict = (lax.broadcasted_iota(jnp.int32, (CHUNK, CHUNK), 0)
              > lax.broadcasted_iota(jnp.int32, (CHUNK, CHUNK), 1))

    def l2n(t):
        return t * lax.rsqrt(jnp.sum(t * t, axis=-1, keepdims=True) + NORM_EPS)

    def mm(x, y):
        return jnp.dot(x.astype(BF16), y.astype(BF16), preferred_element_type=F32)

    for h in range(GDN_HEADS):
        cols = slice(h * HEAD_DIM, (h + 1) * HEAD_DIM)
        q_all = l2n(qkv[:, h * HEAD_DIM:(h + 1) * HEAD_DIM]) * QK_SCALE
        k_all = l2n(qkv[:, GDN_W + h * HEAD_DIM:GDN_W + (h + 1) * HEAD_DIM])
        v_all = qkv[:, 2 * GDN_W + h * HEAD_DIM:2 * GDN_W + (h + 1) * HEAD_DIM]
        state = s_scr[h]
        for r in range(step // CHUNK):
            rows = slice(r * CHUNK, (r + 1) * CHUNK)
            q, k, v = q_all[rows], k_all[rows], v_all[rows]
            g_row = gr_ref[h:h + 1, rows]
            g_col = gc_ref[rows, 16 + h:17 + h]
            beta = gc_ref[rows, 20 + h:21 + h]
            decay = jnp.where(incl, jnp.exp(jnp.where(incl, g_col - g_row, 0.0)), 0.0)
            kb = k * beta
            a_mat = jnp.where(strict, lax.dot_general(kb.astype(BF16), k.astype(BF16), NT_DIMS,
                                                      preferred_element_type=F32) * decay, 0.0)
            inv = _unit_lower_inverse(a_mat)
            e_g = jnp.exp(g_col)
            w_val = mm(inv, v * beta)
            w_key = mm(inv, kb * e_g)
            attn = lax.dot_general(q.astype(BF16), k.astype(BF16), NT_DIMS,
                                   preferred_element_type=F32) * decay
            g_last = g_row[:, CHUNK - 1:CHUNK]
            k_end = k * jnp.exp(g_last - g_col)
            u = w_val - mm(w_key, state)
            out = mm(q * e_g, state) + mm(attn, u)
            state = jnp.exp(g_last) * state + lax.dot_general(
                k_end.astype(BF16), u.astype(BF16), TN_DIMS, preferred_element_type=F32)
            gate = z_ref[rows, cols].astype(F32)
            y_ref[rows, cols] = (_rms(out) * nw_ref[:, cols] * (gate * _sigmoid(gate))).astype(BF16)
        s_scr[h] = state


def _gdn(proj, conv_w, gr, gcol, norm_w, *, batch, seq):
    m = proj.shape[0]
    ns = seq // GDN_STEP
    col0 = (4 * MLSTM_W + 3 * FOX_W) // GDN_W
    row = lambda b, s: b * ns + s
    return pl.pallas_call(
        _gdn_body,
        out_shape=jax.ShapeDtypeStruct((m, GDN_W), BF16),
        grid=(batch, ns),
        in_specs=[
            pl.BlockSpec((GDN_STEP, GDN_W), lambda b, s: (row(b, s), col0)),
            pl.BlockSpec((GDN_STEP, GDN_W), lambda b, s: (row(b, s), col0 + 1)),
            pl.BlockSpec((GDN_STEP, GDN_W), lambda b, s: (row(b, s), col0 + 2)),
            pl.BlockSpec((GDN_STEP, GDN_W), lambda b, s: (row(b, s), col0 + 3)),
            pl.BlockSpec((CONV_WIDTH, 3 * GDN_W), lambda b, s: (0, 0)),
            pl.BlockSpec((SUBLANES, GDN_STEP), lambda b, s: (2, row(b, s))),
            pl.BlockSpec((GDN_STEP, GATE_ROWS), lambda b, s: (row(b, s), 0)),
            pl.BlockSpec((1, GDN_W), lambda b, s: (0, 0)),
        ],
        out_specs=pl.BlockSpec((GDN_STEP, GDN_W), lambda b, s: (row(b, s), 0)),
        scratch_shapes=[
            pltpu.VMEM((GDN_HEADS, HEAD_DIM, HEAD_DIM), F32),
            pltpu.VMEM((GDN_STEP + 2 * SUBLANES, 3 * GDN_W), F32),
        ],
        compiler_params=_params(("parallel", "arbitrary")),
        name="gdn",
    )(proj, proj, proj, proj, conv_w, gr, gcol, norm_w)


def _outproj_body(ym_ref, yf_ref, yg_ref, w_ref, x_ref, o_ref):
    acc = jnp.dot(ym_ref[...], w_ref[0:MLSTM_W, :], preferred_element_type=F32)
    acc += jnp.dot(yf_ref[...], w_ref[MLSTM_W:MLSTM_W + FOX_W, :], preferred_element_type=F32)
    acc += jnp.dot(yg_ref[...], w_ref[MLSTM_W + FOX_W:, :], preferred_element_type=F32)
    o_ref[...] = x_ref[...] + acc


def _outproj(y_m, y_f, y_g, w_out, x, *, tm=512):
    m = x.shape[0]
    return pl.pallas_call(
        _outproj_body,
        out_shape=jax.ShapeDtypeStruct((m, D_MODEL), F32),
        grid=(m // tm,),
        in_specs=[
            pl.BlockSpec((tm, MLSTM_W), lambda i: (i, 0)),
            pl.BlockSpec((tm, FOX_W), lambda i: (i, 0)),
            pl.BlockSpec((tm, GDN_W), lambda i: (i, 0)),
            pl.BlockSpec((D_MODEL, D_MODEL), lambda i: (0, 0)),
            pl.BlockSpec((tm, D_MODEL), lambda i: (i, 0)),
        ],
        out_specs=pl.BlockSpec((tm, D_MODEL), lambda i: (i, 0)),
        compiler_params=_params(("parallel",)),
        name="outproj",
    )(y_m, y_f, y_g, w_out, x)


def _ffn_body(x_ref, nw_ref, wg_ref, wu_ref, wd_ref, fw_ref, o_ref, h_scr, acc_scr, *, final_norm):
    j = pl.program_id(1)

    @pl.when(j == 0)
    def _():
        h_scr[...] = (_rms(x_ref[...]) * nw_ref[...]).astype(BF16)
        acc_scr[...] = jnp.zeros(acc_scr.shape, F32)

    h = h_scr[...]
    gate = jnp.dot(h, wg_ref[...], preferred_element_type=F32)
    up = jnp.dot(h, wu_ref[...], preferred_element_type=F32)
    act = (gate * _sigmoid(gate) * up).astype(BF16)
    acc_scr[...] += jnp.dot(act, wd_ref[...], preferred_element_type=F32)

    @pl.when(j == pl.num_programs(1) - 1)
    def _():
        out = x_ref[...] + acc_scr[...]
        if final_norm:
            out = _rms(out) * fw_ref[...]
        o_ref[...] = out


def _ffn(x, norm_w, w_gate, w_up, w_down, final_w, *, final_norm, tm=512, tf=512):
    m = x.shape[0]
    return pl.pallas_call(
        functools.partial(_ffn_body, final_norm=final_norm),
        out_shape=jax.ShapeDtypeStruct((m, D_MODEL), F32),
        grid=(m // tm, D_FF // tf),
        in_specs=[
            pl.BlockSpec((tm, D_MODEL), lambda i, j: (i, 0)),
            pl.BlockSpec((1, D_MODEL), lambda i, j: (0, 0)),
            pl.BlockSpec((D_MODEL, tf), lambda i, j: (0, j)),
            pl.BlockSpec((D_MODEL, tf), lambda i, j: (0, j)),
            pl.BlockSpec((tf, D_MODEL), lambda i, j: (j, 0)),
            pl.BlockSpec((1, D_MODEL), lambda i, j: (0, 0)),
        ],
        out_specs=pl.BlockSpec((tm, D_MODEL), lambda i, j: (i, 0)),
        scratch_shapes=[pltpu.VMEM((tm, D_MODEL), BF16), pltpu.VMEM((tm, D_MODEL), F32)],
        compiler_params=_params(("parallel", "arbitrary")),
        name="ffn",
    )(x, norm_w, w_gate, w_up, w_down, final_w)


def _split_w_in(w):
    o_mi = 4 * MLSTM_W
    o_fq = o_mi + 2 * MLSTM_HEADS
    o_ff = o_fq + 3 * FOX_W
    o_gq = o_ff + FOX_HEADS
    o_ga = o_gq + 4 * GDN_W
    wide = jnp.concatenate([w[:, :o_mi], w[:, o_fq:o_ff], w[:, o_gq:o_ga]], axis=1).astype(BF16)
    gate = jnp.concatenate([w[:, o_mi:o_fq], w[:, o_ff:o_gq], w[:, o_ga:]], axis=1)
    gate_t = jnp.pad(gate.T, ((0, GATE_ROWS - gate.shape[1]), (0, 0))).astype(BF16)
    return wide, gate_t


def kernel(x, mix_norm_w, w_in, mlstm_i_bias, mlstm_f_bias, fox_f_bias, gdn_conv_w, gdn_a_log, gdn_dt_bias,
           mlstm_out_norm_w, fox_out_norm_w, gdn_out_norm_w, w_out, ffn_norm_w, w_gate, w_up, w_down,
           final_norm_w):
    batch, seq, d_model = x.shape
    depth = w_in.shape[0]
    assert d_model == D_MODEL and seq % FOX_BLOCK == 0 and seq % GDN_STEP == 0
    xf = x.reshape(batch * seq, d_model)
    zeros4 = jnp.zeros((GDN_HEADS,), F32)
    for l in range(depth):
        w_wide, w_gate_t = _split_w_in(w_in[l])
        bias_col = jnp.pad(jnp.concatenate([mlstm_i_bias[l], mlstm_f_bias[l], fox_f_bias[l], gdn_dt_bias[l],
                                            zeros4]), (0, 8)).reshape(GATE_ROWS, 1)
        alog_col = jnp.pad(gdn_a_log[l], (16, 12)).reshape(GATE_ROWS, 1)

        proj, gt = _inproj(xf, mix_norm_w[l].reshape(1, -1), w_wide, w_gate_t)
        gr = _gates(gt, bias_col, alog_col, batch=batch, seq=seq)
        gcol = gr.T
        c_rows = gr[8:16].reshape(FOX_HEADS, batch * (seq // FOX_BLOCK), 1, FOX_BLOCK)

        y_m = _mlstm(proj, gr, gcol, mlstm_out_norm_w[l].reshape(1, -1), batch=batch, seq=seq)
        y_f = _fox(proj, c_rows, fox_out_norm_w[l].reshape(1, -1), batch=batch, seq=seq)
        y_g = _gdn(proj, gdn_conv_w[l], gr, gcol, gdn_out_norm_w[l].reshape(1, -1), batch=batch, seq=seq)
        xf = _outproj(y_m, y_f, y_g, w_out[l].astype(BF16), xf)
        xf = _ffn(xf, ffn_norm_w[l].reshape(1, -1), w_gate[l].astype(BF16), w_up[l].astype(BF16),
                  w_down[l].astype(BF16), final_norm_w.reshape(1, -1), final_norm=(l == depth - 1))
    return xf.reshape(batch, seq, d_model)
```

```python
import functools

import jax
import jax.numpy as jnp
from jax import lax
from jax.experimental import pallas as pl
from jax.experimental.pallas import tpu as pltpu

F32 = jnp.float32
BF16 = jnp.bfloat16

D_MODEL = 2048
HEAD_DIM = 128
MLSTM_HEADS = 4
FOX_HEADS = 8
GDN_HEADS = 4
MLSTM_W = MLSTM_HEADS * HEAD_DIM
FOX_W = FOX_HEADS * HEAD_DIM
GDN_W = GDN_HEADS * HEAD_DIM
D_FF = 5632
CONV_WIDTH = 4
GATE_SOFTCAP = 15.0
NORM_EPS = 1e-6
QK_SCALE = HEAD_DIM ** -0.5
LOG2E = 1.4426950408889634

PROJ_W = 4 * MLSTM_W + 3 * FOX_W + 4 * GDN_W
GATE_ROWS = 32

LANES = 128
SUBLANES = 8
V7X_VMEM_BYTES = 64 * 1024 * 1024
VMEM_LIMIT = 56 * 1024 * 1024

CHUNK = 128
FOX_BLOCK = 512
FOX_SUB = 128
FOX_GROUP = 4
GDN_STEP = 256
MLSTM_STEP = 256

MASK_STRICT, MASK_INCL, MASK_DIAG8, MASK_OFF0 = 0, 1, 2, 3
MASK_COUNT = MASK_OFF0 + 4

NT_DIMS = (((1,), (1,)), ((), ()))
TN_DIMS = (((0,), (0,)), ((), ()))


def _params(sem):
    return pltpu.CompilerParams(dimension_semantics=sem, vmem_limit_bytes=VMEM_LIMIT)


def _sigmoid(z):
    return 1.0 / (1.0 + jnp.exp(-z))


def _log_sigmoid(z):
    return jnp.minimum(z, 0.0) - jnp.log1p(jnp.exp(-jnp.abs(z)))


def _softplus(z):
    return jnp.maximum(z, 0.0) + jnp.log1p(jnp.exp(-jnp.abs(z)))


def _rms(t):
    return t * lax.rsqrt(jnp.mean(t * t, axis=-1, keepdims=True) + NORM_EPS)


def _inproj_body(x_ref, nw_ref, w_ref, wg_ref, proj_ref, gt_ref, h_scr, *, row_chunk):
    @pl.when(pl.program_id(1) == 0)
    def _():
        for r in range(x_ref.shape[0] // row_chunk):
            rows = slice(r * row_chunk, (r + 1) * row_chunk)
            h_scr[rows, :] = (_rms(x_ref[rows, :]) * nw_ref[...]).astype(BF16)
        gt_ref[...] = lax.dot_general(wg_ref[...], h_scr[...], NT_DIMS, preferred_element_type=F32)

    proj_ref[...] = jnp.dot(h_scr[...], w_ref[...], preferred_element_type=F32).astype(BF16)


def _inproj(x, norm_w, w_big, w_gate_t, *, tm=1024, tn=1024):
    m = x.shape[0]
    return pl.pallas_call(
        functools.partial(_inproj_body, row_chunk=256),
        out_shape=(jax.ShapeDtypeStruct((m, PROJ_W), BF16), jax.ShapeDtypeStruct((GATE_ROWS, m), F32)),
        grid=(m // tm, PROJ_W // tn),
        in_specs=[
            pl.BlockSpec((tm, D_MODEL), lambda i, j: (i, 0)),
            pl.BlockSpec((1, D_MODEL), lambda i, j: (0, 0)),
            pl.BlockSpec((D_MODEL, tn), lambda i, j: (0, j)),
            pl.BlockSpec((GATE_ROWS, D_MODEL), lambda i, j: (0, 0)),
        ],
        out_specs=(
            pl.BlockSpec((tm, tn), lambda i, j: (i, j)),
            pl.BlockSpec((GATE_ROWS, tm), lambda i, j: (0, i)),
        ),
        scratch_shapes=[pltpu.VMEM((tm, D_MODEL), BF16)],
        compiler_params=_params(("parallel", "arbitrary")),
        name="inproj",
    )(x, norm_w, w_big, w_gate_t)


def _lane_cumsum(v, seg):
    pos = lax.broadcasted_iota(jnp.int32, v.shape, 1) & (seg - 1)
    shift = 1
    while shift < seg:
        v = v + jnp.where(pos >= shift, pltpu.roll(v, shift, 1), 0.0)
        shift *= 2
    return v


def _gates_body(gt_ref, bias_ref, alog_ref, out_ref):
    seq = gt_ref.shape[1]
    first4 = lax.broadcasted_iota(jnp.int32, (SUBLANES, LANES), 0) < 4
    neg_a = -jnp.exp(alog_ref[16:24, :])
    carry = jnp.zeros((SUBLANES, 1), F32)
    for blk in range(seq // LANES):
        lanes = slice(blk * LANES, (blk + 1) * LANES)
        z = gt_ref[0:8, lanes] + bias_ref[0:8, :]
        z = GATE_SOFTCAP * jnp.tanh(z / GATE_SOFTCAP)
        out_ref[0:8, lanes] = jnp.where(first4, z, _lane_cumsum(_log_sigmoid(z), CHUNK))

        fox = _lane_cumsum(_log_sigmoid(gt_ref[8:16, lanes] + bias_ref[8:16, :]), LANES) + carry
        out_ref[8:16, lanes] = fox
        carry = fox[:, LANES - 1:LANES]

        raw = gt_ref[16:24, lanes]
        decay = _lane_cumsum(neg_a * _softplus(raw + bias_ref[16:24, :]), CHUNK)
        out_ref[16:24, lanes] = jnp.where(first4, decay, _sigmoid(raw))
        out_ref[24:32, lanes] = jnp.zeros((SUBLANES, LANES), F32)


def _gates(gt, bias_col, alog_col, *, batch, seq):
    return pl.pallas_call(
        _gates_body,
        out_shape=jax.ShapeDtypeStruct(gt.shape, F32),
        grid=(batch,),
        in_specs=[
            pl.BlockSpec((GATE_ROWS, seq), lambda b: (0, b)),
            pl.BlockSpec((GATE_ROWS, 1), lambda b: (0, 0)),
            pl.BlockSpec((GATE_ROWS, 1), lambda b: (0, 0)),
        ],
        out_specs=pl.BlockSpec((GATE_ROWS, seq), lambda b: (0, b)),
        compiler_params=_params(("parallel",)),
        name="gates",
    )(gt, bias_col, alog_col)


def _mlstm_body(q_ref, k_ref, v_ref, o_ref, gr_ref, gc_ref, nw_ref, y_ref, ct_scr, m_scr):
    @pl.when(pl.program_id(1) == 0)
    def _():
        ct_scr[...] = jnp.zeros(ct_scr.shape, F32)
        m_scr[...] = jnp.zeros(m_scr.shape, F32)

    heads = range(MLSTM_HEADS)
    causal = (lax.broadcasted_iota(jnp.int32, (CHUNK, CHUNK), 0)
              >= lax.broadcasted_iota(jnp.int32, (CHUNK, CHUNK), 1))
    ones = jnp.ones((CHUNK, HEAD_DIM), BF16)
    cts = [ct_scr[h] for h in heads]
    ms = [m_scr[h:h + 1, 0:1] for h in heads]
    for r in range(q_ref.shape[0] // CHUNK):
        rows = slice(r * CHUNK, (r + 1) * CHUNK)
        hcols = [slice(h * HEAD_DIM, (h + 1) * HEAD_DIM) for h in heads]
        q = [q_ref[rows, c] for c in hcols]
        k = [k_ref[rows, c] for c in hcols]
        v_aug = [jnp.concatenate([v_ref[rows, c], ones], axis=1) for c in hcols]
        i_row = [gr_ref[h:h + 1, rows] for h in heads]
        b_row = [gr_ref[4 + h:5 + h, rows] for h in heads]
        i_col = [gc_ref[rows, h:h + 1] for h in heads]
        b_col = [gc_ref[rows, 4 + h:5 + h] for h in heads]

        qk = [lax.dot_general(q[h], k[h], NT_DIMS, preferred_element_type=F32) for h in heads]
        q_state = [lax.dot_general(q[h], cts[h].astype(BF16), NT_DIMS, preferred_element_type=F32)
                   for h in heads]
        log_intra = [jnp.where(causal, b_col[h] - b_row[h] + i_row[h], -jnp.inf) for h in heads]
        log_inter = [b_col[h] + ms[h] for h in heads]
        m_t = [jnp.maximum(log_inter[h], jnp.max(log_intra[h], axis=-1, keepdims=True)) for h in heads]
        p = [jnp.exp(log_intra[h] - m_t[h]) * (qk[h] * QK_SCALE) for h in heads]
        intra = [jnp.dot(p[h].astype(BF16), v_aug[h], preferred_element_type=F32) for h in heads]
        for h in heads:
            both = jnp.exp(log_inter[h] - m_t[h]) * q_state[h] + intra[h]
            h_out = both[:, :HEAD_DIM] / jnp.maximum(jnp.abs(both[:, HEAD_DIM:]), jnp.exp(-m_t[h]))
            centered = h_out - jnp.mean(h_out, axis=-1, keepdims=True)
            var = jnp.mean(centered * centered, axis=-1, keepdims=True)
            gate = _sigmoid(o_ref[rows, hcols[h]].astype(F32))
            y_ref[rows, hcols[h]] = (centered * lax.rsqrt(var + NORM_EPS) * nw_ref[:, hcols[h]]
                                     * gate).astype(BF16)

        b_last = [b_row[h][:, CHUNK - 1:CHUNK] for h in heads]
        m_new = [jnp.maximum(b_last[h] + ms[h],
                             jnp.max(b_last[h] - b_row[h] + i_row[h], axis=-1, keepdims=True)) for h in heads]
        kw = [(k[h].astype(F32) * (jnp.exp(b_last[h] - b_col[h] + i_col[h] - m_new[h]) * QK_SCALE)).astype(BF16)
              for h in heads]
        cts = [jnp.exp(b_last[h] + ms[h] - m_new[h]) * cts[h]
               + lax.dot_general(v_aug[h], kw[h], TN_DIMS, preferred_element_type=F32) for h in heads]
        ms = m_new
    for h in heads:
        ct_scr[h] = cts[h]
        m_scr[h:h + 1, :] = jnp.broadcast_to(ms[h], (1, LANES))


def _mlstm(proj, gr, gcol, norm_w, *, batch, seq):
    m = proj.shape[0]
    nc = seq // MLSTM_STEP
    row = lambda b, c: b * nc + c
    return pl.pallas_call(
        _mlstm_body,
        out_shape=jax.ShapeDtypeStruct((m, MLSTM_W), BF16),
        grid=(batch, nc),
        in_specs=[
            pl.BlockSpec((MLSTM_STEP, MLSTM_W), lambda b, c: (row(b, c), 0)),
            pl.BlockSpec((MLSTM_STEP, MLSTM_W), lambda b, c: (row(b, c), 1)),
            pl.BlockSpec((MLSTM_STEP, MLSTM_W), lambda b, c: (row(b, c), 2)),
            pl.BlockSpec((MLSTM_STEP, MLSTM_W), lambda b, c: (row(b, c), 3)),
            pl.BlockSpec((SUBLANES, MLSTM_STEP), lambda b, c: (0, row(b, c))),
            pl.BlockSpec((MLSTM_STEP, GATE_ROWS), lambda b, c: (row(b, c), 0)),
            pl.BlockSpec((1, MLSTM_W), lambda b, c: (0, 0)),
        ],
        out_specs=pl.BlockSpec((MLSTM_STEP, MLSTM_W), lambda b, c: (row(b, c), 0)),
        scratch_shapes=[
            pltpu.VMEM((MLSTM_HEADS, 2 * HEAD_DIM, HEAD_DIM), F32),
            pltpu.VMEM((SUBLANES, LANES), F32),
        ],
        compiler_params=_params(("parallel", "arbitrary")),
        name="mlstm",
    )(proj, proj, proj, proj, gr, gcol, norm_w)


def _fox_body(q_ref, k_ref, v_ref, c_ref, nw_ref, y_ref, acc_scr, m_scr, l_scr, q_scr, cq_scr):
    qi = pl.program_id(2)
    tq = q_ref.shape[0]
    nsub = tq // FOX_SUB
    ntile = tq // LANES
    tiles_per_sub = FOX_SUB // LANES

    q_scr[...] = (q_ref[...].astype(F32) * (QK_SCALE * LOG2E)).astype(BF16)
    c_q_row = c_ref[qi] * LOG2E
    eye = (lax.broadcasted_iota(jnp.int32, (LANES, LANES), 0)
           == lax.broadcasted_iota(jnp.int32, (LANES, LANES), 1))
    for j in range(ntile):
        lanes = slice(j * LANES, (j + 1) * LANES)
        col = jnp.sum(jnp.where(eye, c_q_row[:, lanes], 0.0), axis=-1, keepdims=True)
        cq_scr[lanes, :] = jnp.broadcast_to(col, (LANES, LANES))
    m_scr[...] = jnp.full(m_scr.shape, -jnp.inf, F32)
    l_scr[...] = jnp.zeros(l_scr.shape, F32)
    acc_scr[...] = jnp.zeros(acc_scr.shape, F32)

    def logits(kb, diagonal):
        base = pl.multiple_of(kb * tq, tq)
        c_k = c_ref[kb] * LOG2E
        widths = [((s + 1) * FOX_SUB if diagonal else tq) for s in range(nsub)]
        zs = []
        if not diagonal:
            z_all = lax.dot_general(q_scr[...], k_ref[pl.ds(base, tq), :], NT_DIMS,
                                    preferred_element_type=F32) - c_k
        for s in range(nsub):
            if diagonal:
                k = k_ref[pl.ds(base, widths[s]), :]
                z = lax.dot_general(q_scr[s * FOX_SUB:(s + 1) * FOX_SUB, :], k, NT_DIMS,
                                    preferred_element_type=F32) - c_k[:, :widths[s]]
            else:
                z = z_all[s * FOX_SUB:(s + 1) * FOX_SUB, :]
            tiles = [z[:, j * LANES:(j + 1) * LANES] for j in range(widths[s] // LANES)]
            if diagonal:
                for t in range(tiles_per_sub):
                    j = s * tiles_per_sub + t
                    keep = (lax.broadcasted_iota(jnp.int32, (FOX_SUB, LANES), 0)
                            >= lax.broadcasted_iota(jnp.int32, (FOX_SUB, LANES), 1) + t * LANES)
                    tiles[j] = jnp.where(keep, tiles[j], -jnp.inf)
            zs.append(tiles)
        return zs

    def update(kb, zs, diagonal):
        base = pl.multiple_of(kb * tq, tq)
        widths = [((s + 1) * FOX_SUB if diagonal else tq) for s in range(nsub)]
        ps, alphas = [], []
        for s in range(nsub):
            rows = slice(s * FOX_SUB, (s + 1) * FOX_SUB)
            tiles = zs[s]
            mx = tiles[0]
            for t in tiles[1:]:
                mx = jnp.maximum(mx, t)
            c_q = cq_scr[rows, :]
            m_prev = m_scr[rows, :]
            m_new = jnp.maximum(m_prev, jnp.max(mx, axis=-1, keepdims=True) + c_q)
            alpha = jnp.exp2(m_prev - m_new)
            shift = m_new - c_q
            p_tiles = [jnp.exp2(t - shift) for t in tiles]
            p_sum = p_tiles[0]
            for t in p_tiles[1:]:
                p_sum = p_sum + t
            l_scr[rows, :] = alpha * l_scr[rows, :] + p_sum
            m_scr[rows, :] = m_new
            ps.append(jnp.concatenate(p_tiles, axis=1).astype(BF16))
            alphas.append(alpha)
        if diagonal:
            for s in range(nsub):
                rows = slice(s * FOX_SUB, (s + 1) * FOX_SUB)
                v = v_ref[pl.ds(base, widths[s]), :]
                acc_scr[rows, :] = alphas[s] * acc_scr[rows, :] + jnp.dot(ps[s], v, preferred_element_type=F32)
        else:
            pv = jnp.dot(jnp.concatenate(ps, axis=0), v_ref[pl.ds(base, tq), :], preferred_element_type=F32)
            acc_scr[...] = jnp.concatenate(alphas, axis=0) * acc_scr[...] + pv

    def blocks(first, count):
        zs = [logits(first + j, False) for j in range(count)]
        for j in range(count):
            update(first + j, zs[j], False)

    def group(g, carry):
        blocks(g * FOX_GROUP, FOX_GROUP)
        return carry

    lax.fori_loop(0, qi // FOX_GROUP, group, 0)
    done = (qi // FOX_GROUP) * FOX_GROUP
    size = FOX_GROUP // 2
    while size >= 1:
        take = (qi & size) != 0
        pl.when(take)(functools.partial(blocks, done, size))
        done = done + jnp.where(take, size, 0)
        size //= 2
    update(qi, logits(qi, True), True)
    out = acc_scr[...] / jnp.sum(l_scr[...], axis=-1, keepdims=True)
    y_ref[...] = (_rms(out) * nw_ref[...]).astype(BF16)


def _fox(proj, c_rows, norm_w, *, batch, seq):
    m = proj.shape[0]
    nq = seq // FOX_BLOCK
    q_col = (4 * MLSTM_W) // HEAD_DIM
    k_col = q_col + FOX_HEADS
    v_col = k_col + FOX_HEADS
    return pl.pallas_call(
        _fox_body,
        out_shape=jax.ShapeDtypeStruct((m, FOX_W), BF16),
        grid=(batch, FOX_HEADS, nq),
        in_specs=[
            pl.BlockSpec((FOX_BLOCK, HEAD_DIM), lambda b, h, i: (b * nq + i, q_col + h)),
            pl.BlockSpec((seq, HEAD_DIM), lambda b, h, i: (b, k_col + h)),
            pl.BlockSpec((seq, HEAD_DIM), lambda b, h, i: (b, v_col + h)),
            pl.BlockSpec((None, nq, 1, FOX_BLOCK), lambda b, h, i: (h, b, 0, 0)),
            pl.BlockSpec((1, HEAD_DIM), lambda b, h, i: (0, h)),
        ],
        out_specs=pl.BlockSpec((FOX_BLOCK, HEAD_DIM), lambda b, h, i: (b * nq + i, h)),
        scratch_shapes=[
            pltpu.VMEM((FOX_BLOCK, HEAD_DIM), F32),
            pltpu.VMEM((FOX_BLOCK, LANES), F32),
            pltpu.VMEM((FOX_BLOCK, LANES), F32),
            pltpu.VMEM((FOX_BLOCK, HEAD_DIM), BF16),
            pltpu.VMEM((FOX_BLOCK, LANES), F32),
        ],
        compiler_params=_params(("parallel", "parallel", "arbitrary")),
        name="fox",
    )(proj, proj, proj, c_rows, norm_w)


def _mm(x, y):
    return jnp.dot(x.astype(BF16), y.astype(BF16), preferred_element_type=F32)


def _unit_lower_inverses(a_mats, mask_ref):
    eye = mask_ref[MASK_INCL] - mask_ref[MASK_STRICT]
    negs = [-(a * mask_ref[MASK_DIAG8]) for a in a_mats]
    invs = [eye + n for n in negs]
    for _ in range(2):
        negs = [_mm(n, n) for n in negs]
        invs = [i + _mm(i, n) for i, n in zip(invs, negs)]
    for level in range(MASK_OFF0, MASK_COUNT):
        tmps = [_mm(a * mask_ref[level], i) for a, i in zip(a_mats, invs)]
        invs = [i - _mm(i, t) for i, t in zip(invs, tmps)]
    return invs


def _gdn_body(q_ref, k_ref, v_ref, z_ref, cw_ref, gr_ref, gc_ref, nw_ref, y_ref, s_scr, ext_scr, mask_scr):
    step = q_ref.shape[0]
    halo = SUBLANES
    nchunk = step // CHUNK

    @pl.when(pl.program_id(1) == 0)
    def _():
        s_scr[...] = jnp.zeros(s_scr.shape, F32)
        ext_scr[0:halo, :] = jnp.zeros((halo, 3 * GDN_W), F32)
        ri = lax.broadcasted_iota(jnp.int32, (CHUNK, CHUNK), 0)
        ci = lax.broadcasted_iota(jnp.int32, (CHUNK, CHUNK), 1)
        mask_scr[MASK_STRICT] = jnp.where(ri > ci, 1.0, 0.0)
        mask_scr[MASK_INCL] = jnp.where(ri >= ci, 1.0, 0.0)
        mask_scr[MASK_DIAG8] = jnp.where((ri >> 3) == (ci >> 3), 1.0, 0.0)
        for bits in range(3, 3 + MASK_COUNT - MASK_OFF0):
            sibling = ((ri >> (bits + 1)) == (ci >> (bits + 1))) & ((ri >> bits) != (ci >> bits))
            mask_scr[MASK_OFF0 + bits - 3] = jnp.where(sibling, 1.0, 0.0)

    ext_scr[halo:halo + step, 0:GDN_W] = q_ref[...].astype(F32)
    ext_scr[halo:halo + step, GDN_W:2 * GDN_W] = k_ref[...].astype(F32)
    ext_scr[halo:halo + step, 2 * GDN_W:3 * GDN_W] = v_ref[...].astype(F32)
    conv = jnp.zeros((step, 3 * GDN_W), F32)
    for j in range(CONV_WIDTH):
        start = halo - (CONV_WIDTH - 1) + j
        conv = conv + ext_scr[start:start + step, :] * cw_ref[j:j + 1, :]
    ext_scr[0:halo, :] = ext_scr[step:step + halo, :]
    qkv = conv * _sigmoid(conv)

    def l2n(t):
        return t * lax.rsqrt(jnp.sum(t * t, axis=-1, keepdims=True) + NORM_EPS)

    groups = [(h, r) for h in range(GDN_HEADS) for r in range(nchunk)]
    prep = {}
    for h, r in groups:
        rows = slice(r * CHUNK, (r + 1) * CHUNK)
        q = l2n(qkv[rows, h * HEAD_DIM:(h + 1) * HEAD_DIM]) * QK_SCALE
        k = l2n(qkv[rows, GDN_W + h * HEAD_DIM:GDN_W + (h + 1) * HEAD_DIM])
        v = qkv[rows, 2 * GDN_W + h * HEAD_DIM:2 * GDN_W + (h + 1) * HEAD_DIM]
        g_row = gr_ref[h:h + 1, rows]
        g_col = gc_ref[rows, 16 + h:17 + h]
        beta = gc_ref[rows, 20 + h:21 + h]
        incl = mask_scr[MASK_INCL]
        decay = jnp.exp((g_col - g_row) * incl) * incl
        kb = k * beta
        e_g = jnp.exp(g_col)
        g_last = g_row[:, CHUNK - 1:CHUNK]
        kk_qk = lax.dot_general(jnp.concatenate([kb, q], axis=0).astype(BF16), k.astype(BF16), NT_DIMS,
                                preferred_element_type=F32)
        prep[h, r] = dict(
            a=kk_qk[:CHUNK] * (decay * mask_scr[MASK_STRICT]),
            attn=kk_qk[CHUNK:] * decay,
            rhs=jnp.concatenate([v * beta, kb * e_g], axis=1),
            q_dec=q * e_g,
            k_end=k * jnp.exp(g_last - g_col),
            s_decay=jnp.exp(g_last),
        )
    invs = _unit_lower_inverses([prep[g]["a"] for g in groups], mask_scr)
    for g, inv in zip(groups, invs):
        w = _mm(inv, prep[g]["rhs"])
        prep[g]["w_val"], prep[g]["w_key"] = w[:, :HEAD_DIM], w[:, HEAD_DIM:]

    states = [s_scr[h] for h in range(GDN_HEADS)]
    for r in range(nchunk):
        rows = slice(r * CHUNK, (r + 1) * CHUNK)
        ks = [_mm(jnp.concatenate([prep[h, r]["w_key"], prep[h, r]["q_dec"]], axis=0), states[h])
              for h in range(GDN_HEADS)]
        us = [prep[h, r]["w_val"] - ks[h][:CHUNK] for h in range(GDN_HEADS)]
        outs = [ks[h][CHUNK:] + _mm(prep[h, r]["attn"], us[h]) for h in range(GDN_HEADS)]
        states = [prep[h, r]["s_decay"] * states[h]
                  + lax.dot_general(prep[h, r]["k_end"].astype(BF16), us[h].astype(BF16), TN_DIMS,
                                    preferred_element_type=F32) for h in range(GDN_HEADS)]
        for h in range(GDN_HEADS):
            cols = slice(h * HEAD_DIM, (h + 1) * HEAD_DIM)
            gate = z_ref[rows, cols].astype(F32)
            y_ref[rows, cols] = (_rms(outs[h]) * nw_ref[:, cols] * (gate * _sigmoid(gate))).astype(BF16)
    for h in range(GDN_HEADS):
        s_scr[h] = states[h]


def _gdn(proj, conv_w, gr, gcol, norm_w, *, batch, seq):
    m = proj.shape[0]
    ns = seq // GDN_STEP
    col0 = (4 * MLSTM_W + 3 * FOX_W) // GDN_W
    row = lambda b, s: b * ns + s
    return pl.pallas_call(
        _gdn_body,
        out_shape=jax.ShapeDtypeStruct((m, GDN_W), BF16),
        grid=(batch, ns),
        in_specs=[
            pl.BlockSpec((GDN_STEP, GDN_W), lambda b, s: (row(b, s), col0)),
            pl.BlockSpec((GDN_STEP, GDN_W), lambda b, s: (row(b, s), col0 + 1)),
            pl.BlockSpec((GDN_STEP, GDN_W), lambda b, s: (row(b, s), col0 + 2)),
            pl.BlockSpec((GDN_STEP, GDN_W), lambda b, s: (row(b, s), col0 + 3)),
            pl.BlockSpec((CONV_WIDTH, 3 * GDN_W), lambda b, s: (0, 0)),
            pl.BlockSpec((SUBLANES, GDN_STEP), lambda b, s: (2, row(b, s))),
            pl.BlockSpec((GDN_STEP, GATE_ROWS), lambda b, s: (row(b, s), 0)),
            pl.BlockSpec((1, GDN_W), lambda b, s: (0, 0)),
        ],
        out_specs=pl.BlockSpec((GDN_STEP, GDN_W), lambda b, s: (row(b, s), 0)),
        scratch_shapes=[
            pltpu.VMEM((GDN_HEADS, HEAD_DIM, HEAD_DIM), F32),
            pltpu.VMEM((GDN_STEP + 2 * SUBLANES, 3 * GDN_W), F32),
            pltpu.VMEM((MASK_COUNT, CHUNK, CHUNK), F32),
        ],
        compiler_params=_params(("parallel", "arbitrary")),
        name="gdn",
    )(proj, proj, proj, proj, conv_w, gr, gcol, norm_w)


def _outproj_body(ym_ref, yf_ref, yg_ref, w_ref, x_ref, o_ref):
    acc = jnp.dot(ym_ref[...], w_ref[0:MLSTM_W, :], preferred_element_type=F32)
    acc += jnp.dot(yf_ref[...], w_ref[MLSTM_W:MLSTM_W + FOX_W, :], preferred_element_type=F32)
    acc += jnp.dot(yg_ref[...], w_ref[MLSTM_W + FOX_W:, :], preferred_element_type=F32)
    o_ref[...] = x_ref[...] + acc


def _outproj(y_m, y_f, y_g, w_out, x, *, tm=512):
    m = x.shape[0]
    return pl.pallas_call(
        _outproj_body,
        out_shape=jax.ShapeDtypeStruct((m, D_MODEL), F32),
        grid=(m // tm,),
        in_specs=[
            pl.BlockSpec((tm, MLSTM_W), lambda i: (i, 0)),
            pl.BlockSpec((tm, FOX_W), lambda i: (i, 0)),
            pl.BlockSpec((tm, GDN_W), lambda i: (i, 0)),
            pl.BlockSpec((D_MODEL, D_MODEL), lambda i: (0, 0)),
            pl.BlockSpec((tm, D_MODEL), lambda i: (i, 0)),
        ],
        out_specs=pl.BlockSpec((tm, D_MODEL), lambda i: (i, 0)),
        compiler_params=_params(("parallel",)),
        name="outproj",
    )(y_m, y_f, y_g, w_out, x)


def _ffn_body(x_ref, nw_ref, wg_ref, wu_ref, wd_ref, fw_ref, o_ref, h_scr, acc_scr, *, final_norm):
    j = pl.program_id(1)

    @pl.when(j == 0)
    def _():
        h_scr[...] = (_rms(x_ref[...]) * nw_ref[...]).astype(BF16)
        acc_scr[...] = jnp.zeros(acc_scr.shape, F32)

    h = h_scr[...]
    gate = jnp.dot(h, wg_ref[...], preferred_element_type=F32)
    up = jnp.dot(h, wu_ref[...], preferred_element_type=F32)
    act = (gate * _sigmoid(gate) * up).astype(BF16)
    acc_scr[...] += jnp.dot(act, wd_ref[...], preferred_element_type=F32)

    @pl.when(j == pl.num_programs(1) - 1)
    def _():
        out = x_ref[...] + acc_scr[...]
        if final_norm:
            out = _rms(out) * fw_ref[...]
        o_ref[...] = out


def _ffn(x, norm_w, w_gate, w_up, w_down, final_w, *, final_norm, tm=512, tf=512):
    m = x.shape[0]
    return pl.pallas_call(
        functools.partial(_ffn_body, final_norm=final_norm),
        out_shape=jax.ShapeDtypeStruct((m, D_MODEL), F32),
        grid=(m // tm, D_FF // tf),
        in_specs=[
            pl.BlockSpec((tm, D_MODEL), lambda i, j: (i, 0)),
            pl.BlockSpec((1, D_MODEL), lambda i, j: (0, 0)),
            pl.BlockSpec((D_MODEL, tf), lambda i, j: (0, j)),
            pl.BlockSpec((D_MODEL, tf), lambda i, j: (0, j)),
            pl.BlockSpec((tf, D_MODEL), lambda i, j: (j, 0)),
            pl.BlockSpec((1, D_MODEL), lambda i, j: (0, 0)),
        ],
        out_specs=pl.BlockSpec((tm, D_MODEL), lambda i, j: (i, 0)),
        scratch_shapes=[pltpu.VMEM((tm, D_MODEL), BF16), pltpu.VMEM((tm, D_MODEL), F32)],
        compiler_params=_params(("parallel", "arbitrary")),
        name="ffn",
    )(x, norm_w, w_gate, w_up, w_down, final_w)


def _split_w_in(w):
    o_mi = 4 * MLSTM_W
    o_fq = o_mi + 2 * MLSTM_HEADS
    o_ff = o_fq + 3 * FOX_W
    o_gq = o_ff + FOX_HEADS
    o_ga = o_gq + 4 * GDN_W
    wide = jnp.concatenate([w[:, :o_mi], w[:, o_fq:o_ff], w[:, o_gq:o_ga]], axis=1).astype(BF16)
    gate = jnp.concatenate([w[:, o_mi:o_fq], w[:, o_ff:o_gq], w[:, o_ga:]], axis=1)
    gate_t = jnp.pad(gate.T, ((0, GATE_ROWS - gate.shape[1]), (0, 0))).astype(BF16)
    return wide, gate_t


def kernel(x, mix_norm_w, w_in, mlstm_i_bias, mlstm_f_bias, fox_f_bias, gdn_conv_w, gdn_a_log, gdn_dt_bias,
           mlstm_out_norm_w, fox_out_norm_w, gdn_out_norm_w, w_out, ffn_norm_w, w_gate, w_up, w_down,
           final_norm_w):
    batch, seq, d_model = x.shape
    depth = w_in.shape[0]
    assert d_model == D_MODEL and seq % FOX_BLOCK == 0 and seq % GDN_STEP == 0
    xf = x.reshape(batch * seq, d_model)
    zeros4 = jnp.zeros((GDN_HEADS,), F32)
    for l in range(depth):
        w_wide, w_gate_t = _split_w_in(w_in[l])
        bias_col = jnp.pad(jnp.concatenate([mlstm_i_bias[l], mlstm_f_bias[l], fox_f_bias[l], gdn_dt_bias[l],
                                            zeros4]), (0, 8)).reshape(GATE_ROWS, 1)
        alog_col = jnp.pad(gdn_a_log[l], (16, 12)).reshape(GATE_ROWS, 1)

        proj, gt = _inproj(xf, mix_norm_w[l].reshape(1, -1), w_wide, w_gate_t)
        gr = _gates(gt, bias_col, alog_col, batch=batch, seq=seq)
        gcol = gr.T
        c_rows = gr[8:16].reshape(FOX_HEADS, batch * (seq // FOX_BLOCK), 1, FOX_BLOCK)

        y_m = _mlstm(proj, gr, gcol, mlstm_out_norm_w[l].reshape(1, -1), batch=batch, seq=seq)
        y_f = _fox(proj, c_rows, fox_out_norm_w[l].reshape(1, -1), batch=batch, seq=seq)
        y_g = _gdn(proj, gdn_conv_w[l], gr, gcol, gdn_out_norm_w[l].reshape(1, -1), batch=batch, seq=seq)
        xf = _outproj(y_m, y_f, y_g, w_out[l].astype(BF16), xf)
        xf = _ffn(xf, ffn_norm_w[l].reshape(1, -1), w_gate[l].astype(BF16), w_up[l].astype(BF16),
                  w_down[l].astype(BF16), final_norm_w.reshape(1, -1), final_norm=(l == depth - 1))
    return xf.reshape(batch, seq, d_model)
```

```python
import functools

import jax
import jax.numpy as jnp
import numpy as np
from jax import lax
from jax.experimental import pallas as pl
from jax.experimental.pallas import tpu as pltpu

F32 = jnp.float32
BF16 = jnp.bfloat16

D_MODEL = 2048
HEAD_DIM = 128
MLSTM_HEADS = 4
FOX_HEADS = 8
GDN_HEADS = 4
MLSTM_W = MLSTM_HEADS * HEAD_DIM
FOX_W = FOX_HEADS * HEAD_DIM
GDN_W = GDN_HEADS * HEAD_DIM
D_FF = 5632
CONV_WIDTH = 4
GATE_SOFTCAP = 15.0
NORM_EPS = 1e-6
QK_SCALE = HEAD_DIM ** -0.5
LOG2E = 1.4426950408889634

PROJ_W = 4 * MLSTM_W + 3 * FOX_W + 4 * GDN_W
GATE_ROWS = 32

LANES = 128
SUBLANES = 8
V7X_VMEM_BYTES = 64 * 1024 * 1024
VMEM_LIMIT = 56 * 1024 * 1024

CHUNK = 128
FOX_BLOCK = 512
FOX_SUB = 128
FOX_GROUP = 4
GDN_STEP = 512
MLSTM_STEP = 512

MASK_STRICT, MASK_INCL, MASK_DIAG8, MASK_OFF0 = 0, 1, 2, 3
MASK_COUNT = MASK_OFF0 + 4

NT_DIMS = (((1,), (1,)), ((), ()))
TN_DIMS = (((0,), (0,)), ((), ()))


def _params(sem):
    return pltpu.CompilerParams(dimension_semantics=sem, vmem_limit_bytes=VMEM_LIMIT)


def _sigmoid(z):
    return 1.0 / (1.0 + jnp.exp(-z))


def _log_sigmoid(z):
    return jnp.minimum(z, 0.0) - jnp.log1p(jnp.exp(-jnp.abs(z)))


def _softplus(z):
    return jnp.maximum(z, 0.0) + jnp.log1p(jnp.exp(-jnp.abs(z)))


def _rms(t):
    return t * lax.rsqrt(jnp.mean(t * t, axis=-1, keepdims=True) + NORM_EPS)


def _inproj_body(x_ref, nw_ref, w_ref, wg_ref, proj_ref, gt_ref, h_scr, *, row_chunk):
    @pl.when(pl.program_id(1) == 0)
    def _():
        for r in range(x_ref.shape[0] // row_chunk):
            rows = slice(r * row_chunk, (r + 1) * row_chunk)
            h_scr[rows, :] = (_rms(x_ref[rows, :]) * nw_ref[...]).astype(BF16)
        gates_t = jnp.dot(h_scr[...], wg_ref[...], preferred_element_type=F32).T
        gt_ref[0:16, :] = gates_t[0:16]
        gt_ref[16:24, :] = gates_t[LANES - SUBLANES:LANES]
        gt_ref[24:32, :] = jnp.zeros((SUBLANES, gates_t.shape[1]), F32)

    proj_ref[...] = jnp.dot(h_scr[...], w_ref[...], preferred_element_type=F32).astype(BF16)


def _inproj(x, norm_w, w_big, w_gate_t, *, tm=1024, tn=1024):
    m = x.shape[0]
    return pl.pallas_call(
        functools.partial(_inproj_body, row_chunk=256),
        out_shape=(jax.ShapeDtypeStruct((m, PROJ_W), BF16), jax.ShapeDtypeStruct((GATE_ROWS, m), F32)),
        grid=(m // tm, PROJ_W // tn),
        in_specs=[
            pl.BlockSpec((tm, D_MODEL), lambda i, j: (i, 0)),
            pl.BlockSpec((1, D_MODEL), lambda i, j: (0, 0)),
            pl.BlockSpec((D_MODEL, tn), lambda i, j: (0, j)),
            pl.BlockSpec((D_MODEL, LANES), lambda i, j: (0, 0)),
        ],
        out_specs=(
            pl.BlockSpec((tm, tn), lambda i, j: (i, j)),
            pl.BlockSpec((GATE_ROWS, tm), lambda i, j: (0, i)),
        ),
        scratch_shapes=[pltpu.VMEM((tm, D_MODEL), BF16)],
        compiler_params=_params(("parallel", "arbitrary")),
        name="inproj",
    )(x, norm_w, w_big, w_gate_t)


def _lane_cumsum(v, seg):
    pos = lax.broadcasted_iota(jnp.int32, v.shape, 1) & (seg - 1)
    shift = 1
    while shift < seg:
        v = v + jnp.where(pos >= shift, pltpu.roll(v, shift, 1), 0.0)
        shift *= 2
    return v


def _lane_cummax(v, seg):
    pos = lax.broadcasted_iota(jnp.int32, v.shape, 1) & (seg - 1)
    shift = 1
    while shift < seg:
        v = jnp.maximum(v, jnp.where(pos >= shift, pltpu.roll(v, shift, 1), -jnp.inf))
        shift *= 2
    return v


def _gates_body(gt_ref, bias_ref, alog_ref, out_ref, split_ref):
    seq = gt_ref.shape[1]
    first4 = lax.broadcasted_iota(jnp.int32, (SUBLANES, seq), 0) < 4
    z = gt_ref[0:8, :] + bias_ref[0:8, :]
    z = GATE_SOFTCAP * jnp.tanh(z / GATE_SOFTCAP)
    b_cum = _lane_cumsum(_log_sigmoid(z), CHUNK)
    out_ref[0:8, :] = jnp.where(first4, z, b_cum)
    d = z - pltpu.roll(b_cum, 4, 0)
    out_ref[24:32, :] = jnp.where(first4, d, pltpu.roll(_lane_cummax(d, CHUNK), 4, 0))

    raw = gt_ref[16:24, :]
    decay = _lane_cumsum(-jnp.exp(alog_ref[16:24, :]) * _softplus(raw + bias_ref[16:24, :]), CHUNK)
    out_ref[16:24, :] = jnp.where(first4, decay, _sigmoid(raw))

    fox = _lane_cumsum(_log_sigmoid(gt_ref[8:16, :] + bias_ref[8:16, :]), LANES)
    carry = jnp.zeros((SUBLANES, 1), F32)
    for blk in range(seq // LANES):
        lanes = slice(blk * LANES, (blk + 1) * LANES)
        tile = fox[:, lanes] + carry
        out_ref[8:16, lanes] = tile
        carry = tile[:, LANES - 1:LANES]

    value = out_ref[...]
    hi = value.astype(BF16)
    rest = value - hi.astype(F32)
    mid = rest.astype(BF16)
    split_ref[0:GATE_ROWS, :] = hi
    split_ref[GATE_ROWS:2 * GATE_ROWS, :] = mid
    split_ref[2 * GATE_ROWS:3 * GATE_ROWS, :] = (rest - mid.astype(F32)).astype(BF16)
    split_ref[3 * GATE_ROWS:, :] = jnp.zeros((LANES - 3 * GATE_ROWS, seq), BF16)


def _gates(gt, bias_col, alog_col, *, batch, seq):
    return pl.pallas_call(
        _gates_body,
        out_shape=(jax.ShapeDtypeStruct(gt.shape, F32), jax.ShapeDtypeStruct((LANES, gt.shape[1]), BF16)),
        grid=(batch,),
        in_specs=[
            pl.BlockSpec((GATE_ROWS, seq), lambda b: (0, b)),
            pl.BlockSpec((GATE_ROWS, 1), lambda b: (0, 0)),
            pl.BlockSpec((GATE_ROWS, 1), lambda b: (0, 0)),
        ],
        out_specs=(pl.BlockSpec((GATE_ROWS, seq), lambda b: (0, b)), pl.BlockSpec((LANES, seq), lambda b: (0, b))),
        compiler_params=_params(("parallel",)),
        name="gates",
    )(gt, bias_col, alog_col)


def _mlstm_body(q_ref, k_ref, v_ref, o_ref, gb_ref, gd_ref, gc3_ref, sel_ref, mean_ref, nw_ref, y_ref,
                ct_scr, m_scr):
    @pl.when(pl.program_id(1) == 0)
    def _():
        ct_scr[...] = jnp.zeros(ct_scr.shape, F32)
        m_scr[...] = jnp.zeros(m_scr.shape, F32)

    heads = range(MLSTM_HEADS)
    causal = (lax.broadcasted_iota(jnp.int32, (CHUNK, CHUNK), 0)
              >= lax.broadcasted_iota(jnp.int32, (CHUNK, CHUNK), 1))
    ones = jnp.ones((CHUNK, HEAD_DIM), BF16)
    cts = [ct_scr[h] for h in heads]
    ms = [m_scr[h:h + 1, 0:1] for h in heads]
    for r in range(q_ref.shape[0] // CHUNK):
        rows = slice(r * CHUNK, (r + 1) * CHUNK)
        hcols = [slice(h * HEAD_DIM, (h + 1) * HEAD_DIM) for h in heads]
        q = [q_ref[rows, c] for c in hcols]
        k = [k_ref[rows, c] for c in hcols]
        v_aug = [jnp.concatenate([v_ref[rows, c], ones], axis=1) for c in hcols]
        cols_rep = jnp.dot(gc3_ref[rows, :], sel_ref[...], preferred_element_type=F32)
        rep = lambda h, j: cols_rep[:, (3 * h + j) * LANES:(3 * h + j + 1) * LANES]
        a_col = [rep(h, 0) for h in heads]
        d_col = [rep(h, 1) for h in heads]
        b_col = [rep(h, 2) for h in heads]
        d_row = [gd_ref[h:h + 1, rows] for h in heads]
        b_last = [gb_ref[4 + h:5 + h, rows][:, CHUNK - 1:CHUNK] for h in heads]
        a_last = [gd_ref[4 + h:5 + h, rows][:, CHUNK - 1:CHUNK] for h in heads]

        qk = [lax.dot_general(q[h], k[h], NT_DIMS, preferred_element_type=F32) for h in heads]
        q_state = [lax.dot_general(q[h], cts[h].astype(BF16), NT_DIMS, preferred_element_type=F32)
                   for h in heads]
        m_rel = [jnp.maximum(ms[h], a_col[h]) for h in heads]
        p = [jnp.exp(jnp.where(causal, d_row[h] - m_rel[h], -jnp.inf)) * (qk[h] * QK_SCALE) for h in heads]
        intra = [jnp.dot(p[h].astype(BF16), v_aug[h], preferred_element_type=F32) for h in heads]
        for h in heads:
            w_inter = jnp.exp(ms[h] - m_rel[h])
            both = jnp.concatenate([w_inter, w_inter], axis=1) * q_state[h] + intra[h]
            floor = jnp.exp(-(b_col[h] + m_rel[h]))
            h_out = both[:, :HEAD_DIM] / jnp.maximum(jnp.abs(both[:, HEAD_DIM:]), floor)
            mean = jnp.dot(h_out.astype(BF16), mean_ref[...], preferred_element_type=F32)
            centered = h_out - mean
            var = jnp.dot((centered * centered).astype(BF16), mean_ref[...], preferred_element_type=F32)
            gate = _sigmoid(o_ref[rows, hcols[h]].astype(F32))
            y_ref[rows, hcols[h]] = (centered * lax.rsqrt(var + NORM_EPS) * nw_ref[:, hcols[h]]
                                     * gate).astype(BF16)

        m_end = [jnp.maximum(ms[h], a_last[h]) for h in heads]
        kw = [(k[h].astype(F32) * (jnp.exp(d_col[h] - m_end[h]) * QK_SCALE)).astype(BF16) for h in heads]
        cts = [jnp.exp(ms[h] - m_end[h]) * cts[h]
               + lax.dot_general(v_aug[h], kw[h], TN_DIMS, preferred_element_type=F32) for h in heads]
        ms = [b_last[h] + m_end[h] for h in heads]
    for h in heads:
        ct_scr[h] = cts[h]
        m_scr[h:h + 1, :] = jnp.broadcast_to(ms[h], (1, LANES))


def _one_hot_columns(gate_cols):
    sel = np.zeros((LANES, LANES * len(gate_cols)), np.float32)
    for j, col in enumerate(gate_cols):
        for part in range(3):
            sel[part * GATE_ROWS + col, j * LANES:(j + 1) * LANES] = 1.0
    return jnp.asarray(sel, BF16)


def _mlstm(proj, gr, gc3, norm_w, *, batch, seq):
    m = proj.shape[0]
    nc = seq // MLSTM_STEP
    row = lambda b, c: b * nc + c
    sel = _one_hot_columns([c for h in range(MLSTM_HEADS) for c in (28 + h, 24 + h, 4 + h)])
    mean_w = jnp.full((HEAD_DIM, HEAD_DIM), 1.0 / HEAD_DIM, BF16)
    return pl.pallas_call(
        _mlstm_body,
        out_shape=jax.ShapeDtypeStruct((m, MLSTM_W), BF16),
        grid=(batch, nc),
        in_specs=[
            pl.BlockSpec((MLSTM_STEP, MLSTM_W), lambda b, c: (row(b, c), 0)),
            pl.BlockSpec((MLSTM_STEP, MLSTM_W), lambda b, c: (row(b, c), 1)),
            pl.BlockSpec((MLSTM_STEP, MLSTM_W), lambda b, c: (row(b, c), 2)),
            pl.BlockSpec((MLSTM_STEP, MLSTM_W), lambda b, c: (row(b, c), 3)),
            pl.BlockSpec((SUBLANES, MLSTM_STEP), lambda b, c: (0, row(b, c))),
            pl.BlockSpec((SUBLANES, MLSTM_STEP), lambda b, c: (3, row(b, c))),
            pl.BlockSpec((MLSTM_STEP, LANES), lambda b, c: (row(b, c), 0)),
            pl.BlockSpec(sel.shape, lambda b, c: (0, 0)),
            pl.BlockSpec((HEAD_DIM, HEAD_DIM), lambda b, c: (0, 0)),
            pl.BlockSpec((1, MLSTM_W), lambda b, c: (0, 0)),
        ],
        out_specs=pl.BlockSpec((MLSTM_STEP, MLSTM_W), lambda b, c: (row(b, c), 0)),
        scratch_shapes=[
            pltpu.VMEM((MLSTM_HEADS, 2 * HEAD_DIM, HEAD_DIM), F32),
            pltpu.VMEM((SUBLANES, LANES), F32),
        ],
        compiler_params=_params(("parallel", "arbitrary")),
        name="mlstm",
    )(proj, proj, proj, proj, gr, gr, gc3, sel, mean_w, norm_w)


def _fox_body(q_ref, k_ref, v_ref, c_ref, nw_ref, y_ref, acc_scr, m_scr, l_scr, q_scr, cq_scr):
    qi = pl.program_id(2)
    tq = q_ref.shape[0]
    nsub = tq // FOX_SUB
    ntile = tq // LANES
    tiles_per_sub = FOX_SUB // LANES

    q_scr[...] = (q_ref[...].astype(F32) * (QK_SCALE * LOG2E)).astype(BF16)
    c_q_row = c_ref[qi] * LOG2E
    eye = (lax.broadcasted_iota(jnp.int32, (LANES, LANES), 0)
           == lax.broadcasted_iota(jnp.int32, (LANES, LANES), 1))
    for j in range(ntile):
        lanes = slice(j * LANES, (j + 1) * LANES)
        col = jnp.sum(jnp.where(eye, c_q_row[:, lanes], 0.0), axis=-1, keepdims=True)
        cq_scr[lanes, :] = jnp.broadcast_to(col, (LANES, LANES))
    m_scr[...] = jnp.full(m_scr.shape, -jnp.inf, F32)
    l_scr[...] = jnp.zeros(l_scr.shape, F32)
    acc_scr[...] = jnp.zeros(acc_scr.shape, F32)

    def logits(kb, diagonal):
        base = pl.multiple_of(kb * tq, tq)
        c_k = c_ref[kb] * LOG2E
        widths = [((s + 1) * FOX_SUB if diagonal else tq) for s in range(nsub)]
        zs = []
        if not diagonal:
            z_all = lax.dot_general(q_scr[...], k_ref[pl.ds(base, tq), :], NT_DIMS,
                                    preferred_element_type=F32) - c_k
        for s in range(nsub):
            if diagonal:
                k = k_ref[pl.ds(base, widths[s]), :]
                z = lax.dot_general(q_scr[s * FOX_SUB:(s + 1) * FOX_SUB, :], k, NT_DIMS,
                                    preferred_element_type=F32) - c_k[:, :widths[s]]
            else:
                z = z_all[s * FOX_SUB:(s + 1) * FOX_SUB, :]
            tiles = [z[:, j * LANES:(j + 1) * LANES] for j in range(widths[s] // LANES)]
            if diagonal:
                for t in range(tiles_per_sub):
                    j = s * tiles_per_sub + t
                    keep = (lax.broadcasted_iota(jnp.int32, (FOX_SUB, LANES), 0)
                            >= lax.broadcasted_iota(jnp.int32, (FOX_SUB, LANES), 1) + t * LANES)
                    tiles[j] = jnp.where(keep, tiles[j], -jnp.inf)
            zs.append(tiles)
        return zs

    def update(kb, zs, diagonal):
        base = pl.multiple_of(kb * tq, tq)
        widths = [((s + 1) * FOX_SUB if diagonal else tq) for s in range(nsub)]
        ps, alphas = [], []
        for s in range(nsub):
            rows = slice(s * FOX_SUB, (s + 1) * FOX_SUB)
            tiles = zs[s]
            mx = tiles[0]
            for t in tiles[1:]:
                mx = jnp.maximum(mx, t)
            c_q = cq_scr[rows, :]
            m_prev = m_scr[rows, :]
            m_new = jnp.maximum(m_prev, jnp.max(mx, axis=-1, keepdims=True) + c_q)
            alpha = jnp.exp2(m_prev - m_new)
            shift = m_new - c_q
            p_tiles = [jnp.exp2(t - shift) for t in tiles]
            p_sum = p_tiles[0]
            for t in p_tiles[1:]:
                p_sum = p_sum + t
            l_scr[rows, :] = alpha * l_scr[rows, :] + p_sum
            m_scr[rows, :] = m_new
            ps.append(jnp.concatenate(p_tiles, axis=1).astype(BF16))
            alphas.append(alpha)
        if diagonal:
            for s in range(nsub):
                rows = slice(s * FOX_SUB, (s + 1) * FOX_SUB)
                v = v_ref[pl.ds(base, widths[s]), :]
                acc_scr[rows, :] = alphas[s] * acc_scr[rows, :] + jnp.dot(ps[s], v, preferred_element_type=F32)
        else:
            pv = jnp.dot(jnp.concatenate(ps, axis=0), v_ref[pl.ds(base, tq), :], preferred_element_type=F32)
            acc_scr[...] = jnp.concatenate(alphas, axis=0) * acc_scr[...] + pv

    def blocks(first, count):
        zs = [logits(first + j, False) for j in range(count)]
        for j in range(count):
            update(first + j, zs[j], False)

    def group(g, carry):
        blocks(g * FOX_GROUP, FOX_GROUP)
        return carry

    lax.fori_loop(0, qi // FOX_GROUP, group, 0)
    done = (qi // FOX_GROUP) * FOX_GROUP
    size = FOX_GROUP // 2
    while size >= 1:
        take = (qi & size) != 0
        pl.when(take)(functools.partial(blocks, done, size))
        done = done + jnp.where(take, size, 0)
        size //= 2
    update(qi, logits(qi, True), True)
    out = acc_scr[...] / jnp.sum(l_scr[...], axis=-1, keepdims=True)
    y_ref[...] = (_rms(out) * nw_ref[...]).astype(BF16)


def _fox(proj, c_rows, norm_w, *, batch, seq):
    m = proj.shape[0]
    nq = seq // FOX_BLOCK
    q_col = (4 * MLSTM_W) // HEAD_DIM
    k_col = q_col + FOX_HEADS
    v_col = k_col + FOX_HEADS
    return pl.pallas_call(
        _fox_body,
        out_shape=jax.ShapeDtypeStruct((m, FOX_W), BF16),
        grid=(batch, FOX_HEADS, nq),
        in_specs=[
            pl.BlockSpec((FOX_BLOCK, HEAD_DIM), lambda b, h, i: (b * nq + i, q_col + h)),
            pl.BlockSpec((seq, HEAD_DIM), lambda b, h, i: (b, k_col + h)),
            pl.BlockSpec((seq, HEAD_DIM), lambda b, h, i: (b, v_col + h)),
            pl.BlockSpec((None, nq, 1, FOX_BLOCK), lambda b, h, i: (h, b, 0, 0)),
            pl.BlockSpec((1, HEAD_DIM), lambda b, h, i: (0, h)),
        ],
        out_specs=pl.BlockSpec((FOX_BLOCK, HEAD_DIM), lambda b, h, i: (b * nq + i, h)),
        scratch_shapes=[
            pltpu.VMEM((FOX_BLOCK, HEAD_DIM), F32),
            pltpu.VMEM((FOX_BLOCK, LANES), F32),
            pltpu.VMEM((FOX_BLOCK, LANES), F32),
            pltpu.VMEM((FOX_BLOCK, HEAD_DIM), BF16),
            pltpu.VMEM((FOX_BLOCK, LANES), F32),
        ],
        compiler_params=_params(("parallel", "parallel", "arbitrary")),
        name="fox",
    )(proj, proj, proj, c_rows, norm_w)


def _mm(x, y):
    return jnp.dot(x.astype(BF16), y.astype(BF16), preferred_element_type=F32)


def _unit_lower_inverses(a_mats, mask_ref):
    eye = mask_ref[MASK_INCL] - mask_ref[MASK_STRICT]
    negs = [-(a * mask_ref[MASK_DIAG8]) for a in a_mats]
    invs = [eye + n for n in negs]
    for _ in range(2):
        negs = [_mm(n, n) for n in negs]
        invs = [i + _mm(i, n) for i, n in zip(invs, negs)]
    for level in range(MASK_OFF0, MASK_COUNT):
        tmps = [_mm(a * mask_ref[level], i) for a, i in zip(a_mats, invs)]
        invs = [i - _mm(i, t) for i, t in zip(invs, tmps)]
    return invs


def _gdn_body(q_ref, k_ref, v_ref, z_ref, cw_ref, gr_ref, gc3_ref, sel_ref, ones_ref, nw_ref, y_ref,
              s_scr, ext_scr, mask_scr):
    step = q_ref.shape[0]
    halo = SUBLANES
    nchunk = step // CHUNK

    @pl.when(pl.program_id(1) == 0)
    def _():
        s_scr[...] = jnp.zeros(s_scr.shape, F32)
        ext_scr[0:halo, :] = jnp.zeros((halo, 3 * GDN_W), F32)
        ri = lax.broadcasted_iota(jnp.int32, (CHUNK, CHUNK), 0)
        ci = lax.broadcasted_iota(jnp.int32, (CHUNK, CHUNK), 1)
        mask_scr[MASK_STRICT] = jnp.where(ri > ci, 1.0, 0.0)
        mask_scr[MASK_INCL] = jnp.where(ri >= ci, 1.0, 0.0)
        mask_scr[MASK_DIAG8] = jnp.where((ri >> 3) == (ci >> 3), 1.0, 0.0)
        for bits in range(3, 3 + MASK_COUNT - MASK_OFF0):
            sibling = ((ri >> (bits + 1)) == (ci >> (bits + 1))) & ((ri >> bits) != (ci >> bits))
            mask_scr[MASK_OFF0 + bits - 3] = jnp.where(sibling, 1.0, 0.0)

    ext_scr[halo:halo + step, 0:GDN_W] = q_ref[...].astype(F32)
    ext_scr[halo:halo + step, GDN_W:2 * GDN_W] = k_ref[...].astype(F32)
    ext_scr[halo:halo + step, 2 * GDN_W:3 * GDN_W] = v_ref[...].astype(F32)
    ext = ext_scr[...]
    conv = ext * cw_ref[0:1, :]
    for j in range(1, CONV_WIDTH):
        conv = pltpu.roll(conv, 1, 0) + ext * cw_ref[j:j + 1, :]
    conv = conv[halo:, :]
    ext_scr[0:halo, :] = ext_scr[step:step + halo, :]
    qkv = conv * _sigmoid(conv)

    def row_sum(t):
        return jnp.dot(t.astype(BF16), ones_ref[...], preferred_element_type=F32)

    def l2n(t):
        return t * lax.rsqrt(row_sum(t * t) + NORM_EPS)

    cols_rep = [jnp.dot(gc3_ref[r * CHUNK:(r + 1) * CHUNK, :], sel_ref[...], preferred_element_type=F32)
                for r in range(nchunk)]

    groups = [(h, r) for h in range(GDN_HEADS) for r in range(nchunk)]
    prep = {}
    for h, r in groups:
        rows = slice(r * CHUNK, (r + 1) * CHUNK)
        q = l2n(qkv[rows, h * HEAD_DIM:(h + 1) * HEAD_DIM]) * QK_SCALE
        k = l2n(qkv[rows, GDN_W + h * HEAD_DIM:GDN_W + (h + 1) * HEAD_DIM])
        v = qkv[rows, 2 * GDN_W + h * HEAD_DIM:2 * GDN_W + (h + 1) * HEAD_DIM]
        g_row = gr_ref[h:h + 1, rows]
        g_col = cols_rep[r][:, (2 * h) * LANES:(2 * h + 1) * LANES]
        beta = cols_rep[r][:, (2 * h + 1) * LANES:(2 * h + 2) * LANES]
        incl = mask_scr[MASK_INCL]
        decay = jnp.exp((g_col - g_row) * incl) * incl
        kb = k * beta
        e_g = jnp.exp(g_col)
        g_last = g_row[:, CHUNK - 1:CHUNK]
        kk_qk = lax.dot_general(jnp.concatenate([kb, q], axis=0).astype(BF16), k.astype(BF16), NT_DIMS,
                                preferred_element_type=F32)
        prep[h, r] = dict(
            a=kk_qk[:CHUNK] * (decay * mask_scr[MASK_STRICT]),
            attn=kk_qk[CHUNK:] * decay,
            rhs=jnp.concatenate([v * beta, kb * e_g], axis=1),
            q_dec=q * e_g,
            k_end=k * jnp.exp(g_last - g_col),
            s_decay=jnp.exp(g_last),
        )
    invs = _unit_lower_inverses([prep[g]["a"] for g in groups], mask_scr)
    for g, inv in zip(groups, invs):
        w = _mm(inv, prep[g]["rhs"])
        prep[g]["w_val"], prep[g]["w_key"] = w[:, :HEAD_DIM], w[:, HEAD_DIM:]

    states = [s_scr[h] for h in range(GDN_HEADS)]
    for r in range(nchunk):
        rows = slice(r * CHUNK, (r + 1) * CHUNK)
        ks = [_mm(jnp.concatenate([prep[h, r]["w_key"], prep[h, r]["q_dec"]], axis=0), states[h])
              for h in range(GDN_HEADS)]
        us = [prep[h, r]["w_val"] - ks[h][:CHUNK] for h in range(GDN_HEADS)]
        outs = [ks[h][CHUNK:] + _mm(prep[h, r]["attn"], us[h]) for h in range(GDN_HEADS)]
        states = [prep[h, r]["s_decay"] * states[h]
                  + lax.dot_general(prep[h, r]["k_end"].astype(BF16), us[h].astype(BF16), TN_DIMS,
                                    preferred_element_type=F32) for h in range(GDN_HEADS)]
        for h in range(GDN_HEADS):
            cols = slice(h * HEAD_DIM, (h + 1) * HEAD_DIM)
            gate = z_ref[rows, cols].astype(F32)
            normed = outs[h] * lax.rsqrt(row_sum(outs[h] * outs[h]) * (1.0 / HEAD_DIM) + NORM_EPS)
            y_ref[rows, cols] = (normed * nw_ref[:, cols] * (gate * _sigmoid(gate))).astype(BF16)
    for h in range(GDN_HEADS):
        s_scr[h] = states[h]


def _gdn(proj, conv_w, gr, gc3, norm_w, *, batch, seq):
    m = proj.shape[0]
    ns = seq // GDN_STEP
    sel = _one_hot_columns([c for h in range(GDN_HEADS) for c in (16 + h, 20 + h)])
    col0 = (4 * MLSTM_W + 3 * FOX_W) // GDN_W
    row = lambda b, s: b * ns + s
    return pl.pallas_call(
        _gdn_body,
        out_shape=jax.ShapeDtypeStruct((m, GDN_W), BF16),
        grid=(batch, ns),
        in_specs=[
            pl.BlockSpec((GDN_STEP, GDN_W), lambda b, s: (row(b, s), col0)),
            pl.BlockSpec((GDN_STEP, GDN_W), lambda b, s: (row(b, s), col0 + 1)),
            pl.BlockSpec((GDN_STEP, GDN_W), lambda b, s: (row(b, s), col0 + 2)),
            pl.BlockSpec((GDN_STEP, GDN_W), lambda b, s: (row(b, s), col0 + 3)),
            pl.BlockSpec((CONV_WIDTH, 3 * GDN_W), lambda b, s: (0, 0)),
            pl.BlockSpec((SUBLANES, GDN_STEP), lambda b, s: (2, row(b, s))),
            pl.BlockSpec((GDN_STEP, LANES), lambda b, s: (row(b, s), 0)),
            pl.BlockSpec(sel.shape, lambda b, s: (0, 0)),
            pl.BlockSpec((HEAD_DIM, HEAD_DIM), lambda b, s: (0, 0)),
            pl.BlockSpec((1, GDN_W), lambda b, s: (0, 0)),
        ],
        out_specs=pl.BlockSpec((GDN_STEP, GDN_W), lambda b, s: (row(b, s), 0)),
        scratch_shapes=[
            pltpu.VMEM((GDN_HEADS, HEAD_DIM, HEAD_DIM), F32),
            pltpu.VMEM((GDN_STEP + SUBLANES, 3 * GDN_W), F32),
            pltpu.VMEM((MASK_COUNT, CHUNK, CHUNK), F32),
        ],
        compiler_params=_params(("parallel", "arbitrary")),
        name="gdn",
    )(proj, proj, proj, proj, conv_w, gr, gc3, sel, jnp.ones((HEAD_DIM, HEAD_DIM), BF16), norm_w)


def _outproj_body(ym_ref, yf_ref, yg_ref, w_ref, x_ref, o_ref):
    acc = jnp.dot(ym_ref[...], w_ref[0:MLSTM_W, :], preferred_element_type=F32)
    acc += jnp.dot(yf_ref[...], w_ref[MLSTM_W:MLSTM_W + FOX_W, :], preferred_element_type=F32)
    acc += jnp.dot(yg_ref[...], w_ref[MLSTM_W + FOX_W:, :], preferred_element_type=F32)
    o_ref[...] = x_ref[...] + acc


def _outproj(y_m, y_f, y_g, w_out, x, *, tm=512):
    m = x.shape[0]
    return pl.pallas_call(
        _outproj_body,
        out_shape=jax.ShapeDtypeStruct((m, D_MODEL), F32),
        grid=(m // tm,),
        in_specs=[
            pl.BlockSpec((tm, MLSTM_W), lambda i: (i, 0)),
            pl.BlockSpec((tm, FOX_W), lambda i: (i, 0)),
            pl.BlockSpec((tm, GDN_W), lambda i: (i, 0)),
            pl.BlockSpec((D_MODEL, D_MODEL), lambda i: (0, 0)),
            pl.BlockSpec((tm, D_MODEL), lambda i: (i, 0)),
        ],
        out_specs=pl.BlockSpec((tm, D_MODEL), lambda i: (i, 0)),
        compiler_params=_params(("parallel",)),
        name="outproj",
    )(y_m, y_f, y_g, w_out, x)


def _ffn_body(x_ref, nw_ref, wg_ref, wu_ref, wd_ref, fw_ref, o_ref, h_scr, acc_scr, *, final_norm):
    j = pl.program_id(1)

    @pl.when(j == 0)
    def _():
        h_scr[...] = (_rms(x_ref[...]) * nw_ref[...]).astype(BF16)
        acc_scr[...] = jnp.zeros(acc_scr.shape, F32)

    h = h_scr[...]
    gate = jnp.dot(h, wg_ref[...], preferred_element_type=F32)
    up = jnp.dot(h, wu_ref[...], preferred_element_type=F32)
    act = (gate * _sigmoid(gate) * up).astype(BF16)
    acc_scr[...] += jnp.dot(act, wd_ref[...], preferred_element_type=F32)

    @pl.when(j == pl.num_programs(1) - 1)
    def _():
        out = x_ref[...] + acc_scr[...]
        if final_norm:
            out = _rms(out) * fw_ref[...]
        o_ref[...] = out


def _ffn(x, norm_w, w_gate, w_up, w_down, final_w, *, final_norm, tm=512, tf=512):
    m = x.shape[0]
    return pl.pallas_call(
        functools.partial(_ffn_body, final_norm=final_norm),
        out_shape=jax.ShapeDtypeStruct((m, D_MODEL), F32),
        grid=(m // tm, D_FF // tf),
        in_specs=[
            pl.BlockSpec((tm, D_MODEL), lambda i, j: (i, 0)),
            pl.BlockSpec((1, D_MODEL), lambda i, j: (0, 0)),
            pl.BlockSpec((D_MODEL, tf), lambda i, j: (0, j)),
            pl.BlockSpec((D_MODEL, tf), lambda i, j: (0, j)),
            pl.BlockSpec((tf, D_MODEL), lambda i, j: (j, 0)),
            pl.BlockSpec((1, D_MODEL), lambda i, j: (0, 0)),
        ],
        out_specs=pl.BlockSpec((tm, D_MODEL), lambda i, j: (i, 0)),
        scratch_shapes=[pltpu.VMEM((tm, D_MODEL), BF16), pltpu.VMEM((tm, D_MODEL), F32)],
        compiler_params=_params(("parallel", "arbitrary")),
        name="ffn",
    )(x, norm_w, w_gate, w_up, w_down, final_w)


def _prep_w_in_body(w_ref, wide_ref, gate_ref):
    o_mi = 4 * MLSTM_W
    o_fq = o_mi + 2 * MLSTM_HEADS
    o_ff = o_fq + 3 * FOX_W
    o_gq = o_ff + FOX_HEADS
    o_ga = o_gq + 4 * GDN_W
    n_in = o_ga + 2 * GDN_HEADS
    wide_ref[:, 0:o_mi] = w_ref[:, 0:o_mi].astype(BF16)
    wide_ref[:, o_mi:o_mi + 3 * FOX_W] = w_ref[:, o_fq:o_ff].astype(BF16)
    wide_ref[:, o_mi + 3 * FOX_W:] = w_ref[:, o_gq:o_ga].astype(BF16)
    lane = lax.broadcasted_iota(jnp.int32, (w_ref.shape[0], LANES), 1)
    first_tile = (o_ff // LANES) * LANES
    gate = jnp.where(lane < 2 * MLSTM_HEADS, w_ref[:, o_mi:o_mi + LANES],
                     jnp.where(lane < 2 * MLSTM_HEADS + FOX_HEADS, w_ref[:, first_tile:first_tile + LANES],
                               jnp.where(lane >= LANES - 2 * GDN_HEADS, w_ref[:, n_in - LANES:n_in], 0.0)))
    gate_ref[...] = gate.astype(BF16)


def _prep_w_in(w_in, layer, *, rows=256):
    _, d, n_in = w_in.shape
    return pl.pallas_call(
        _prep_w_in_body,
        out_shape=(jax.ShapeDtypeStruct((d, PROJ_W), BF16), jax.ShapeDtypeStruct((d, LANES), BF16)),
        grid=(d // rows,),
        in_specs=[pl.BlockSpec((None, rows, n_in), lambda i: (layer, i, 0))],
        out_specs=(pl.BlockSpec((rows, PROJ_W), lambda i: (i, 0)), pl.BlockSpec((rows, LANES), lambda i: (i, 0))),
        compiler_params=_params(("parallel",)),
        name="prep_w_in",
    )(w_in)


def kernel(x, mix_norm_w, w_in, mlstm_i_bias, mlstm_f_bias, fox_f_bias, gdn_conv_w, gdn_a_log, gdn_dt_bias,
           mlstm_out_norm_w, fox_out_norm_w, gdn_out_norm_w, w_out, ffn_norm_w, w_gate, w_up, w_down,
           final_norm_w):
    batch, seq, d_model = x.shape
    depth = w_in.shape[0]
    assert d_model == D_MODEL and seq % FOX_BLOCK == 0 and seq % GDN_STEP == 0
    xf = x.reshape(batch * seq, d_model)
    zeros4 = jnp.zeros((GDN_HEADS,), F32)
    for l in range(depth):
        w_wide, w_gate_t = _prep_w_in(w_in, l)
        bias_col = jnp.pad(jnp.concatenate([mlstm_i_bias[l], mlstm_f_bias[l], fox_f_bias[l], gdn_dt_bias[l],
                                            zeros4]), (0, 8)).reshape(GATE_ROWS, 1)
        alog_col = jnp.pad(gdn_a_log[l], (16, 12)).reshape(GATE_ROWS, 1)

        proj, gt = _inproj(xf, mix_norm_w[l].reshape(1, -1), w_wide, w_gate_t)
        gr, gr3 = _gates(gt, bias_col, alog_col, batch=batch, seq=seq)
        gc3 = gr3.T
        c_rows = gr[8:16].reshape(FOX_HEADS, batch * (seq // FOX_BLOCK), 1, FOX_BLOCK)

        y_m = _mlstm(proj, gr, gc3, mlstm_out_norm_w[l].reshape(1, -1), batch=batch, seq=seq)
        y_f = _fox(proj, c_rows, fox_out_norm_w[l].reshape(1, -1), batch=batch, seq=seq)
        y_g = _gdn(proj, gdn_conv_w[l], gr, gc3, gdn_out_norm_w[l].reshape(1, -1), batch=batch, seq=seq)
        xf = _outproj(y_m, y_f, y_g, w_out[l].astype(BF16), xf)
        xf = _ffn(xf, ffn_norm_w[l].reshape(1, -1), w_gate[l].astype(BF16), w_up[l].astype(BF16),
                  w_down[l].astype(BF16), final_norm_w.reshape(1, -1), final_norm=(l == depth - 1))
    return xf.reshape(batch, seq, d_model)
```

```python
import functools

import jax
import jax.numpy as jnp
import numpy as np
from jax import lax
from jax.experimental import pallas as pl
from jax.experimental.pallas import tpu as pltpu

F32 = jnp.float32
BF16 = jnp.bfloat16

D_MODEL = 2048
HEAD_DIM = 128
MLSTM_HEADS = 4
FOX_HEADS = 8
GDN_HEADS = 4
MLSTM_W = MLSTM_HEADS * HEAD_DIM
FOX_W = FOX_HEADS * HEAD_DIM
GDN_W = GDN_HEADS * HEAD_DIM
D_FF = 5632
CONV_WIDTH = 4
GATE_SOFTCAP = 15.0
NORM_EPS = 1e-6
QK_SCALE = HEAD_DIM ** -0.5
LOG2E = 1.4426950408889634

PROJ_W = 4 * MLSTM_W + 3 * FOX_W + 4 * GDN_W
GATE_ROWS = 32

LANES = 128
SUBLANES = 8
V7X_VMEM_BYTES = 64 * 1024 * 1024
VMEM_LIMIT = 56 * 1024 * 1024

CHUNK = 128
FOX_BLOCK = 512
FOX_SUB = 128
FOX_GROUP = 4
GDN_STEP = 512
MLSTM_STEP = 512

MASK_STRICT, MASK_INCL, MASK_DIAG8, MASK_OFF0 = 0, 1, 2, 3
MASK_COUNT = MASK_OFF0 + 4

NT_DIMS = (((1,), (1,)), ((), ()))
TN_DIMS = (((0,), (0,)), ((), ()))


def _params(sem):
    return pltpu.CompilerParams(dimension_semantics=sem, vmem_limit_bytes=VMEM_LIMIT)


def _sigmoid(z):
    return 1.0 / (1.0 + jnp.exp(-z))


def _log_sigmoid(z):
    return jnp.minimum(z, 0.0) - jnp.log1p(jnp.exp(-jnp.abs(z)))


def _softplus(z):
    return jnp.maximum(z, 0.0) + jnp.log1p(jnp.exp(-jnp.abs(z)))


def _rms(t):
    return t * lax.rsqrt(jnp.mean(t * t, axis=-1, keepdims=True) + NORM_EPS)


def _inproj_body(x_ref, nw_ref, w_ref, wg_ref, proj_ref, gt_ref, h_scr, *, row_chunk):
    @pl.when(pl.program_id(1) == 0)
    def _():
        for r in range(x_ref.shape[0] // row_chunk):
            rows = slice(r * row_chunk, (r + 1) * row_chunk)
            h_scr[rows, :] = (_rms(x_ref[rows, :]) * nw_ref[...]).astype(BF16)
        gt_ref[...] = lax.dot_general(wg_ref[...], h_scr[...], NT_DIMS, preferred_element_type=F32)

    proj_ref[...] = lax.dot_general(h_scr[...], w_ref[...], NT_DIMS, preferred_element_type=F32).astype(BF16)


def _inproj(x, norm_w, w_big, w_gate_t, *, tm=1024, tn=1024):
    m = x.shape[0]
    return pl.pallas_call(
        functools.partial(_inproj_body, row_chunk=256),
        out_shape=(jax.ShapeDtypeStruct((m, PROJ_W), BF16), jax.ShapeDtypeStruct((GATE_ROWS, m), F32)),
        grid=(m // tm, PROJ_W // tn),
        in_specs=[
            pl.BlockSpec((tm, D_MODEL), lambda i, j: (i, 0)),
            pl.BlockSpec((1, D_MODEL), lambda i, j: (0, 0)),
            pl.BlockSpec((tn, D_MODEL), lambda i, j: (j, 0)),
            pl.BlockSpec((GATE_ROWS, D_MODEL), lambda i, j: (0, 0)),
        ],
        out_specs=(
            pl.BlockSpec((tm, tn), lambda i, j: (i, j)),
            pl.BlockSpec((GATE_ROWS, tm), lambda i, j: (0, i)),
        ),
        scratch_shapes=[pltpu.VMEM((tm, D_MODEL), BF16)],
        compiler_params=_params(("parallel", "arbitrary")),
        name="inproj",
    )(x, norm_w, w_big, w_gate_t)


def _lane_cumsum(v, seg):
    pos = lax.broadcasted_iota(jnp.int32, v.shape, 1) & (seg - 1)
    shift = 1
    while shift < seg:
        v = v + jnp.where(pos >= shift, pltpu.roll(v, shift, 1), 0.0)
        shift *= 2
    return v


def _lane_cummax(v, seg):
    pos = lax.broadcasted_iota(jnp.int32, v.shape, 1) & (seg - 1)
    shift = 1
    while shift < seg:
        v = jnp.maximum(v, jnp.where(pos >= shift, pltpu.roll(v, shift, 1), -jnp.inf))
        shift *= 2
    return v


def _gates_body(gt_ref, bias_ref, alog_ref, out_ref, split_ref):
    seq = gt_ref.shape[1]
    first4 = lax.broadcasted_iota(jnp.int32, (SUBLANES, seq), 0) < 4
    z = gt_ref[0:8, :] + bias_ref[0:8, :]
    z = GATE_SOFTCAP * jnp.tanh(z / GATE_SOFTCAP)
    b_cum = _lane_cumsum(_log_sigmoid(z), CHUNK)
    out_ref[0:8, :] = jnp.where(first4, z, b_cum)
    d = z - pltpu.roll(b_cum, 4, 0)
    out_ref[24:32, :] = jnp.where(first4, d, pltpu.roll(_lane_cummax(d, CHUNK), 4, 0))

    raw = gt_ref[16:24, :]
    decay = _lane_cumsum(-jnp.exp(alog_ref[16:24, :]) * _softplus(raw + bias_ref[16:24, :]), CHUNK)
    out_ref[16:24, :] = jnp.where(first4, decay, _sigmoid(raw))

    fox = _lane_cumsum(_log_sigmoid(gt_ref[8:16, :] + bias_ref[8:16, :]), LANES)
    carry = jnp.zeros((SUBLANES, 1), F32)
    for blk in range(seq // LANES):
        lanes = slice(blk * LANES, (blk + 1) * LANES)
        tile = fox[:, lanes] + carry
        out_ref[8:16, lanes] = tile
        carry = tile[:, LANES - 1:LANES]

    value = out_ref[...]
    hi = value.astype(BF16)
    rest = value - hi.astype(F32)
    mid = rest.astype(BF16)
    split_ref[0:GATE_ROWS, :] = hi
    split_ref[GATE_ROWS:2 * GATE_ROWS, :] = mid
    split_ref[2 * GATE_ROWS:3 * GATE_ROWS, :] = (rest - mid.astype(F32)).astype(BF16)
    split_ref[3 * GATE_ROWS:, :] = jnp.zeros((LANES - 3 * GATE_ROWS, seq), BF16)


def _gates(gt, bias_col, alog_col, *, batch, seq):
    return pl.pallas_call(
        _gates_body,
        out_shape=(jax.ShapeDtypeStruct(gt.shape, F32), jax.ShapeDtypeStruct((LANES, gt.shape[1]), BF16)),
        grid=(batch,),
        in_specs=[
            pl.BlockSpec((GATE_ROWS, seq), lambda b: (0, b)),
            pl.BlockSpec((GATE_ROWS, 1), lambda b: (0, 0)),
            pl.BlockSpec((GATE_ROWS, 1), lambda b: (0, 0)),
        ],
        out_specs=(pl.BlockSpec((GATE_ROWS, seq), lambda b: (0, b)), pl.BlockSpec((LANES, seq), lambda b: (0, b))),
        compiler_params=_params(("parallel",)),
        name="gates",
    )(gt, bias_col, alog_col)


def _mlstm_body(q_ref, k_ref, v_ref, o_ref, gb_ref, gd_ref, gc3_ref, sel_ref, mean_ref, nw_ref, y_ref,
                ct_scr, m_scr):
    @pl.when(pl.program_id(1) == 0)
    def _():
        ct_scr[...] = jnp.zeros(ct_scr.shape, F32)
        m_scr[...] = jnp.zeros(m_scr.shape, F32)

    heads = range(MLSTM_HEADS)
    causal = (lax.broadcasted_iota(jnp.int32, (CHUNK, CHUNK), 0)
              >= lax.broadcasted_iota(jnp.int32, (CHUNK, CHUNK), 1))
    ones = jnp.ones((CHUNK, HEAD_DIM), BF16)
    cts = [ct_scr[h] for h in heads]
    ms = [m_scr[h:h + 1, 0:1] for h in heads]
    for r in range(q_ref.shape[0] // CHUNK):
        rows = slice(r * CHUNK, (r + 1) * CHUNK)
        hcols = [slice(h * HEAD_DIM, (h + 1) * HEAD_DIM) for h in heads]
        q = [q_ref[rows, c] for c in hcols]
        k = [k_ref[rows, c] for c in hcols]
        v_aug = [jnp.concatenate([v_ref[rows, c], ones], axis=1) for c in hcols]
        cols_rep = jnp.dot(gc3_ref[rows, :], sel_ref[...], preferred_element_type=F32)
        rep = lambda h, j: cols_rep[:, (3 * h + j) * LANES:(3 * h + j + 1) * LANES]
        a_col = [rep(h, 0) for h in heads]
        d_col = [rep(h, 1) for h in heads]
        b_col = [rep(h, 2) for h in heads]
        d_row = [gd_ref[h:h + 1, rows] for h in heads]
        b_last = [gb_ref[4 + h:5 + h, rows][:, CHUNK - 1:CHUNK] for h in heads]
        a_last = [gd_ref[4 + h:5 + h, rows][:, CHUNK - 1:CHUNK] for h in heads]

        qk = [lax.dot_general(q[h], k[h], NT_DIMS, preferred_element_type=F32) for h in heads]
        q_state = [lax.dot_general(q[h], cts[h].astype(BF16), NT_DIMS, preferred_element_type=F32)
                   for h in heads]
        m_rel = [jnp.maximum(ms[h], a_col[h]) for h in heads]
        p = [jnp.exp(jnp.where(causal, d_row[h] - m_rel[h], -jnp.inf)) * (qk[h] * QK_SCALE) for h in heads]
        intra = [jnp.dot(p[h].astype(BF16), v_aug[h], preferred_element_type=F32) for h in heads]
        for h in heads:
            w_inter = jnp.exp(ms[h] - m_rel[h])
            both = jnp.concatenate([w_inter, w_inter], axis=1) * q_state[h] + intra[h]
            floor = jnp.exp(-(b_col[h] + m_rel[h]))
            h_out = both[:, :HEAD_DIM] / jnp.maximum(jnp.abs(both[:, HEAD_DIM:]), floor)
            mean = jnp.dot(h_out.astype(BF16), mean_ref[...], preferred_element_type=F32)
            centered = h_out - mean
            var = jnp.dot((centered * centered).astype(BF16), mean_ref[...], preferred_element_type=F32)
            gate = _sigmoid(o_ref[rows, hcols[h]].astype(F32))
            y_ref[rows, hcols[h]] = (centered * lax.rsqrt(var + NORM_EPS) * nw_ref[:, hcols[h]]
                                     * gate).astype(BF16)

        m_end = [jnp.maximum(ms[h], a_last[h]) for h in heads]
        kw = [(k[h].astype(F32) * (jnp.exp(d_col[h] - m_end[h]) * QK_SCALE)).astype(BF16) for h in heads]
        cts = [jnp.exp(ms[h] - m_end[h]) * cts[h]
               + lax.dot_general(v_aug[h], kw[h], TN_DIMS, preferred_element_type=F32) for h in heads]
        ms = [b_last[h] + m_end[h] for h in heads]
    for h in heads:
        ct_scr[h] = cts[h]
        m_scr[h:h + 1, :] = jnp.broadcast_to(ms[h], (1, LANES))


def _one_hot_columns(gate_cols):
    sel = np.zeros((LANES, LANES * len(gate_cols)), np.float32)
    for j, col in enumerate(gate_cols):
        for part in range(3):
            sel[part * GATE_ROWS + col, j * LANES:(j + 1) * LANES] = 1.0
    return jnp.asarray(sel, BF16)


def _mlstm(proj, gr, gc3, norm_w, *, batch, seq):
    m = proj.shape[0]
    nc = seq // MLSTM_STEP
    row = lambda b, c: b * nc + c
    sel = _one_hot_columns([c for h in range(MLSTM_HEADS) for c in (28 + h, 24 + h, 4 + h)])
    mean_w = jnp.full((HEAD_DIM, HEAD_DIM), 1.0 / HEAD_DIM, BF16)
    return pl.pallas_call(
        _mlstm_body,
        out_shape=jax.ShapeDtypeStruct((m, MLSTM_W), BF16),
        grid=(batch, nc),
        in_specs=[
            pl.BlockSpec((MLSTM_STEP, MLSTM_W), lambda b, c: (row(b, c), 0)),
            pl.BlockSpec((MLSTM_STEP, MLSTM_W), lambda b, c: (row(b, c), 1)),
            pl.BlockSpec((MLSTM_STEP, MLSTM_W), lambda b, c: (row(b, c), 2)),
            pl.BlockSpec((MLSTM_STEP, MLSTM_W), lambda b, c: (row(b, c), 3)),
            pl.BlockSpec((SUBLANES, MLSTM_STEP), lambda b, c: (0, row(b, c))),
            pl.BlockSpec((SUBLANES, MLSTM_STEP), lambda b, c: (3, row(b, c))),
            pl.BlockSpec((MLSTM_STEP, LANES), lambda b, c: (row(b, c), 0)),
            pl.BlockSpec(sel.shape, lambda b, c: (0, 0)),
            pl.BlockSpec((HEAD_DIM, HEAD_DIM), lambda b, c: (0, 0)),
            pl.BlockSpec((1, MLSTM_W), lambda b, c: (0, 0)),
        ],
        out_specs=pl.BlockSpec((MLSTM_STEP, MLSTM_W), lambda b, c: (row(b, c), 0)),
        scratch_shapes=[
            pltpu.VMEM((MLSTM_HEADS, 2 * HEAD_DIM, HEAD_DIM), F32),
            pltpu.VMEM((SUBLANES, LANES), F32),
        ],
        compiler_params=_params(("parallel", "arbitrary")),
        name="mlstm",
    )(proj, proj, proj, proj, gr, gr, gc3, sel, mean_w, norm_w)


def _fox_body(q_ref, k_ref, v_ref, c_ref, nw_ref, y_ref, acc_scr, m_scr, l_scr, q_scr, cq_scr):
    qi = pl.program_id(2)
    tq = q_ref.shape[0]
    nsub = tq // FOX_SUB
    ntile = tq // LANES
    tiles_per_sub = FOX_SUB // LANES

    q_scr[...] = (q_ref[...].astype(F32) * (QK_SCALE * LOG2E)).astype(BF16)
    c_q_row = c_ref[qi] * LOG2E
    eye = (lax.broadcasted_iota(jnp.int32, (LANES, LANES), 0)
           == lax.broadcasted_iota(jnp.int32, (LANES, LANES), 1))
    for j in range(ntile):
        lanes = slice(j * LANES, (j + 1) * LANES)
        col = jnp.sum(jnp.where(eye, c_q_row[:, lanes], 0.0), axis=-1, keepdims=True)
        cq_scr[lanes, :] = jnp.broadcast_to(col, (LANES, LANES))
    m_scr[...] = jnp.full(m_scr.shape, -jnp.inf, F32)
    l_scr[...] = jnp.zeros(l_scr.shape, F32)
    acc_scr[...] = jnp.zeros(acc_scr.shape, F32)

    def logits(kb, diagonal):
        base = pl.multiple_of(kb * tq, tq)
        c_k = c_ref[kb] * LOG2E
        widths = [((s + 1) * FOX_SUB if diagonal else tq) for s in range(nsub)]
        zs = []
        if not diagonal:
            z_all = lax.dot_general(q_scr[...], k_ref[pl.ds(base, tq), :], NT_DIMS,
                                    preferred_element_type=F32) - c_k
        for s in range(nsub):
            if diagonal:
                k = k_ref[pl.ds(base, widths[s]), :]
                z = lax.dot_general(q_scr[s * FOX_SUB:(s + 1) * FOX_SUB, :], k, NT_DIMS,
                                    preferred_element_type=F32) - c_k[:, :widths[s]]
            else:
                z = z_all[s * FOX_SUB:(s + 1) * FOX_SUB, :]
            tiles = [z[:, j * LANES:(j + 1) * LANES] for j in range(widths[s] // LANES)]
            if diagonal:
                for t in range(tiles_per_sub):
                    j = s * tiles_per_sub + t
                    keep = (lax.broadcasted_iota(jnp.int32, (FOX_SUB, LANES), 0)
                            >= lax.broadcasted_iota(jnp.int32, (FOX_SUB, LANES), 1) + t * LANES)
                    tiles[j] = jnp.where(keep, tiles[j], -jnp.inf)
            zs.append(tiles)
        return zs

    def update(kb, zs, diagonal):
        base = pl.multiple_of(kb * tq, tq)
        widths = [((s + 1) * FOX_SUB if diagonal else tq) for s in range(nsub)]
        ps, alphas = [], []
        for s in range(nsub):
            rows = slice(s * FOX_SUB, (s + 1) * FOX_SUB)
            tiles = zs[s]
            mx = tiles[0]
            for t in tiles[1:]:
                mx = jnp.maximum(mx, t)
            c_q = cq_scr[rows, :]
            m_prev = m_scr[rows, :]
            m_new = jnp.maximum(m_prev, jnp.max(mx, axis=-1, keepdims=True) + c_q)
            alpha = jnp.exp2(m_prev - m_new)
            shift = m_new - c_q
            p_tiles = [jnp.exp2(t - shift) for t in tiles]
            p_sum = p_tiles[0]
            for t in p_tiles[1:]:
                p_sum = p_sum + t
            l_scr[rows, :] = alpha * l_scr[rows, :] + p_sum
            m_scr[rows, :] = m_new
            ps.append(jnp.concatenate(p_tiles, axis=1).astype(BF16))
            alphas.append(alpha)
        if diagonal:
            for s in range(nsub):
                rows = slice(s * FOX_SUB, (s + 1) * FOX_SUB)
                v = v_ref[pl.ds(base, widths[s]), :]
                acc_scr[rows, :] = alphas[s] * acc_scr[rows, :] + jnp.dot(ps[s], v, preferred_element_type=F32)
        else:
            pv = jnp.dot(jnp.concatenate(ps, axis=0), v_ref[pl.ds(base, tq), :], preferred_element_type=F32)
            acc_scr[...] = jnp.concatenate(alphas, axis=0) * acc_scr[...] + pv

    def blocks(first, count):
        zs = [logits(first + j, False) for j in range(count)]
        for j in range(count):
            update(first + j, zs[j], False)

    def group(g, carry):
        blocks(g * FOX_GROUP, FOX_GROUP)
        return carry

    lax.fori_loop(0, qi // FOX_GROUP, group, 0)
    done = (qi // FOX_GROUP) * FOX_GROUP
    size = FOX_GROUP // 2
    while size >= 1:
        take = (qi & size) != 0
        pl.when(take)(functools.partial(blocks, done, size))
        done = done + jnp.where(take, size, 0)
        size //= 2
    update(qi, logits(qi, True), True)
    out = acc_scr[...] / jnp.sum(l_scr[...], axis=-1, keepdims=True)
    y_ref[...] = (_rms(out) * nw_ref[...]).astype(BF16)


def _fox(proj, c_rows, norm_w, *, batch, seq):
    m = proj.shape[0]
    nq = seq // FOX_BLOCK
    q_col = (4 * MLSTM_W) // HEAD_DIM
    k_col = q_col + FOX_HEADS
    v_col = k_col + FOX_HEADS
    return pl.pallas_call(
        _fox_body,
        out_shape=jax.ShapeDtypeStruct((m, FOX_W), BF16),
        grid=(batch, FOX_HEADS, nq),
        in_specs=[
            pl.BlockSpec((FOX_BLOCK, HEAD_DIM), lambda b, h, i: (b * nq + i, q_col + h)),
            pl.BlockSpec((seq, HEAD_DIM), lambda b, h, i: (b, k_col + h)),
            pl.BlockSpec((seq, HEAD_DIM), lambda b, h, i: (b, v_col + h)),
            pl.BlockSpec((None, nq, 1, FOX_BLOCK), lambda b, h, i: (h, b, 0, 0)),
            pl.BlockSpec((1, HEAD_DIM), lambda b, h, i: (0, h)),
        ],
        out_specs=pl.BlockSpec((FOX_BLOCK, HEAD_DIM), lambda b, h, i: (b * nq + i, h)),
        scratch_shapes=[
            pltpu.VMEM((FOX_BLOCK, HEAD_DIM), F32),
            pltpu.VMEM((FOX_BLOCK, LANES), F32),
            pltpu.VMEM((FOX_BLOCK, LANES), F32),
            pltpu.VMEM((FOX_BLOCK, HEAD_DIM), BF16),
            pltpu.VMEM((FOX_BLOCK, LANES), F32),
        ],
        compiler_params=_params(("parallel", "parallel", "arbitrary")),
        name="fox",
    )(proj, proj, proj, c_rows, norm_w)


def _mm(x, y):
    return jnp.dot(x.astype(BF16), y.astype(BF16), preferred_element_type=F32)


def _unit_lower_inverses(a_mats, mask_ref):
    eye = mask_ref[MASK_INCL] - mask_ref[MASK_STRICT]
    negs = [-(a * mask_ref[MASK_DIAG8]) for a in a_mats]
    invs = [eye + n for n in negs]
    for _ in range(2):
        negs = [_mm(n, n) for n in negs]
        invs = [i + _mm(i, n) for i, n in zip(invs, negs)]
    for level in range(MASK_OFF0, MASK_COUNT):
        tmps = [_mm(a * mask_ref[level], i) for a, i in zip(a_mats, invs)]
        invs = [i - _mm(i, t) for i, t in zip(invs, tmps)]
    return invs


def _gdn_body(q_ref, k_ref, v_ref, z_ref, cw_ref, gr_ref, gc3_ref, sel_ref, ones_ref, nw_ref, y_ref,
              s_scr, ext_scr, mask_scr):
    step = q_ref.shape[0]
    halo = SUBLANES
    nchunk = step // CHUNK

    @pl.when(pl.program_id(1) == 0)
    def _():
        s_scr[...] = jnp.zeros(s_scr.shape, F32)
        ext_scr[0:halo, :] = jnp.zeros((halo, 3 * GDN_W), F32)
        ri = lax.broadcasted_iota(jnp.int32, (CHUNK, CHUNK), 0)
        ci = lax.broadcasted_iota(jnp.int32, (CHUNK, CHUNK), 1)
        mask_scr[MASK_STRICT] = jnp.where(ri > ci, 1.0, 0.0)
        mask_scr[MASK_INCL] = jnp.where(ri >= ci, 1.0, 0.0)
        mask_scr[MASK_DIAG8] = jnp.where((ri >> 3) == (ci >> 3), 1.0, 0.0)
        for bits in range(3, 3 + MASK_COUNT - MASK_OFF0):
            sibling = ((ri >> (bits + 1)) == (ci >> (bits + 1))) & ((ri >> bits) != (ci >> bits))
            mask_scr[MASK_OFF0 + bits - 3] = jnp.where(sibling, 1.0, 0.0)

    ext_scr[halo:halo + step, 0:GDN_W] = q_ref[...].astype(F32)
    ext_scr[halo:halo + step, GDN_W:2 * GDN_W] = k_ref[...].astype(F32)
    ext_scr[halo:halo + step, 2 * GDN_W:3 * GDN_W] = v_ref[...].astype(F32)
    ext = ext_scr[...]
    conv = ext * cw_ref[0:1, :]
    for j in range(1, CONV_WIDTH):
        conv = pltpu.roll(conv, 1, 0) + ext * cw_ref[j:j + 1, :]
    conv = conv[halo:, :]
    ext_scr[0:halo, :] = ext_scr[step:step + halo, :]
    qkv = conv * _sigmoid(conv)

    def row_sum(t):
        return jnp.dot(t.astype(BF16), ones_ref[...], preferred_element_type=F32)

    def l2n(t):
        return t * lax.rsqrt(row_sum(t * t) + NORM_EPS)

    cols_rep = [jnp.dot(gc3_ref[r * CHUNK:(r + 1) * CHUNK, :], sel_ref[...], preferred_element_type=F32)
                for r in range(nchunk)]

    groups = [(h, r) for h in range(GDN_HEADS) for r in range(nchunk)]
    prep = {}
    for h, r in groups:
        rows = slice(r * CHUNK, (r + 1) * CHUNK)
        q = l2n(qkv[rows, h * HEAD_DIM:(h + 1) * HEAD_DIM]) * QK_SCALE
        k = l2n(qkv[rows, GDN_W + h * HEAD_DIM:GDN_W + (h + 1) * HEAD_DIM])
        v = qkv[rows, 2 * GDN_W + h * HEAD_DIM:2 * GDN_W + (h + 1) * HEAD_DIM]
        g_row = gr_ref[h:h + 1, rows]
        g_col = cols_rep[r][:, (2 * h) * LANES:(2 * h + 1) * LANES]
        beta = cols_rep[r][:, (2 * h + 1) * LANES:(2 * h + 2) * LANES]
        incl = mask_scr[MASK_INCL]
        decay = jnp.exp((g_col - g_row) * incl) * incl
        kb = k * beta
        e_g = jnp.exp(g_col)
        g_last = g_row[:, CHUNK - 1:CHUNK]
        kk_qk = lax.dot_general(jnp.concatenate([kb, q], axis=0).astype(BF16), k.astype(BF16), NT_DIMS,
                                preferred_element_type=F32)
        prep[h, r] = dict(
            a=kk_qk[:CHUNK] * (decay * mask_scr[MASK_STRICT]),
            attn=kk_qk[CHUNK:] * decay,
            rhs=jnp.concatenate([v * beta, kb * e_g], axis=1),
            q_dec=q * e_g,
            k_end=k * jnp.exp(g_last - g_col),
            s_decay=jnp.exp(g_last),
        )
    invs = _unit_lower_inverses([prep[g]["a"] for g in groups], mask_scr)
    for g, inv in zip(groups, invs):
        w = _mm(inv, prep[g]["rhs"])
        prep[g]["w_val"], prep[g]["w_key"] = w[:, :HEAD_DIM], w[:, HEAD_DIM:]

    states = [s_scr[h] for h in range(GDN_HEADS)]
    for r in range(nchunk):
        rows = slice(r * CHUNK, (r + 1) * CHUNK)
        ks = [_mm(jnp.concatenate([prep[h, r]["w_key"], prep[h, r]["q_dec"]], axis=0), states[h])
              for h in range(GDN_HEADS)]
        us = [prep[h, r]["w_val"] - ks[h][:CHUNK] for h in range(GDN_HEADS)]
        outs = [ks[h][CHUNK:] + _mm(prep[h, r]["attn"], us[h]) for h in range(GDN_HEADS)]
        states = [prep[h, r]["s_decay"] * states[h]
                  + lax.dot_general(prep[h, r]["k_end"].astype(BF16), us[h].astype(BF16), TN_DIMS,
                                    preferred_element_type=F32) for h in range(GDN_HEADS)]
        for h in range(GDN_HEADS):
            cols = slice(h * HEAD_DIM, (h + 1) * HEAD_DIM)
            gate = z_ref[rows, cols].astype(F32)
            normed = outs[h] * lax.rsqrt(row_sum(outs[h] * outs[h]) * (1.0 / HEAD_DIM) + NORM_EPS)
            y_ref[rows, cols] = (normed * nw_ref[:, cols] * (gate * _sigmoid(gate))).astype(BF16)
    for h in range(GDN_HEADS):
        s_scr[h] = states[h]


def _gdn(proj, conv_w, gr, gc3, norm_w, *, batch, seq):
    m = proj.shape[0]
    ns = seq // GDN_STEP
    sel = _one_hot_columns([c for h in range(GDN_HEADS) for c in (16 + h, 20 + h)])
    col0 = (4 * MLSTM_W + 3 * FOX_W) // GDN_W
    row = lambda b, s: b * ns + s
    return pl.pallas_call(
        _gdn_body,
        out_shape=jax.ShapeDtypeStruct((m, GDN_W), BF16),
        grid=(batch, ns),
        in_specs=[
            pl.BlockSpec((GDN_STEP, GDN_W), lambda b, s: (row(b, s), col0)),
            pl.BlockSpec((GDN_STEP, GDN_W), lambda b, s: (row(b, s), col0 + 1)),
            pl.BlockSpec((GDN_STEP, GDN_W), lambda b, s: (row(b, s), col0 + 2)),
            pl.BlockSpec((GDN_STEP, GDN_W), lambda b, s: (row(b, s), col0 + 3)),
            pl.BlockSpec((CONV_WIDTH, 3 * GDN_W), lambda b, s: (0, 0)),
            pl.BlockSpec((SUBLANES, GDN_STEP), lambda b, s: (2, row(b, s))),
            pl.BlockSpec((GDN_STEP, LANES), lambda b, s: (row(b, s), 0)),
            pl.BlockSpec(sel.shape, lambda b, s: (0, 0)),
            pl.BlockSpec((HEAD_DIM, HEAD_DIM), lambda b, s: (0, 0)),
            pl.BlockSpec((1, GDN_W), lambda b, s: (0, 0)),
        ],
        out_specs=pl.BlockSpec((GDN_STEP, GDN_W), lambda b, s: (row(b, s), 0)),
        scratch_shapes=[
            pltpu.VMEM((GDN_HEADS, HEAD_DIM, HEAD_DIM), F32),
            pltpu.VMEM((GDN_STEP + SUBLANES, 3 * GDN_W), F32),
            pltpu.VMEM((MASK_COUNT, CHUNK, CHUNK), F32),
        ],
        compiler_params=_params(("parallel", "arbitrary")),
        name="gdn",
    )(proj, proj, proj, proj, conv_w, gr, gc3, sel, jnp.ones((HEAD_DIM, HEAD_DIM), BF16), norm_w)


def _outproj_body(ym_ref, yf_ref, yg_ref, w_ref, x_ref, o_ref):
    acc = jnp.dot(ym_ref[...], w_ref[0:MLSTM_W, :], preferred_element_type=F32)
    acc += jnp.dot(yf_ref[...], w_ref[MLSTM_W:MLSTM_W + FOX_W, :], preferred_element_type=F32)
    acc += jnp.dot(yg_ref[...], w_ref[MLSTM_W + FOX_W:, :], preferred_element_type=F32)
    o_ref[...] = x_ref[...] + acc


def _outproj(y_m, y_f, y_g, w_out, layer, x, *, tm=512):
    m = x.shape[0]
    return pl.pallas_call(
        _outproj_body,
        out_shape=jax.ShapeDtypeStruct((m, D_MODEL), F32),
        grid=(m // tm,),
        in_specs=[
            pl.BlockSpec((tm, MLSTM_W), lambda i: (i, 0)),
            pl.BlockSpec((tm, FOX_W), lambda i: (i, 0)),
            pl.BlockSpec((tm, GDN_W), lambda i: (i, 0)),
            pl.BlockSpec((None, D_MODEL, D_MODEL), lambda i: (layer, 0, 0)),
            pl.BlockSpec((tm, D_MODEL), lambda i: (i, 0)),
        ],
        out_specs=pl.BlockSpec((tm, D_MODEL), lambda i: (i, 0)),
        compiler_params=_params(("parallel",)),
        name="outproj",
    )(y_m, y_f, y_g, w_out, x)


def _ffn_body(x_ref, nw_ref, wg_ref, wu_ref, wd_ref, fw_ref, o_ref, h_scr, acc_scr, *, final_norm):
    j = pl.program_id(1)

    @pl.when(j == 0)
    def _():
        h_scr[...] = (_rms(x_ref[...]) * nw_ref[...]).astype(BF16)
        acc_scr[...] = jnp.zeros(acc_scr.shape, F32)

    h = h_scr[...]
    gate = jnp.dot(h, wg_ref[...], preferred_element_type=F32)
    up = jnp.dot(h, wu_ref[...], preferred_element_type=F32)
    act = (gate * _sigmoid(gate) * up).astype(BF16)
    acc_scr[...] += jnp.dot(act, wd_ref[...], preferred_element_type=F32)

    @pl.when(j == pl.num_programs(1) - 1)
    def _():
        out = x_ref[...] + acc_scr[...]
        if final_norm:
            out = _rms(out) * fw_ref[...]
        o_ref[...] = out


def _ffn(x, norm_w, w_gate, w_up, w_down, layer, final_w, *, final_norm, tm=512, tf=512):
    m = x.shape[0]
    return pl.pallas_call(
        functools.partial(_ffn_body, final_norm=final_norm),
        out_shape=jax.ShapeDtypeStruct((m, D_MODEL), F32),
        grid=(m // tm, D_FF // tf),
        in_specs=[
            pl.BlockSpec((tm, D_MODEL), lambda i, j: (i, 0)),
            pl.BlockSpec((1, D_MODEL), lambda i, j: (0, 0)),
            pl.BlockSpec((None, D_MODEL, tf), lambda i, j: (layer, 0, j)),
            pl.BlockSpec((None, D_MODEL, tf), lambda i, j: (layer, 0, j)),
            pl.BlockSpec((None, tf, D_MODEL), lambda i, j: (layer, j, 0)),
            pl.BlockSpec((1, D_MODEL), lambda i, j: (0, 0)),
        ],
        out_specs=pl.BlockSpec((tm, D_MODEL), lambda i, j: (i, 0)),
        scratch_shapes=[pltpu.VMEM((tm, D_MODEL), BF16), pltpu.VMEM((tm, D_MODEL), F32)],
        compiler_params=_params(("parallel", "arbitrary")),
        name="ffn",
    )(x, norm_w, w_gate, w_up, w_down, final_w)


def _prep_w_in_body(w_ref, wide_ref, gate_ref):
    o_mi = 4 * MLSTM_W
    o_fq = o_mi + 2 * MLSTM_HEADS
    o_ff = o_fq + 3 * FOX_W
    o_gq = o_ff + FOX_HEADS
    o_ga = o_gq + 4 * GDN_W
    wide_ref[0:o_mi, :] = w_ref[0:o_mi, :].astype(BF16)
    wide_ref[o_mi:o_mi + 3 * FOX_W, :] = w_ref[o_fq:o_ff, :].astype(BF16)
    wide_ref[o_mi + 3 * FOX_W:, :] = w_ref[o_gq:o_ga, :].astype(BF16)
    gate = jnp.concatenate([w_ref[o_mi:o_fq, :], w_ref[o_ff:o_gq, :], w_ref[o_ga:, :],
                            jnp.zeros((SUBLANES, w_ref.shape[1]), F32)], axis=0)
    gate_ref[...] = gate.astype(BF16)


def _prep_w_in(w_in_t, layer, *, cols=256):
    _, n_in, d = w_in_t.shape
    return pl.pallas_call(
        _prep_w_in_body,
        out_shape=(jax.ShapeDtypeStruct((PROJ_W, d), BF16), jax.ShapeDtypeStruct((GATE_ROWS, d), BF16)),
        grid=(d // cols,),
        in_specs=[pl.BlockSpec((None, n_in, cols), lambda i: (layer, 0, i))],
        out_specs=(pl.BlockSpec((PROJ_W, cols), lambda i: (0, i)), pl.BlockSpec((GATE_ROWS, cols), lambda i: (0, i))),
        compiler_params=_params(("parallel",)),
        name="prep_w_in",
    )(w_in_t)


def kernel(x, mix_norm_w, w_in, mlstm_i_bias, mlstm_f_bias, fox_f_bias, gdn_conv_w, gdn_a_log, gdn_dt_bias,
           mlstm_out_norm_w, fox_out_norm_w, gdn_out_norm_w, w_out, ffn_norm_w, w_gate, w_up, w_down,
           final_norm_w):
    batch, seq, d_model = x.shape
    depth = w_in.shape[0]
    assert d_model == D_MODEL and seq % FOX_BLOCK == 0 and seq % GDN_STEP == 0
    xf = x.reshape(batch * seq, d_model)
    zeros4 = jnp.zeros((GDN_HEADS,), F32)
    w_in_t = jnp.swapaxes(w_in, 1, 2)
    w_out_b, w_gate_b, w_up_b, w_down_b = (w.astype(BF16) for w in (w_out, w_gate, w_up, w_down))
    for l in range(depth):
        w_wide, w_gate_t = _prep_w_in(w_in_t, l)
        bias_col = jnp.pad(jnp.concatenate([mlstm_i_bias[l], mlstm_f_bias[l], fox_f_bias[l], gdn_dt_bias[l],
                                            zeros4]), (0, 8)).reshape(GATE_ROWS, 1)
        alog_col = jnp.pad(gdn_a_log[l], (16, 12)).reshape(GATE_ROWS, 1)

        proj, gt = _inproj(xf, mix_norm_w[l].reshape(1, -1), w_wide, w_gate_t)
        gr, gr3 = _gates(gt, bias_col, alog_col, batch=batch, seq=seq)
        gc3 = gr3.T
        c_rows = gr[8:16].reshape(FOX_HEADS, batch * (seq // FOX_BLOCK), 1, FOX_BLOCK)

        y_m = _mlstm(proj, gr, gc3, mlstm_out_norm_w[l].reshape(1, -1), batch=batch, seq=seq)
        y_f = _fox(proj, c_rows, fox_out_norm_w[l].reshape(1, -1), batch=batch, seq=seq)
        y_g = _gdn(proj, gdn_conv_w[l], gr, gc3, gdn_out_norm_w[l].reshape(1, -1), batch=batch, seq=seq)
        xf = _outproj(y_m, y_f, y_g, w_out_b, l, xf)
        xf = _ffn(xf, ffn_norm_w[l].reshape(1, -1), w_gate_b, w_up_b, w_down_b, l, final_norm_w.reshape(1, -1),
                  final_norm=(l == depth - 1))
    return xf.reshape(batch, seq, d_model)
```

```python
import functools

import jax
import jax.numpy as jnp
import numpy as np
from jax import lax
from jax.experimental import pallas as pl
from jax.experimental.pallas import tpu as pltpu

F32 = jnp.float32
BF16 = jnp.bfloat16

D_MODEL = 2048
HEAD_DIM = 128
MLSTM_HEADS = 4
FOX_HEADS = 8
GDN_HEADS = 4
MLSTM_W = MLSTM_HEADS * HEAD_DIM
FOX_W = FOX_HEADS * HEAD_DIM
GDN_W = GDN_HEADS * HEAD_DIM
D_FF = 5632
CONV_WIDTH = 4
GATE_SOFTCAP = 15.0
NORM_EPS = 1e-6
QK_SCALE = HEAD_DIM ** -0.5
LOG2E = 1.4426950408889634

PROJ_W = 4 * MLSTM_W + 3 * FOX_W + 4 * GDN_W
GATE_ROWS = 32

LANES = 128
SUBLANES = 8
V7X_VMEM_BYTES = 64 * 1024 * 1024
VMEM_LIMIT = 56 * 1024 * 1024

CHUNK = 128
FOX_BLOCK = 512
FOX_GROUP = 4
FOX_PAIR = 2
GDN_STEP = 512
MLSTM_STEP = 512

MASK_STRICT, MASK_INCL, MASK_DIAG8, MASK_OFF0 = 0, 1, 2, 3
MASK_COUNT = MASK_OFF0 + 4

NT_DIMS = (((1,), (1,)), ((), ()))
TN_DIMS = (((0,), (0,)), ((), ()))


def _params(sem):
    return pltpu.CompilerParams(dimension_semantics=sem, vmem_limit_bytes=VMEM_LIMIT)


def _sigmoid(z):
    return 1.0 / (1.0 + jnp.exp(-z))


def _log_sigmoid(z):
    return jnp.minimum(z, 0.0) - jnp.log1p(jnp.exp(-jnp.abs(z)))


def _softplus(z):
    return jnp.maximum(z, 0.0) + jnp.log1p(jnp.exp(-jnp.abs(z)))


def _rms(t):
    return t * lax.rsqrt(jnp.mean(t * t, axis=-1, keepdims=True) + NORM_EPS)


def _inproj_body(x_ref, nw_ref, w_ref, wg_ref, proj_ref, gt_ref, h_scr, *, row_chunk):
    @pl.when(pl.program_id(1) == 0)
    def _():
        for r in range(x_ref.shape[0] // row_chunk):
            rows = slice(r * row_chunk, (r + 1) * row_chunk)
            h_scr[rows, :] = (_rms(x_ref[rows, :]) * nw_ref[...]).astype(BF16)
        gt_ref[...] = lax.dot_general(wg_ref[...], h_scr[...], NT_DIMS, preferred_element_type=F32)

    proj_ref[...] = lax.dot_general(h_scr[...], w_ref[...], NT_DIMS, preferred_element_type=F32).astype(BF16)


def _inproj(x, norm_w, w_big, w_gate_t, *, tm=1024, tn=1024):
    m = x.shape[0]
    return pl.pallas_call(
        functools.partial(_inproj_body, row_chunk=256),
        out_shape=(jax.ShapeDtypeStruct((m, PROJ_W), BF16), jax.ShapeDtypeStruct((GATE_ROWS, m), F32)),
        grid=(m // tm, PROJ_W // tn),
        in_specs=[
            pl.BlockSpec((tm, D_MODEL), lambda i, j: (i, 0)),
            pl.BlockSpec((1, D_MODEL), lambda i, j: (0, 0)),
            pl.BlockSpec((tn, D_MODEL), lambda i, j: (j, 0)),
            pl.BlockSpec((GATE_ROWS, D_MODEL), lambda i, j: (0, 0)),
        ],
        out_specs=(
            pl.BlockSpec((tm, tn), lambda i, j: (i, j)),
            pl.BlockSpec((GATE_ROWS, tm), lambda i, j: (0, i)),
        ),
        scratch_shapes=[pltpu.VMEM((tm, D_MODEL), BF16)],
        compiler_params=_params(("parallel", "arbitrary")),
        name="inproj",
    )(x, norm_w, w_big, w_gate_t)


def _lane_cumsum(v, seg):
    pos = lax.broadcasted_iota(jnp.int32, v.shape, 1) & (seg - 1)
    shift = 1
    while shift < seg:
        v = v + jnp.where(pos >= shift, pltpu.roll(v, shift, 1), 0.0)
        shift *= 2
    return v


def _lane_cummax(v, seg):
    pos = lax.broadcasted_iota(jnp.int32, v.shape, 1) & (seg - 1)
    shift = 1
    while shift < seg:
        v = jnp.maximum(v, jnp.where(pos >= shift, pltpu.roll(v, shift, 1), -jnp.inf))
        shift *= 2
    return v


def _gates_body(gt_ref, bias_ref, alog_ref, out_ref, split_ref):
    seq = gt_ref.shape[1]
    first4 = lax.broadcasted_iota(jnp.int32, (SUBLANES, seq), 0) < 4
    z = gt_ref[0:8, :] + bias_ref[0:8, :]
    z = GATE_SOFTCAP * jnp.tanh(z / GATE_SOFTCAP)
    b_cum = _lane_cumsum(_log_sigmoid(z), CHUNK)
    out_ref[0:8, :] = jnp.where(first4, z, b_cum)
    d = z - pltpu.roll(b_cum, 4, 0)
    out_ref[24:32, :] = jnp.where(first4, d, pltpu.roll(_lane_cummax(d, CHUNK), 4, 0))

    raw = gt_ref[16:24, :]
    decay = _lane_cumsum(-jnp.exp(alog_ref[16:24, :]) * _softplus(raw + bias_ref[16:24, :]), CHUNK)
    out_ref[16:24, :] = jnp.where(first4, decay, _sigmoid(raw))

    fox = _lane_cumsum(LOG2E * _log_sigmoid(gt_ref[8:16, :] + bias_ref[8:16, :]), LANES)
    carry = jnp.zeros((SUBLANES, 1), F32)
    for blk in range(seq // LANES):
        lanes = slice(blk * LANES, (blk + 1) * LANES)
        tile = fox[:, lanes] + carry
        out_ref[8:16, lanes] = tile
        carry = tile[:, LANES - 1:LANES]

    value = out_ref[...]
    hi = value.astype(BF16)
    rest = value - hi.astype(F32)
    mid = rest.astype(BF16)
    split_ref[0:GATE_ROWS, :] = hi
    split_ref[GATE_ROWS:2 * GATE_ROWS, :] = mid
    split_ref[2 * GATE_ROWS:3 * GATE_ROWS, :] = (rest - mid.astype(F32)).astype(BF16)
    split_ref[3 * GATE_ROWS:, :] = jnp.zeros((LANES - 3 * GATE_ROWS, seq), BF16)


def _gates(gt, bias_col, alog_col, *, batch, seq):
    return pl.pallas_call(
        _gates_body,
        out_shape=(jax.ShapeDtypeStruct(gt.shape, F32), jax.ShapeDtypeStruct((LANES, gt.shape[1]), BF16)),
        grid=(batch,),
        in_specs=[
            pl.BlockSpec((GATE_ROWS, seq), lambda b: (0, b)),
            pl.BlockSpec((GATE_ROWS, 1), lambda b: (0, 0)),
            pl.BlockSpec((GATE_ROWS, 1), lambda b: (0, 0)),
        ],
        out_specs=(pl.BlockSpec((GATE_ROWS, seq), lambda b: (0, b)), pl.BlockSpec((LANES, seq), lambda b: (0, b))),
        compiler_params=_params(("parallel",)),
        name="gates",
    )(gt, bias_col, alog_col)


def _mlstm_body(q_ref, k_ref, v_ref, o_ref, gb_ref, gd_ref, gc3_ref, sel_ref, mean_ref, nw_ref, y_ref,
                ct_scr, m_scr):
    @pl.when(pl.program_id(1) == 0)
    def _():
        ct_scr[...] = jnp.zeros(ct_scr.shape, F32)
        m_scr[...] = jnp.zeros(m_scr.shape, F32)

    heads = range(MLSTM_HEADS)
    causal = (lax.broadcasted_iota(jnp.int32, (CHUNK, CHUNK), 0)
              >= lax.broadcasted_iota(jnp.int32, (CHUNK, CHUNK), 1))
    ones = jnp.ones((CHUNK, HEAD_DIM), BF16)
    cts = [ct_scr[h] for h in heads]
    ms = [m_scr[h:h + 1, 0:1] for h in heads]
    for r in range(q_ref.shape[0] // CHUNK):
        rows = slice(r * CHUNK, (r + 1) * CHUNK)
        hcols = [slice(h * HEAD_DIM, (h + 1) * HEAD_DIM) for h in heads]
        q = [q_ref[rows, c] for c in hcols]
        k = [k_ref[rows, c] for c in hcols]
        v_aug = [jnp.concatenate([v_ref[rows, c], ones], axis=1) for c in hcols]
        cols_rep = jnp.dot(gc3_ref[rows, :], sel_ref[...], preferred_element_type=F32)
        rep = lambda h, j: cols_rep[:, (3 * h + j) * LANES:(3 * h + j + 1) * LANES]
        a_col = [rep(h, 0) for h in heads]
        d_col = [rep(h, 1) for h in heads]
        b_col = [rep(h, 2) for h in heads]
        d_row = [gd_ref[h:h + 1, rows] for h in heads]
        b_last = [gb_ref[4 + h:5 + h, rows][:, CHUNK - 1:CHUNK] for h in heads]
        a_last = [gd_ref[4 + h:5 + h, rows][:, CHUNK - 1:CHUNK] for h in heads]

        qk = [lax.dot_general(q[h], k[h], NT_DIMS, preferred_element_type=F32) for h in heads]
        q_state = [lax.dot_general(q[h], cts[h].astype(BF16), NT_DIMS, preferred_element_type=F32)
                   for h in heads]
        m_rel = [jnp.maximum(ms[h], a_col[h]) for h in heads]
        p = [jnp.exp(jnp.where(causal, d_row[h] - m_rel[h], -jnp.inf)) * (qk[h] * QK_SCALE) for h in heads]
        intra = [jnp.dot(p[h].astype(BF16), v_aug[h], preferred_element_type=F32) for h in heads]
        for h in heads:
            w_inter = jnp.exp(ms[h] - m_rel[h])
            both = jnp.concatenate([w_inter, w_inter], axis=1) * q_state[h] + intra[h]
            floor = jnp.exp(-(b_col[h] + m_rel[h]))
            h_out = both[:, :HEAD_DIM] / jnp.maximum(jnp.abs(both[:, HEAD_DIM:]), floor)
            mean = jnp.dot(h_out.astype(BF16), mean_ref[...], preferred_element_type=F32)
            centered = h_out - mean
            var = jnp.dot((centered * centered).astype(BF16), mean_ref[...], preferred_element_type=F32)
            gate = _sigmoid(o_ref[rows, hcols[h]].astype(F32))
            y_ref[rows, hcols[h]] = (centered * lax.rsqrt(var + NORM_EPS) * nw_ref[:, hcols[h]]
                                     * gate).astype(BF16)

        m_end = [jnp.maximum(ms[h], a_last[h]) for h in heads]
        kw = [(k[h].astype(F32) * (jnp.exp(d_col[h] - m_end[h]) * QK_SCALE)).astype(BF16) for h in heads]
        cts = [jnp.exp(ms[h] - m_end[h]) * cts[h]
               + lax.dot_general(v_aug[h], kw[h], TN_DIMS, preferred_element_type=F32) for h in heads]
        ms = [b_last[h] + m_end[h] for h in heads]
    for h in heads:
        ct_scr[h] = cts[h]
        m_scr[h:h + 1, :] = jnp.broadcast_to(ms[h], (1, LANES))


def _one_hot_columns(gate_cols):
    sel = np.zeros((LANES, LANES * len(gate_cols)), np.float32)
    for j, col in enumerate(gate_cols):
        for part in range(3):
            sel[part * GATE_ROWS + col, j * LANES:(j + 1) * LANES] = 1.0
    return jnp.asarray(sel, BF16)


def _mlstm(proj, gr, gc3, norm_w, *, batch, seq):
    m = proj.shape[0]
    nc = seq // MLSTM_STEP
    row = lambda b, c: b * nc + c
    sel = _one_hot_columns([c for h in range(MLSTM_HEADS) for c in (28 + h, 24 + h, 4 + h)])
    mean_w = jnp.full((HEAD_DIM, HEAD_DIM), 1.0 / HEAD_DIM, BF16)
    return pl.pallas_call(
        _mlstm_body,
        out_shape=jax.ShapeDtypeStruct((m, MLSTM_W), BF16),
        grid=(batch, nc),
        in_specs=[
            pl.BlockSpec((MLSTM_STEP, MLSTM_W), lambda b, c: (row(b, c), 0)),
            pl.BlockSpec((MLSTM_STEP, MLSTM_W), lambda b, c: (row(b, c), 1)),
            pl.BlockSpec((MLSTM_STEP, MLSTM_W), lambda b, c: (row(b, c), 2)),
            pl.BlockSpec((MLSTM_STEP, MLSTM_W), lambda b, c: (row(b, c), 3)),
            pl.BlockSpec((SUBLANES, MLSTM_STEP), lambda b, c: (0, row(b, c))),
            pl.BlockSpec((SUBLANES, MLSTM_STEP), lambda b, c: (3, row(b, c))),
            pl.BlockSpec((MLSTM_STEP, LANES), lambda b, c: (row(b, c), 0)),
            pl.BlockSpec(sel.shape, lambda b, c: (0, 0)),
            pl.BlockSpec((HEAD_DIM, HEAD_DIM), lambda b, c: (0, 0)),
            pl.BlockSpec((1, MLSTM_W), lambda b, c: (0, 0)),
        ],
        out_specs=pl.BlockSpec((MLSTM_STEP, MLSTM_W), lambda b, c: (row(b, c), 0)),
        scratch_shapes=[
            pltpu.VMEM((MLSTM_HEADS, 2 * HEAD_DIM, HEAD_DIM), F32),
            pltpu.VMEM((SUBLANES, LANES), F32),
        ],
        compiler_params=_params(("parallel", "arbitrary")),
        name="mlstm",
    )(proj, proj, proj, proj, gr, gr, gc3, sel, mean_w, norm_w)


def _fox_body(q_ref, k_ref, v_ref, g3_ref, c_ref, nw_ref, y_ref, acc_scr, m_scr, l_scr, qa_scr, vt_scr):
    pair = pl.program_id(1)
    qi = pl.program_id(2)
    tq = q_ref.shape[0]
    heads = range(FOX_PAIR)
    hcols = [slice(e * HEAD_DIM, (e + 1) * HEAD_DIM) for e in heads]

    @pl.when(qi == 0)
    def _():
        for e in heads:
            for j in range(vt_scr.shape[1]):
                vt_scr[e, j] = v_ref[j * tq:(j + 1) * tq, hcols[e]].T

    lane = lax.broadcasted_iota(jnp.int32, (tq, LANES), 1)
    for e in heads:
        fox_row = 2 * MLSTM_HEADS + FOX_PAIR * pair + e
        split_lane = (lane == fox_row) | (lane == GATE_ROWS + fox_row) | (lane == 2 * GATE_ROWS + fox_row)
        qa_scr[e, :, 0:HEAD_DIM] = (q_ref[:, hcols[e]].astype(F32) * (QK_SCALE * LOG2E)).astype(BF16)
        qa_scr[e, :, HEAD_DIM:] = jnp.where(split_lane, -1.0, 0.0).astype(BF16)
    c_q = [c_ref[e, qi] for e in heads]
    m_scr[...] = jnp.full(m_scr.shape, -jnp.inf, F32)
    l_scr[...] = jnp.zeros(l_scr.shape, F32)
    acc_scr[...] = jnp.zeros(acc_scr.shape, F32)

    def logits(kb, diagonal):
        base = pl.multiple_of(kb * tq, tq)
        gates = g3_ref[pl.ds(base, tq), :]
        zs = []
        for e in heads:
            keys = jnp.concatenate([k_ref[pl.ds(base, tq), hcols[e]], gates], axis=1)
            z = lax.dot_general(keys, qa_scr[e], NT_DIMS, preferred_element_type=F32)
            if diagonal:
                visible = (lax.broadcasted_iota(jnp.int32, (tq, tq), 0)
                           <= lax.broadcasted_iota(jnp.int32, (tq, tq), 1))
                z = jnp.where(visible, z, -jnp.inf)
            zs.append(z)
        return zs

    def update(kb, zs):
        ps, alphas = [], []
        for e in heads:
            m_prev = m_scr[e, 0:1, :]
            m_new = jnp.maximum(m_prev, jnp.max(zs[e], axis=0, keepdims=True) + c_q[e])
            alpha = jnp.exp2(m_prev - m_new)
            p = jnp.exp2(zs[e] - (m_new - c_q[e]))
            l_scr[e, 0:1, :] = alpha * l_scr[e, 0:1, :] + jnp.sum(p, axis=0, keepdims=True)
            m_scr[e, 0:1, :] = m_new
            ps.append(p.astype(BF16))
            alphas.append(alpha)
        for e in heads:
            acc_scr[e] = alphas[e] * acc_scr[e] + jnp.dot(vt_scr[e, kb], ps[e], preferred_element_type=F32)

    def blocks(first, count):
        zs = [logits(first + j, False) for j in range(count)]
        for j in range(count):
            update(first + j, zs[j])

    def group(g, carry):
        blocks(g * FOX_GROUP, FOX_GROUP)
        return carry

    lax.fori_loop(0, qi // FOX_GROUP, group, 0)
    done = (qi // FOX_GROUP) * FOX_GROUP
    size = FOX_GROUP // 2
    while size >= 1:
        take = (qi & size) != 0
        pl.when(take)(functools.partial(blocks, done, size))
        done = done + jnp.where(take, size, 0)
        size //= 2
    update(qi, logits(qi, True))
    for e in heads:
        out = (acc_scr[e] / l_scr[e, 0:1, :]).T
        y_ref[:, hcols[e]] = (_rms(out) * nw_ref[:, hcols[e]]).astype(BF16)


def _fox(proj, g3, c_rows, norm_w, *, batch, seq):
    m = proj.shape[0]
    nq = seq // FOX_BLOCK
    width = FOX_PAIR * HEAD_DIM
    q_col = (4 * MLSTM_W) // width
    k_col = q_col + FOX_W // width
    v_col = k_col + FOX_W // width
    return pl.pallas_call(
        _fox_body,
        out_shape=jax.ShapeDtypeStruct((m, FOX_W), BF16),
        grid=(batch, FOX_HEADS // FOX_PAIR, nq),
        in_specs=[
            pl.BlockSpec((FOX_BLOCK, width), lambda b, h, i: (b * nq + i, q_col + h)),
            pl.BlockSpec((seq, width), lambda b, h, i: (b, k_col + h)),
            pl.BlockSpec((seq, width), lambda b, h, i: (b, v_col + h)),
            pl.BlockSpec((seq, LANES), lambda b, h, i: (b, 0)),
            pl.BlockSpec((FOX_PAIR, nq, 1, FOX_BLOCK), lambda b, h, i: (h, b, 0, 0)),
            pl.BlockSpec((1, width), lambda b, h, i: (0, h)),
        ],
        out_specs=pl.BlockSpec((FOX_BLOCK, width), lambda b, h, i: (b * nq + i, h)),
        scratch_shapes=[
            pltpu.VMEM((FOX_PAIR, HEAD_DIM, FOX_BLOCK), F32),
            pltpu.VMEM((FOX_PAIR, SUBLANES, FOX_BLOCK), F32),
            pltpu.VMEM((FOX_PAIR, SUBLANES, FOX_BLOCK), F32),
            pltpu.VMEM((FOX_PAIR, FOX_BLOCK, 2 * HEAD_DIM), BF16),
            pltpu.VMEM((FOX_PAIR, nq, HEAD_DIM, FOX_BLOCK), BF16),
        ],
        compiler_params=_params(("parallel", "parallel", "arbitrary")),
        name="fox",
    )(proj, proj, proj, g3, c_rows, norm_w)


def _mm(x, y):
    return jnp.dot(x.astype(BF16), y.astype(BF16), preferred_element_type=F32)


def _unit_lower_inverses(a_mats, mask_ref):
    eye = mask_ref[MASK_INCL] - mask_ref[MASK_STRICT]
    negs = [-(a * mask_ref[MASK_DIAG8]) for a in a_mats]
    invs = [eye + n for n in negs]
    for _ in range(2):
        negs = [_mm(n, n) for n in negs]
        invs = [i + _mm(i, n) for i, n in zip(invs, negs)]
    for level in range(MASK_OFF0, MASK_COUNT):
        tmps = [_mm(a * mask_ref[level], i) for a, i in zip(a_mats, invs)]
        invs = [i - _mm(i, t) for i, t in zip(invs, tmps)]
    return invs


def _gdn_body(q_ref, k_ref, v_ref, z_ref, cw_ref, gr_ref, gc3_ref, sel_ref, ones_ref, nw_ref, y_ref,
              s_scr, ext_scr, mask_scr):
    step = q_ref.shape[0]
    halo = SUBLANES
    nchunk = step // CHUNK

    @pl.when(pl.program_id(1) == 0)
    def _():
        s_scr[...] = jnp.zeros(s_scr.shape, F32)
        ext_scr[0:halo, :] = jnp.zeros((halo, 3 * GDN_W), F32)
        ri = lax.broadcasted_iota(jnp.int32, (CHUNK, CHUNK), 0)
        ci = lax.broadcasted_iota(jnp.int32, (CHUNK, CHUNK), 1)
        mask_scr[MASK_STRICT] = jnp.where(ri > ci, 1.0, 0.0)
        mask_scr[MASK_INCL] = jnp.where(ri >= ci, 1.0, 0.0)
        mask_scr[MASK_DIAG8] = jnp.where((ri >> 3) == (ci >> 3), 1.0, 0.0)
        for bits in range(3, 3 + MASK_COUNT - MASK_OFF0):
            sibling = ((ri >> (bits + 1)) == (ci >> (bits + 1))) & ((ri >> bits) != (ci >> bits))
            mask_scr[MASK_OFF0 + bits - 3] = jnp.where(sibling, 1.0, 0.0)

    ext_scr[halo:halo + step, 0:GDN_W] = q_ref[...].astype(F32)
    ext_scr[halo:halo + step, GDN_W:2 * GDN_W] = k_ref[...].astype(F32)
    ext_scr[halo:halo + step, 2 * GDN_W:3 * GDN_W] = v_ref[...].astype(F32)
    ext = ext_scr[...]
    conv = ext * cw_ref[0:1, :]
    for j in range(1, CONV_WIDTH):
        conv = pltpu.roll(conv, 1, 0) + ext * cw_ref[j:j + 1, :]
    conv = conv[halo:, :]
    ext_scr[0:halo, :] = ext_scr[step:step + halo, :]
    qkv = conv * _sigmoid(conv)

    def row_sum(t):
        return jnp.dot(t.astype(BF16), ones_ref[...], preferred_element_type=F32)

    def l2n(t):
        return t * lax.rsqrt(row_sum(t * t) + NORM_EPS)

    cols_rep = [jnp.dot(gc3_ref[r * CHUNK:(r + 1) * CHUNK, :], sel_ref[...], preferred_element_type=F32)
                for r in range(nchunk)]

    groups = [(h, r) for h in range(GDN_HEADS) for r in range(nchunk)]
    prep = {}
    for h, r in groups:
        rows = slice(r * CHUNK, (r + 1) * CHUNK)
        q = l2n(qkv[rows, h * HEAD_DIM:(h + 1) * HEAD_DIM]) * QK_SCALE
        k = l2n(qkv[rows, GDN_W + h * HEAD_DIM:GDN_W + (h + 1) * HEAD_DIM])
        v = qkv[rows, 2 * GDN_W + h * HEAD_DIM:2 * GDN_W + (h + 1) * HEAD_DIM]
        g_row = gr_ref[h:h + 1, rows]
        g_col = cols_rep[r][:, (2 * h) * LANES:(2 * h + 1) * LANES]
        beta = cols_rep[r][:, (2 * h + 1) * LANES:(2 * h + 2) * LANES]
        incl = mask_scr[MASK_INCL]
        decay = jnp.exp((g_col - g_row) * incl) * incl
        kb = k * beta
        e_g = jnp.exp(g_col)
        g_last = g_row[:, CHUNK - 1:CHUNK]
        kk_qk = lax.dot_general(jnp.concatenate([kb, q], axis=0).astype(BF16), k.astype(BF16), NT_DIMS,
                                preferred_element_type=F32)
        prep[h, r] = dict(
            a=kk_qk[:CHUNK] * (decay * mask_scr[MASK_STRICT]),
            attn=kk_qk[CHUNK:] * decay,
            rhs=jnp.concatenate([v * beta, kb * e_g], axis=1),
            q_dec=q * e_g,
            k_end=k * jnp.exp(g_last - g_col),
            s_decay=jnp.exp(g_last),
        )
    invs = _unit_lower_inverses([prep[g]["a"] for g in groups], mask_scr)
    for g, inv in zip(groups, invs):
        w = _mm(inv, prep[g]["rhs"])
        prep[g]["w_val"], prep[g]["w_key"] = w[:, :HEAD_DIM], w[:, HEAD_DIM:]

    states = [s_scr[h] for h in range(GDN_HEADS)]
    for r in range(nchunk):
        rows = slice(r * CHUNK, (r + 1) * CHUNK)
        ks = [_mm(jnp.concatenate([prep[h, r]["w_key"], prep[h, r]["q_dec"]], axis=0), states[h])
              for h in range(GDN_HEADS)]
        us = [prep[h, r]["w_val"] - ks[h][:CHUNK] for h in range(GDN_HEADS)]
        outs = [ks[h][CHUNK:] + _mm(prep[h, r]["attn"], us[h]) for h in range(GDN_HEADS)]
        states = [prep[h, r]["s_decay"] * states[h]
                  + lax.dot_general(prep[h, r]["k_end"].astype(BF16), us[h].astype(BF16), TN_DIMS,
                                    preferred_element_type=F32) for h in range(GDN_HEADS)]
        for h in range(GDN_HEADS):
            cols = slice(h * HEAD_DIM, (h + 1) * HEAD_DIM)
            gate = z_ref[rows, cols].astype(F32)
            normed = outs[h] * lax.rsqrt(row_sum(outs[h] * outs[h]) * (1.0 / HEAD_DIM) + NORM_EPS)
            y_ref[rows, cols] = (normed * nw_ref[:, cols] * (gate * _sigmoid(gate))).astype(BF16)
    for h in range(GDN_HEADS):
        s_scr[h] = states[h]


def _gdn(proj, conv_w, gr, gc3, norm_w, *, batch, seq):
    m = proj.shape[0]
    ns = seq // GDN_STEP
    sel = _one_hot_columns([c for h in range(GDN_HEADS) for c in (16 + h, 20 + h)])
    col0 = (4 * MLSTM_W + 3 * FOX_W) // GDN_W
    row = lambda b, s: b * ns + s
    return pl.pallas_call(
        _gdn_body,
        out_shape=jax.ShapeDtypeStruct((m, GDN_W), BF16),
        grid=(batch, ns),
        in_specs=[
            pl.BlockSpec((GDN_STEP, GDN_W), lambda b, s: (row(b, s), col0)),
            pl.BlockSpec((GDN_STEP, GDN_W), lambda b, s: (row(b, s), col0 + 1)),
            pl.BlockSpec((GDN_STEP, GDN_W), lambda b, s: (row(b, s), col0 + 2)),
            pl.BlockSpec((GDN_STEP, GDN_W), lambda b, s: (row(b, s), col0 + 3)),
            pl.BlockSpec((CONV_WIDTH, 3 * GDN_W), lambda b, s: (0, 0)),
            pl.BlockSpec((SUBLANES, GDN_STEP), lambda b, s: (2, row(b, s))),
            pl.BlockSpec((GDN_STEP, LANES), lambda b, s: (row(b, s), 0)),
            pl.BlockSpec(sel.shape, lambda b, s: (0, 0)),
            pl.BlockSpec((HEAD_DIM, HEAD_DIM), lambda b, s: (0, 0)),
            pl.BlockSpec((1, GDN_W), lambda b, s: (0, 0)),
        ],
        out_specs=pl.BlockSpec((GDN_STEP, GDN_W), lambda b, s: (row(b, s), 0)),
        scratch_shapes=[
            pltpu.VMEM((GDN_HEADS, HEAD_DIM, HEAD_DIM), F32),
            pltpu.VMEM((GDN_STEP + SUBLANES, 3 * GDN_W), F32),
            pltpu.VMEM((MASK_COUNT, CHUNK, CHUNK), F32),
        ],
        compiler_params=_params(("parallel", "arbitrary")),
        name="gdn",
    )(proj, proj, proj, proj, conv_w, gr, gc3, sel, jnp.ones((HEAD_DIM, HEAD_DIM), BF16), norm_w)


def _outproj_body(ym_ref, yf_ref, yg_ref, w_ref, x_ref, o_ref):
    acc = jnp.dot(ym_ref[...], w_ref[0:MLSTM_W, :], preferred_element_type=F32)
    acc += jnp.dot(yf_ref[...], w_ref[MLSTM_W:MLSTM_W + FOX_W, :], preferred_element_type=F32)
    acc += jnp.dot(yg_ref[...], w_ref[MLSTM_W + FOX_W:, :], preferred_element_type=F32)
    o_ref[...] = x_ref[...] + acc


def _outproj(y_m, y_f, y_g, w_out, layer, x, *, tm=512):
    m = x.shape[0]
    return pl.pallas_call(
        _outproj_body,
        out_shape=jax.ShapeDtypeStruct((m, D_MODEL), F32),
        grid=(m // tm,),
        in_specs=[
            pl.BlockSpec((tm, MLSTM_W), lambda i: (i, 0)),
            pl.BlockSpec((tm, FOX_W), lambda i: (i, 0)),
            pl.BlockSpec((tm, GDN_W), lambda i: (i, 0)),
            pl.BlockSpec((None, D_MODEL, D_MODEL), lambda i: (layer, 0, 0)),
            pl.BlockSpec((tm, D_MODEL), lambda i: (i, 0)),
        ],
        out_specs=pl.BlockSpec((tm, D_MODEL), lambda i: (i, 0)),
        compiler_params=_params(("parallel",)),
        name="outproj",
    )(y_m, y_f, y_g, w_out, x)


def _ffn_body(x_ref, nw_ref, wg_ref, wu_ref, wd_ref, fw_ref, o_ref, h_scr, acc_scr, *, final_norm):
    j = pl.program_id(1)

    @pl.when(j == 0)
    def _():
        h_scr[...] = (_rms(x_ref[...]) * nw_ref[...]).astype(BF16)
        acc_scr[...] = jnp.zeros(acc_scr.shape, F32)

    h = h_scr[...]
    gate = jnp.dot(h, wg_ref[...], preferred_element_type=F32)
    up = jnp.dot(h, wu_ref[...], preferred_element_type=F32)
    act = (gate * _sigmoid(gate) * up).astype(BF16)
    acc_scr[...] += jnp.dot(act, wd_ref[...], preferred_element_type=F32)

    @pl.when(j == pl.num_programs(1) - 1)
    def _():
        out = x_ref[...] + acc_scr[...]
        if final_norm:
            out = _rms(out) * fw_ref[...]
        o_ref[...] = out


def _ffn(x, norm_w, w_gate, w_up, w_down, layer, final_w, *, final_norm, tm=512, tf=512):
    m = x.shape[0]
    return pl.pallas_call(
        functools.partial(_ffn_body, final_norm=final_norm),
        out_shape=jax.ShapeDtypeStruct((m, D_MODEL), F32),
        grid=(m // tm, D_FF // tf),
        in_specs=[
            pl.BlockSpec((tm, D_MODEL), lambda i, j: (i, 0)),
            pl.BlockSpec((1, D_MODEL), lambda i, j: (0, 0)),
            pl.BlockSpec((None, D_MODEL, tf), lambda i, j: (layer, 0, j)),
            pl.BlockSpec((None, D_MODEL, tf), lambda i, j: (layer, 0, j)),
            pl.BlockSpec((None, tf, D_MODEL), lambda i, j: (layer, j, 0)),
            pl.BlockSpec((1, D_MODEL), lambda i, j: (0, 0)),
        ],
        out_specs=pl.BlockSpec((tm, D_MODEL), lambda i, j: (i, 0)),
        scratch_shapes=[pltpu.VMEM((tm, D_MODEL), BF16), pltpu.VMEM((tm, D_MODEL), F32)],
        compiler_params=_params(("parallel", "arbitrary")),
        name="ffn",
    )(x, norm_w, w_gate, w_up, w_down, final_w)


def _prep_w_in_body(w_ref, wide_ref, gate_ref):
    o_mi = 4 * MLSTM_W
    o_fq = o_mi + 2 * MLSTM_HEADS
    o_ff = o_fq + 3 * FOX_W
    o_gq = o_ff + FOX_HEADS
    o_ga = o_gq + 4 * GDN_W
    wide_ref[0:o_mi, :] = w_ref[0:o_mi, :].astype(BF16)
    wide_ref[o_mi:o_mi + 3 * FOX_W, :] = w_ref[o_fq:o_ff, :].astype(BF16)
    wide_ref[o_mi + 3 * FOX_W:, :] = w_ref[o_gq:o_ga, :].astype(BF16)
    gate = jnp.concatenate([w_ref[o_mi:o_fq, :], w_ref[o_ff:o_gq, :], w_ref[o_ga:, :],
                            jnp.zeros((SUBLANES, w_ref.shape[1]), F32)], axis=0)
    gate_ref[...] = gate.astype(BF16)


def _prep_w_in(w_in_t, layer, *, cols=256):
    _, n_in, d = w_in_t.shape
    return pl.pallas_call(
        _prep_w_in_body,
        out_shape=(jax.ShapeDtypeStruct((PROJ_W, d), BF16), jax.ShapeDtypeStruct((GATE_ROWS, d), BF16)),
        grid=(d // cols,),
        in_specs=[pl.BlockSpec((None, n_in, cols), lambda i: (layer, 0, i))],
        out_specs=(pl.BlockSpec((PROJ_W, cols), lambda i: (0, i)), pl.BlockSpec((GATE_ROWS, cols), lambda i: (0, i))),
        compiler_params=_params(("parallel",)),
        name="prep_w_in",
    )(w_in_t)


def kernel(x, mix_norm_w, w_in, mlstm_i_bias, mlstm_f_bias, fox_f_bias, gdn_conv_w, gdn_a_log, gdn_dt_bias,
           mlstm_out_norm_w, fox_out_norm_w, gdn_out_norm_w, w_out, ffn_norm_w, w_gate, w_up, w_down,
           final_norm_w):
    batch, seq, d_model = x.shape
    depth = w_in.shape[0]
    assert d_model == D_MODEL and seq % FOX_BLOCK == 0 and seq % GDN_STEP == 0
    xf = x.reshape(batch * seq, d_model)
    zeros4 = jnp.zeros((GDN_HEADS,), F32)
    w_in_t = jnp.swapaxes(w_in, 1, 2)
    w_out_b, w_gate_b, w_up_b, w_down_b = (w.astype(BF16) for w in (w_out, w_gate, w_up, w_down))
    for l in range(depth):
        w_wide, w_gate_t = _prep_w_in(w_in_t, l)
        bias_col = jnp.pad(jnp.concatenate([mlstm_i_bias[l], mlstm_f_bias[l], fox_f_bias[l], gdn_dt_bias[l],
                                            zeros4]), (0, 8)).reshape(GATE_ROWS, 1)
        alog_col = jnp.pad(gdn_a_log[l], (16, 12)).reshape(GATE_ROWS, 1)

        proj, gt = _inproj(xf, mix_norm_w[l].reshape(1, -1), w_wide, w_gate_t)
        gr, gr3 = _gates(gt, bias_col, alog_col, batch=batch, seq=seq)
        gc3 = gr3.T
        c_rows = gr[8:16].reshape(FOX_HEADS, batch * (seq // FOX_BLOCK), 1, FOX_BLOCK)

        y_m = _mlstm(proj, gr, gc3, mlstm_out_norm_w[l].reshape(1, -1), batch=batch, seq=seq)
        y_f = _fox(proj, gc3, c_rows, fox_out_norm_w[l].reshape(1, -1), batch=batch, seq=seq)
        y_g = _gdn(proj, gdn_conv_w[l], gr, gc3, gdn_out_norm_w[l].reshape(1, -1), batch=batch, seq=seq)
        xf = _outproj(y_m, y_f, y_g, w_out_b, l, xf)
        xf = _ffn(xf, ffn_norm_w[l].reshape(1, -1), w_gate_b, w_up_b, w_down_b, l, final_norm_w.reshape(1, -1),
                  final_norm=(l == depth - 1))
    return xf.reshape(batch, seq, d_model)
```

```python
import functools

import jax
import jax.numpy as jnp
import numpy as np
from jax import lax
from jax.experimental import pallas as pl
from jax.experimental.pallas import tpu as pltpu

F32 = jnp.float32
BF16 = jnp.bfloat16

D_MODEL = 2048
HEAD_DIM = 128
MLSTM_HEADS = 4
FOX_HEADS = 8
GDN_HEADS = 4
MLSTM_W = MLSTM_HEADS * HEAD_DIM
FOX_W = FOX_HEADS * HEAD_DIM
GDN_W = GDN_HEADS * HEAD_DIM
D_FF = 5632
CONV_WIDTH = 4
GATE_SOFTCAP = 15.0
NORM_EPS = 1e-6
QK_SCALE = HEAD_DIM ** -0.5
LOG2E = 1.4426950408889634

PROJ_W = 4 * MLSTM_W + 3 * FOX_W + 4 * GDN_W
GATE_ROWS = 32

LANES = 128
SUBLANES = 8
V7X_VMEM_BYTES = 64 * 1024 * 1024
VMEM_LIMIT = 56 * 1024 * 1024

CHUNK = 128
FOX_BLOCK = 512
FOX_GROUP = 4
FOX_PAIR = 2
GDN_STEP = 512
MLSTM_STEP = 512

MASK_STRICT, MASK_INCL, MASK_DIAG8, MASK_OFF0 = 0, 1, 2, 3
MASK_COUNT = MASK_OFF0 + 4

NT_DIMS = (((1,), (1,)), ((), ()))
TN_DIMS = (((0,), (0,)), ((), ()))


def _params(sem):
    return pltpu.CompilerParams(dimension_semantics=sem, vmem_limit_bytes=VMEM_LIMIT)


def _sigmoid(z):
    return 1.0 / (1.0 + jnp.exp(-z))


def _log_sigmoid(z):
    return jnp.minimum(z, 0.0) - jnp.log1p(jnp.exp(-jnp.abs(z)))


def _softplus(z):
    return jnp.maximum(z, 0.0) + jnp.log1p(jnp.exp(-jnp.abs(z)))


def _rms(t):
    return t * lax.rsqrt(jnp.mean(t * t, axis=-1, keepdims=True) + NORM_EPS)


def _inproj_body(*refs, row_chunk, n_side):
    x_ref, nw_ref, w_ref, wg_ref = refs[:4]
    side_in = refs[4:4 + n_side]
    proj_ref, gt_ref = refs[4 + n_side:6 + n_side]
    side_out = refs[6 + n_side:6 + 2 * n_side]
    h_scr = refs[6 + 2 * n_side]

    @pl.when(pl.program_id(1) == 0)
    def _():
        for r in range(x_ref.shape[0] // row_chunk):
            rows = slice(r * row_chunk, (r + 1) * row_chunk)
            h_scr[rows, :] = (_rms(x_ref[rows, :]) * nw_ref[...]).astype(BF16)
        gt_ref[...] = lax.dot_general(wg_ref[...], h_scr[...], NT_DIMS, preferred_element_type=F32)

    proj_ref[...] = lax.dot_general(h_scr[...], w_ref[...], NT_DIMS, preferred_element_type=F32).astype(BF16)

    for src, dst in zip(side_in, side_out):
        dst[...] = src[...].astype(BF16)


def _inproj(x, norm_w, w_big, w_gate_t, side_weights, layer, *, tm=1024, tn=1024):
    m = x.shape[0]
    ni, nj = m // tm, PROJ_W // tn
    side_in_specs, side_out_specs, side_out_shapes = [], [], []
    for w, band, first in side_weights:
        _, rows, cols = w.shape
        count = rows // band
        assert rows % band == 0 and first + count <= ni * nj

        def band_index(i, j, first=first, count=count):
            return jnp.clip(i * nj + j - first, 0, count - 1)

        side_in_specs.append(pl.BlockSpec((None, band, cols), lambda i, j, f=band_index: (layer, f(i, j), 0)))
        side_out_specs.append(pl.BlockSpec((band, cols), lambda i, j, f=band_index: (f(i, j), 0)))
        side_out_shapes.append(jax.ShapeDtypeStruct((rows, cols), BF16))
    outs = pl.pallas_call(
        functools.partial(_inproj_body, row_chunk=256, n_side=len(side_weights)),
        out_shape=(jax.ShapeDtypeStruct((m, PROJ_W), BF16), jax.ShapeDtypeStruct((GATE_ROWS, m), F32),
                   *side_out_shapes),
        grid=(ni, nj),
        in_specs=[
            pl.BlockSpec((tm, D_MODEL), lambda i, j: (i, 0)),
            pl.BlockSpec((1, D_MODEL), lambda i, j: (0, 0)),
            pl.BlockSpec((tn, D_MODEL), lambda i, j: (j, 0)),
            pl.BlockSpec((GATE_ROWS, D_MODEL), lambda i, j: (0, 0)),
            *side_in_specs,
        ],
        out_specs=(
            pl.BlockSpec((tm, tn), lambda i, j: (i, j)),
            pl.BlockSpec((GATE_ROWS, tm), lambda i, j: (0, i)),
            *side_out_specs,
        ),
        scratch_shapes=[pltpu.VMEM((tm, D_MODEL), BF16)],
        compiler_params=_params(("arbitrary", "arbitrary")),
        name="inproj",
    )(x, norm_w, w_big, w_gate_t, *[w for w, _, _ in side_weights])
    return outs[0], outs[1], outs[2:]


def _lane_cumsum(v, seg):
    pos = lax.broadcasted_iota(jnp.int32, v.shape, 1) & (seg - 1)
    shift = 1
    while shift < seg:
        v = v + jnp.where(pos >= shift, pltpu.roll(v, shift, 1), 0.0)
        shift *= 2
    return v


def _lane_cummax(v, seg):
    pos = lax.broadcasted_iota(jnp.int32, v.shape, 1) & (seg - 1)
    shift = 1
    while shift < seg:
        v = jnp.maximum(v, jnp.where(pos >= shift, pltpu.roll(v, shift, 1), -jnp.inf))
        shift *= 2
    return v


def _gates_body(gt_ref, bias_ref, alog_ref, out_ref, split_ref):
    seq = gt_ref.shape[1]
    first4 = lax.broadcasted_iota(jnp.int32, (SUBLANES, seq), 0) < 4
    z = gt_ref[0:8, :] + bias_ref[0:8, :]
    z = GATE_SOFTCAP * jnp.tanh(z / GATE_SOFTCAP)
    b_cum = _lane_cumsum(_log_sigmoid(z), CHUNK)
    out_ref[0:8, :] = jnp.where(first4, z, b_cum)
    d = z - pltpu.roll(b_cum, 4, 0)
    out_ref[24:32, :] = jnp.where(first4, d, pltpu.roll(_lane_cummax(d, CHUNK), 4, 0))

    raw = gt_ref[16:24, :]
    decay = _lane_cumsum(-jnp.exp(alog_ref[16:24, :]) * _softplus(raw + bias_ref[16:24, :]), CHUNK)
    out_ref[16:24, :] = jnp.where(first4, decay, _sigmoid(raw))

    fox = _lane_cumsum(LOG2E * _log_sigmoid(gt_ref[8:16, :] + bias_ref[8:16, :]), LANES)
    carry = jnp.zeros((SUBLANES, 1), F32)
    for blk in range(seq // LANES):
        lanes = slice(blk * LANES, (blk + 1) * LANES)
        tile = fox[:, lanes] + carry
        out_ref[8:16, lanes] = tile
        carry = tile[:, LANES - 1:LANES]

    value = out_ref[...]
    hi = value.astype(BF16)
    rest = value - hi.astype(F32)
    mid = rest.astype(BF16)
    split_ref[0:GATE_ROWS, :] = hi
    split_ref[GATE_ROWS:2 * GATE_ROWS, :] = mid
    split_ref[2 * GATE_ROWS:3 * GATE_ROWS, :] = (rest - mid.astype(F32)).astype(BF16)
    split_ref[3 * GATE_ROWS:, :] = jnp.zeros((LANES - 3 * GATE_ROWS, seq), BF16)


def _gates(gt, bias_col, alog_col, *, batch, seq):
    return pl.pallas_call(
        _gates_body,
        out_shape=(jax.ShapeDtypeStruct(gt.shape, F32), jax.ShapeDtypeStruct((LANES, gt.shape[1]), BF16)),
        grid=(batch,),
        in_specs=[
            pl.BlockSpec((GATE_ROWS, seq), lambda b: (0, b)),
            pl.BlockSpec((GATE_ROWS, 1), lambda b: (0, 0)),
            pl.BlockSpec((GATE_ROWS, 1), lambda b: (0, 0)),
        ],
        out_specs=(pl.BlockSpec((GATE_ROWS, seq), lambda b: (0, b)), pl.BlockSpec((LANES, seq), lambda b: (0, b))),
        compiler_params=_params(("parallel",)),
        name="gates",
    )(gt, bias_col, alog_col)


def _mlstm_body(q_ref, k_ref, v_ref, o_ref, gb_ref, gd_ref, gc3_ref, sel_ref, mean_ref, nw_ref, y_ref,
                ct_scr, m_scr):
    @pl.when(pl.program_id(1) == 0)
    def _():
        ct_scr[...] = jnp.zeros(ct_scr.shape, F32)
        m_scr[...] = jnp.zeros(m_scr.shape, F32)

    heads = range(MLSTM_HEADS)
    causal = (lax.broadcasted_iota(jnp.int32, (CHUNK, CHUNK), 0)
              >= lax.broadcasted_iota(jnp.int32, (CHUNK, CHUNK), 1))
    ones = jnp.ones((CHUNK, HEAD_DIM), BF16)
    cts = [ct_scr[h] for h in heads]
    ms = [m_scr[h:h + 1, 0:1] for h in heads]
    for r in range(q_ref.shape[0] // CHUNK):
        rows = slice(r * CHUNK, (r + 1) * CHUNK)
        hcols = [slice(h * HEAD_DIM, (h + 1) * HEAD_DIM) for h in heads]
        q = [q_ref[rows, c] for c in hcols]
        k = [k_ref[rows, c] for c in hcols]
        v_aug = [jnp.concatenate([v_ref[rows, c], ones], axis=1) for c in hcols]
        cols_rep = jnp.dot(gc3_ref[rows, :], sel_ref[...], preferred_element_type=F32)
        rep = lambda h, j: cols_rep[:, (3 * h + j) * LANES:(3 * h + j + 1) * LANES]
        a_col = [rep(h, 0) for h in heads]
        d_col = [rep(h, 1) for h in heads]
        b_col = [rep(h, 2) for h in heads]
        d_row = [gd_ref[h:h + 1, rows] for h in heads]
        b_last = [gb_ref[4 + h:5 + h, rows][:, CHUNK - 1:CHUNK] for h in heads]
        a_last = [gd_ref[4 + h:5 + h, rows][:, CHUNK - 1:CHUNK] for h in heads]

        qk = [lax.dot_general(q[h], k[h], NT_DIMS, preferred_element_type=F32) for h in heads]
        q_state = [lax.dot_general(q[h], cts[h].astype(BF16), NT_DIMS, preferred_element_type=F32)
                   for h in heads]
        m_rel = [jnp.maximum(ms[h], a_col[h]) for h in heads]
        p = [jnp.exp(jnp.where(causal, d_row[h] - m_rel[h], -jnp.inf)) * (qk[h] * QK_SCALE) for h in heads]
        intra = [jnp.dot(p[h].astype(BF16), v_aug[h], preferred_element_type=F32) for h in heads]
        for h in heads:
            w_inter = jnp.exp(ms[h] - m_rel[h])
            both = jnp.concatenate([w_inter, w_inter], axis=1) * q_state[h] + intra[h]
            floor = jnp.exp(-(b_col[h] + m_rel[h]))
            h_out = both[:, :HEAD_DIM] / jnp.maximum(jnp.abs(both[:, HEAD_DIM:]), floor)
            mean = jnp.dot(h_out.astype(BF16), mean_ref[...], preferred_element_type=F32)
            centered = h_out - mean
            var = jnp.dot((centered * centered).astype(BF16), mean_ref[...], preferred_element_type=F32)
            gate = _sigmoid(o_ref[rows, hcols[h]].astype(F32))
            y_ref[rows, hcols[h]] = (centered * lax.rsqrt(var + NORM_EPS) * nw_ref[:, hcols[h]]
                                     * gate).astype(BF16)

        m_end = [jnp.maximum(ms[h], a_last[h]) for h in heads]
        kw = [(k[h].astype(F32) * (jnp.exp(d_col[h] - m_end[h]) * QK_SCALE)).astype(BF16) for h in heads]
        cts = [jnp.exp(ms[h] - m_end[h]) * cts[h]
               + lax.dot_general(v_aug[h], kw[h], TN_DIMS, preferred_element_type=F32) for h in heads]
        ms = [b_last[h] + m_end[h] for h in heads]
    for h in heads:
        ct_scr[h] = cts[h]
        m_scr[h:h + 1, :] = jnp.broadcast_to(ms[h], (1, LANES))


def _one_hot_columns(gate_cols):
    sel = np.zeros((LANES, LANES * len(gate_cols)), np.float32)
    for j, col in enumerate(gate_cols):
        for part in range(3):
            sel[part * GATE_ROWS + col, j * LANES:(j + 1) * LANES] = 1.0
    return jnp.asarray(sel, BF16)


def _mlstm(proj, gr, gc3, norm_w, *, batch, seq):
    m = proj.shape[0]
    nc = seq // MLSTM_STEP
    row = lambda b, c: b * nc + c
    sel = _one_hot_columns([c for h in range(MLSTM_HEADS) for c in (28 + h, 24 + h, 4 + h)])
    mean_w = jnp.full((HEAD_DIM, HEAD_DIM), 1.0 / HEAD_DIM, BF16)
    return pl.pallas_call(
        _mlstm_body,
        out_shape=jax.ShapeDtypeStruct((m, MLSTM_W), BF16),
        grid=(batch, nc),
        in_specs=[
            pl.BlockSpec((MLSTM_STEP, MLSTM_W), lambda b, c: (row(b, c), 0)),
            pl.BlockSpec((MLSTM_STEP, MLSTM_W), lambda b, c: (row(b, c), 1)),
            pl.BlockSpec((MLSTM_STEP, MLSTM_W), lambda b, c: (row(b, c), 2)),
            pl.BlockSpec((MLSTM_STEP, MLSTM_W), lambda b, c: (row(b, c), 3)),
            pl.BlockSpec((SUBLANES, MLSTM_STEP), lambda b, c: (0, row(b, c))),
            pl.BlockSpec((SUBLANES, MLSTM_STEP), lambda b, c: (3, row(b, c))),
            pl.BlockSpec((MLSTM_STEP, LANES), lambda b, c: (row(b, c), 0)),
            pl.BlockSpec(sel.shape, lambda b, c: (0, 0)),
            pl.BlockSpec((HEAD_DIM, HEAD_DIM), lambda b, c: (0, 0)),
            pl.BlockSpec((1, MLSTM_W), lambda b, c: (0, 0)),
        ],
        out_specs=pl.BlockSpec((MLSTM_STEP, MLSTM_W), lambda b, c: (row(b, c), 0)),
        scratch_shapes=[
            pltpu.VMEM((MLSTM_HEADS, 2 * HEAD_DIM, HEAD_DIM), F32),
            pltpu.VMEM((SUBLANES, LANES), F32),
        ],
        compiler_params=_params(("parallel", "arbitrary")),
        name="mlstm",
    )(proj, proj, proj, proj, gr, gr, gc3, sel, mean_w, norm_w)


def _fox_body(q_ref, k_ref, v_ref, g3_ref, c_ref, nw_ref, y_ref, acc_scr, m_scr, l_scr, qa_scr, vt_scr):
    pair = pl.program_id(1)
    qi = pl.program_id(2)
    tq = q_ref.shape[0]
    heads = range(FOX_PAIR)
    hcols = [slice(e * HEAD_DIM, (e + 1) * HEAD_DIM) for e in heads]

    @pl.when(qi == 0)
    def _():
        for e in heads:
            for j in range(vt_scr.shape[1]):
                vt_scr[e, j] = v_ref[j * tq:(j + 1) * tq, hcols[e]].T

    lane = lax.broadcasted_iota(jnp.int32, (tq, LANES), 1)
    for e in heads:
        fox_row = 2 * MLSTM_HEADS + FOX_PAIR * pair + e
        split_lane = (lane == fox_row) | (lane == GATE_ROWS + fox_row) | (lane == 2 * GATE_ROWS + fox_row)
        qa_scr[e, :, 0:HEAD_DIM] = (q_ref[:, hcols[e]].astype(F32) * (QK_SCALE * LOG2E)).astype(BF16)
        qa_scr[e, :, HEAD_DIM:] = jnp.where(split_lane, -1.0, 0.0).astype(BF16)
    c_q = [c_ref[e, qi] for e in heads]
    m_scr[...] = jnp.full(m_scr.shape, -jnp.inf, F32)
    l_scr[...] = jnp.zeros(l_scr.shape, F32)
    acc_scr[...] = jnp.zeros(acc_scr.shape, F32)

    def logits(kb, diagonal):
        base = pl.multiple_of(kb * tq, tq)
        gates = g3_ref[pl.ds(base, tq), :]
        zs = []
        for e in heads:
            keys = jnp.concatenate([k_ref[pl.ds(base, tq), hcols[e]], gates], axis=1)
            z = lax.dot_general(keys, qa_scr[e], NT_DIMS, preferred_element_type=F32)
            if diagonal:
                visible = (lax.broadcasted_iota(jnp.int32, (tq, tq), 0)
                           <= lax.broadcasted_iota(jnp.int32, (tq, tq), 1))
                z = jnp.where(visible, z, -jnp.inf)
            zs.append(z)
        return zs

    def update(kb, zs):
        ps, alphas = [], []
        for e in heads:
            m_prev = m_scr[e, 0:1, :]
            m_new = jnp.maximum(m_prev, jnp.max(zs[e], axis=0, keepdims=True) + c_q[e])
            alpha = jnp.exp2(m_prev - m_new)
            p = jnp.exp2(zs[e] - (m_new - c_q[e]))
            l_scr[e, 0:1, :] = alpha * l_scr[e, 0:1, :] + jnp.sum(p, axis=0, keepdims=True)
            m_scr[e, 0:1, :] = m_new
            ps.append(p.astype(BF16))
            alphas.append(alpha)
        for e in heads:
            acc_scr[e] = alphas[e] * acc_scr[e] + jnp.dot(vt_scr[e, kb], ps[e], preferred_element_type=F32)

    def blocks(first, count):
        zs = [logits(first + j, False) for j in range(count)]
        for j in range(count):
            update(first + j, zs[j])

    def group(g, carry):
        blocks(g * FOX_GROUP, FOX_GROUP)
        return carry

    lax.fori_loop(0, qi // FOX_GROUP, group, 0)
    done = (qi // FOX_GROUP) * FOX_GROUP
    size = FOX_GROUP // 2
    while size >= 1:
        take = (qi & size) != 0
        pl.when(take)(functools.partial(blocks, done, size))
        done = done + jnp.where(take, size, 0)
        size //= 2
    update(qi, logits(qi, True))
    for e in heads:
        out = (acc_scr[e] / l_scr[e, 0:1, :]).T
        y_ref[:, hcols[e]] = (_rms(out) * nw_ref[:, hcols[e]]).astype(BF16)


def _fox(proj, g3, c_rows, norm_w, *, batch, seq):
    m = proj.shape[0]
    nq = seq // FOX_BLOCK
    width = FOX_PAIR * HEAD_DIM
    q_col = (4 * MLSTM_W) // width
    k_col = q_col + FOX_W // width
    v_col = k_col + FOX_W // width
    return pl.pallas_call(
        _fox_body,
        out_shape=jax.ShapeDtypeStruct((m, FOX_W), BF16),
        grid=(batch, FOX_HEADS // FOX_PAIR, nq),
        in_specs=[
            pl.BlockSpec((FOX_BLOCK, width), lambda b, h, i: (b * nq + i, q_col + h)),
            pl.BlockSpec((seq, width), lambda b, h, i: (b, k_col + h)),
            pl.BlockSpec((seq, width), lambda b, h, i: (b, v_col + h)),
            pl.BlockSpec((seq, LANES), lambda b, h, i: (b, 0)),
            pl.BlockSpec((FOX_PAIR, nq, 1, FOX_BLOCK), lambda b, h, i: (h, b, 0, 0)),
            pl.BlockSpec((1, width), lambda b, h, i: (0, h)),
        ],
        out_specs=pl.BlockSpec((FOX_BLOCK, width), lambda b, h, i: (b * nq + i, h)),
        scratch_shapes=[
            pltpu.VMEM((FOX_PAIR, HEAD_DIM, FOX_BLOCK), F32),
            pltpu.VMEM((FOX_PAIR, SUBLANES, FOX_BLOCK), F32),
            pltpu.VMEM((FOX_PAIR, SUBLANES, FOX_BLOCK), F32),
            pltpu.VMEM((FOX_PAIR, FOX_BLOCK, 2 * HEAD_DIM), BF16),
            pltpu.VMEM((FOX_PAIR, nq, HEAD_DIM, FOX_BLOCK), BF16),
        ],
        compiler_params=_params(("parallel", "parallel", "arbitrary")),
        name="fox",
    )(proj, proj, proj, g3, c_rows, norm_w)


def _mm(x, y):
    return jnp.dot(x.astype(BF16), y.astype(BF16), preferred_element_type=F32)


def _unit_lower_inverses(a_mats, mask_ref):
    eye = mask_ref[MASK_INCL] - mask_ref[MASK_STRICT]
    negs = [-(a * mask_ref[MASK_DIAG8]) for a in a_mats]
    invs = [eye + n for n in negs]
    for _ in range(2):
        negs = [_mm(n, n) for n in negs]
        invs = [i + _mm(i, n) for i, n in zip(invs, negs)]
    for level in range(MASK_OFF0, MASK_COUNT):
        tmps = [_mm(a * mask_ref[level], i) for a, i in zip(a_mats, invs)]
        invs = [i - _mm(i, t) for i, t in zip(invs, tmps)]
    return invs


def _gdn_body(q_ref, k_ref, v_ref, z_ref, cw_ref, gr_ref, gc3_ref, sel_ref, ones_ref, nw_ref, y_ref,
              s_scr, ext_scr, mask_scr):
    step = q_ref.shape[0]
    halo = SUBLANES
    nchunk = step // CHUNK

    @pl.when(pl.program_id(1) == 0)
    def _():
        s_scr[...] = jnp.zeros(s_scr.shape, F32)
        ext_scr[0:halo, :] = jnp.zeros((halo, 3 * GDN_W), F32)
        ri = lax.broadcasted_iota(jnp.int32, (CHUNK, CHUNK), 0)
        ci = lax.broadcasted_iota(jnp.int32, (CHUNK, CHUNK), 1)
        mask_scr[MASK_STRICT] = jnp.where(ri > ci, 1.0, 0.0)
        mask_scr[MASK_INCL] = jnp.where(ri >= ci, 1.0, 0.0)
        mask_scr[MASK_DIAG8] = jnp.where((ri >> 3) == (ci >> 3), 1.0, 0.0)
        for bits in range(3, 3 + MASK_COUNT - MASK_OFF0):
            sibling = ((ri >> (bits + 1)) == (ci >> (bits + 1))) & ((ri >> bits) != (ci >> bits))
            mask_scr[MASK_OFF0 + bits - 3] = jnp.where(sibling, 1.0, 0.0)

    ext_scr[halo:halo + step, 0:GDN_W] = q_ref[...].astype(F32)
    ext_scr[halo:halo + step, GDN_W:2 * GDN_W] = k_ref[...].astype(F32)
    ext_scr[halo:halo + step, 2 * GDN_W:3 * GDN_W] = v_ref[...].astype(F32)
    ext = ext_scr[...]
    conv = ext * cw_ref[0:1, :]
    for j in range(1, CONV_WIDTH):
        conv = pltpu.roll(conv, 1, 0) + ext * cw_ref[j:j + 1, :]
    conv = conv[halo:, :]
    ext_scr[0:halo, :] = ext_scr[step:step + halo, :]
    qkv = conv * _sigmoid(conv)

    def row_sum(t):
        return jnp.dot(t.astype(BF16), ones_ref[...], preferred_element_type=F32)

    def l2n(t):
        return t * lax.rsqrt(row_sum(t * t) + NORM_EPS)

    cols_rep = [jnp.dot(gc3_ref[r * CHUNK:(r + 1) * CHUNK, :], sel_ref[...], preferred_element_type=F32)
                for r in range(nchunk)]

    groups = [(h, r) for h in range(GDN_HEADS) for r in range(nchunk)]
    prep = {}
    for h, r in groups:
        rows = slice(r * CHUNK, (r + 1) * CHUNK)
        q = l2n(qkv[rows, h * HEAD_DIM:(h + 1) * HEAD_DIM]) * QK_SCALE
        k = l2n(qkv[rows, GDN_W + h * HEAD_DIM:GDN_W + (h + 1) * HEAD_DIM])
        v = qkv[rows, 2 * GDN_W + h * HEAD_DIM:2 * GDN_W + (h + 1) * HEAD_DIM]
        g_row = gr_ref[h:h + 1, rows]
        g_col = cols_rep[r][:, (2 * h) * LANES:(2 * h + 1) * LANES]
        beta = cols_rep[r][:, (2 * h + 1) * LANES:(2 * h + 2) * LANES]
        incl = mask_scr[MASK_INCL]
        decay = jnp.exp((g_col - g_row) * incl) * incl
        kb = k * beta
        e_g = jnp.exp(g_col)
        g_last = g_row[:, CHUNK - 1:CHUNK]
        kk_qk = lax.dot_general(jnp.concatenate([kb, q], axis=0).astype(BF16), k.astype(BF16), NT_DIMS,
                                preferred_element_type=F32)
        prep[h, r] = dict(
            a=kk_qk[:CHUNK] * (decay * mask_scr[MASK_STRICT]),
            attn=kk_qk[CHUNK:] * decay,
            rhs=jnp.concatenate([v * beta, kb * e_g], axis=1),
            q_dec=q * e_g,
            k_end=k * jnp.exp(g_last - g_col),
            s_decay=jnp.exp(g_last),
        )
    invs = _unit_lower_inverses([prep[g]["a"] for g in groups], mask_scr)
    for g, inv in zip(groups, invs):
        w = _mm(inv, prep[g]["rhs"])
        prep[g]["w_val"], prep[g]["w_key"] = w[:, :HEAD_DIM], w[:, HEAD_DIM:]

    states = [s_scr[h] for h in range(GDN_HEADS)]
    for r in range(nchunk):
        rows = slice(r * CHUNK, (r + 1) * CHUNK)
        ks = [_mm(jnp.concatenate([prep[h, r]["w_key"], prep[h, r]["q_dec"]], axis=0), states[h])
              for h in range(GDN_HEADS)]
        us = [prep[h, r]["w_val"] - ks[h][:CHUNK] for h in range(GDN_HEADS)]
        outs = [ks[h][CHUNK:] + _mm(prep[h, r]["attn"], us[h]) for h in range(GDN_HEADS)]
        states = [prep[h, r]["s_decay"] * states[h]
                  + lax.dot_general(prep[h, r]["k_end"].astype(BF16), us[h].astype(BF16), TN_DIMS,
                                    preferred_element_type=F32) for h in range(GDN_HEADS)]
        for h in range(GDN_HEADS):
            cols = slice(h * HEAD_DIM, (h + 1) * HEAD_DIM)
            gate = z_ref[rows, cols].astype(F32)
            normed = outs[h] * lax.rsqrt(row_sum(outs[h] * outs[h]) * (1.0 / HEAD_DIM) + NORM_EPS)
            y_ref[rows, cols] = (normed * nw_ref[:, cols] * (gate * _sigmoid(gate))).astype(BF16)
    for h in range(GDN_HEADS):
        s_scr[h] = states[h]


def _gdn(proj, conv_w, gr, gc3, norm_w, *, batch, seq):
    m = proj.shape[0]
    ns = seq // GDN_STEP
    sel = _one_hot_columns([c for h in range(GDN_HEADS) for c in (16 + h, 20 + h)])
    col0 = (4 * MLSTM_W + 3 * FOX_W) // GDN_W
    row = lambda b, s: b * ns + s
    return pl.pallas_call(
        _gdn_body,
        out_shape=jax.ShapeDtypeStruct((m, GDN_W), BF16),
        grid=(batch, ns),
        in_specs=[
            pl.BlockSpec((GDN_STEP, GDN_W), lambda b, s: (row(b, s), col0)),
            pl.BlockSpec((GDN_STEP, GDN_W), lambda b, s: (row(b, s), col0 + 1)),
            pl.BlockSpec((GDN_STEP, GDN_W), lambda b, s: (row(b, s), col0 + 2)),
            pl.BlockSpec((GDN_STEP, GDN_W), lambda b, s: (row(b, s), col0 + 3)),
            pl.BlockSpec((CONV_WIDTH, 3 * GDN_W), lambda b, s: (0, 0)),
            pl.BlockSpec((SUBLANES, GDN_STEP), lambda b, s: (2, row(b, s))),
            pl.BlockSpec((GDN_STEP, LANES), lambda b, s: (row(b, s), 0)),
            pl.BlockSpec(sel.shape, lambda b, s: (0, 0)),
            pl.BlockSpec((HEAD_DIM, HEAD_DIM), lambda b, s: (0, 0)),
            pl.BlockSpec((1, GDN_W), lambda b, s: (0, 0)),
        ],
        out_specs=pl.BlockSpec((GDN_STEP, GDN_W), lambda b, s: (row(b, s), 0)),
        scratch_shapes=[
            pltpu.VMEM((GDN_HEADS, HEAD_DIM, HEAD_DIM), F32),
            pltpu.VMEM((GDN_STEP + SUBLANES, 3 * GDN_W), F32),
            pltpu.VMEM((MASK_COUNT, CHUNK, CHUNK), F32),
        ],
        compiler_params=_params(("parallel", "arbitrary")),
        name="gdn",
    )(proj, proj, proj, proj, conv_w, gr, gc3, sel, jnp.ones((HEAD_DIM, HEAD_DIM), BF16), norm_w)


def _outproj_body(ym_ref, yf_ref, yg_ref, w_ref, x_ref, o_ref):
    acc = jnp.dot(ym_ref[...], w_ref[0:MLSTM_W, :], preferred_element_type=F32)
    acc += jnp.dot(yf_ref[...], w_ref[MLSTM_W:MLSTM_W + FOX_W, :], preferred_element_type=F32)
    acc += jnp.dot(yg_ref[...], w_ref[MLSTM_W + FOX_W:, :], preferred_element_type=F32)
    o_ref[...] = x_ref[...] + acc


def _outproj(y_m, y_f, y_g, w_out, x, *, tm=512):
    m = x.shape[0]
    return pl.pallas_call(
        _outproj_body,
        out_shape=jax.ShapeDtypeStruct((m, D_MODEL), F32),
        grid=(m // tm,),
        in_specs=[
            pl.BlockSpec((tm, MLSTM_W), lambda i: (i, 0)),
            pl.BlockSpec((tm, FOX_W), lambda i: (i, 0)),
            pl.BlockSpec((tm, GDN_W), lambda i: (i, 0)),
            pl.BlockSpec((D_MODEL, D_MODEL), lambda i: (0, 0)),
            pl.BlockSpec((tm, D_MODEL), lambda i: (i, 0)),
        ],
        out_specs=pl.BlockSpec((tm, D_MODEL), lambda i: (i, 0)),
        compiler_params=_params(("parallel",)),
        name="outproj",
    )(y_m, y_f, y_g, w_out, x)


def _ffn_body(x_ref, nw_ref, wg_ref, wu_ref, wd_ref, fw_ref, o_ref, h_scr, acc_scr, *, final_norm):
    j = pl.program_id(1)

    @pl.when(j == 0)
    def _():
        h_scr[...] = (_rms(x_ref[...]) * nw_ref[...]).astype(BF16)
        acc_scr[...] = jnp.zeros(acc_scr.shape, F32)

    h = h_scr[...]
    gate = jnp.dot(h, wg_ref[...], preferred_element_type=F32)
    up = jnp.dot(h, wu_ref[...], preferred_element_type=F32)
    act = (gate * _sigmoid(gate) * up).astype(BF16)
    acc_scr[...] += jnp.dot(act, wd_ref[...], preferred_element_type=F32)

    @pl.when(j == pl.num_programs(1) - 1)
    def _():
        out = x_ref[...] + acc_scr[...]
        if final_norm:
            out = _rms(out) * fw_ref[...]
        o_ref[...] = out


def _ffn(x, norm_w, w_gate, w_up, w_down, final_w, *, final_norm, tm=512, tf=512):
    m = x.shape[0]
    return pl.pallas_call(
        functools.partial(_ffn_body, final_norm=final_norm),
        out_shape=jax.ShapeDtypeStruct((m, D_MODEL), F32),
        grid=(m // tm, D_FF // tf),
        in_specs=[
            pl.BlockSpec((tm, D_MODEL), lambda i, j: (i, 0)),
            pl.BlockSpec((1, D_MODEL), lambda i, j: (0, 0)),
            pl.BlockSpec((D_MODEL, tf), lambda i, j: (0, j)),
            pl.BlockSpec((D_MODEL, tf), lambda i, j: (0, j)),
            pl.BlockSpec((tf, D_MODEL), lambda i, j: (j, 0)),
            pl.BlockSpec((1, D_MODEL), lambda i, j: (0, 0)),
        ],
        out_specs=pl.BlockSpec((tm, D_MODEL), lambda i, j: (i, 0)),
        scratch_shapes=[pltpu.VMEM((tm, D_MODEL), BF16), pltpu.VMEM((tm, D_MODEL), F32)],
        compiler_params=_params(("parallel", "arbitrary")),
        name="ffn",
    )(x, norm_w, w_gate, w_up, w_down, final_w)


def _prep_w_in_body(w_ref, wide_ref, gate_ref):
    o_mi = 4 * MLSTM_W
    o_fq = o_mi + 2 * MLSTM_HEADS
    o_ff = o_fq + 3 * FOX_W
    o_gq = o_ff + FOX_HEADS
    o_ga = o_gq + 4 * GDN_W
    wide_ref[0:o_mi, :] = w_ref[0:o_mi, :].astype(BF16)
    wide_ref[o_mi:o_mi + 3 * FOX_W, :] = w_ref[o_fq:o_ff, :].astype(BF16)
    wide_ref[o_mi + 3 * FOX_W:, :] = w_ref[o_gq:o_ga, :].astype(BF16)
    gate = jnp.concatenate([w_ref[o_mi:o_fq, :], w_ref[o_ff:o_gq, :], w_ref[o_ga:, :],
                            jnp.zeros((SUBLANES, w_ref.shape[1]), F32)], axis=0)
    gate_ref[...] = gate.astype(BF16)


def _prep_w_in(w_in_t, layer, *, cols=256):
    _, n_in, d = w_in_t.shape
    return pl.pallas_call(
        _prep_w_in_body,
        out_shape=(jax.ShapeDtypeStruct((PROJ_W, d), BF16), jax.ShapeDtypeStruct((GATE_ROWS, d), BF16)),
        grid=(d // cols,),
        in_specs=[pl.BlockSpec((None, n_in, cols), lambda i: (layer, 0, i))],
        out_specs=(pl.BlockSpec((PROJ_W, cols), lambda i: (0, i)), pl.BlockSpec((GATE_ROWS, cols), lambda i: (0, i))),
        compiler_params=_params(("parallel",)),
        name="prep_w_in",
    )(w_in_t)


def kernel(x, mix_norm_w, w_in, mlstm_i_bias, mlstm_f_bias, fox_f_bias, gdn_conv_w, gdn_a_log, gdn_dt_bias,
           mlstm_out_norm_w, fox_out_norm_w, gdn_out_norm_w, w_out, ffn_norm_w, w_gate, w_up, w_down,
           final_norm_w):
    batch, seq, d_model = x.shape
    depth = w_in.shape[0]
    assert d_model == D_MODEL and seq % FOX_BLOCK == 0 and seq % GDN_STEP == 0
    xf = x.reshape(batch * seq, d_model)
    zeros4 = jnp.zeros((GDN_HEADS,), F32)
    w_in_t = jnp.swapaxes(w_in, 1, 2)
    side_weights = [(w_gate, 64, 0), (w_up, 64, 24), (w_down, 128, 0), (w_out, 128, 40)]
    for l in range(depth):
        w_wide, w_gate_t = _prep_w_in(w_in_t, l)
        bias_col = jnp.pad(jnp.concatenate([mlstm_i_bias[l], mlstm_f_bias[l], fox_f_bias[l], gdn_dt_bias[l],
                                            zeros4]), (0, 8)).reshape(GATE_ROWS, 1)
        alog_col = jnp.pad(gdn_a_log[l], (16, 12)).reshape(GATE_ROWS, 1)

        proj, gt, (w_gate_b, w_up_b, w_down_b, w_out_b) = _inproj(
            xf, mix_norm_w[l].reshape(1, -1), w_wide, w_gate_t, side_weights, l)
        gr, gr3 = _gates(gt, bias_col, alog_col, batch=batch, seq=seq)
        gc3 = gr3.T
        c_rows = gr[8:16].reshape(FOX_HEADS, batch * (seq // FOX_BLOCK), 1, FOX_BLOCK)

        y_m = _mlstm(proj, gr, gc3, mlstm_out_norm_w[l].reshape(1, -1), batch=batch, seq=seq)
        y_f = _fox(proj, gc3, c_rows, fox_out_norm_w[l].reshape(1, -1), batch=batch, seq=seq)
        y_g = _gdn(proj, gdn_conv_w[l], gr, gc3, gdn_out_norm_w[l].reshape(1, -1), batch=batch, seq=seq)
        xf = _outproj(y_m, y_f, y_g, w_out_b, xf)
        xf = _ffn(xf, ffn_norm_w[l].reshape(1, -1), w_gate_b, w_up_b, w_down_b, final_norm_w.reshape(1, -1),
                  final_norm=(l == depth - 1))
    return xf.reshape(batch, seq, d_model)
```

```python
import functools

import jax
import jax.numpy as jnp
import numpy as np
from jax import lax
from jax.experimental import pallas as pl
from jax.experimental.pallas import tpu as pltpu

F32 = jnp.float32
BF16 = jnp.bfloat16

D_MODEL = 2048
HEAD_DIM = 128
MLSTM_HEADS = 4
FOX_HEADS = 8
GDN_HEADS = 4
MLSTM_W = MLSTM_HEADS * HEAD_DIM
FOX_W = FOX_HEADS * HEAD_DIM
GDN_W = GDN_HEADS * HEAD_DIM
D_FF = 5632
CONV_WIDTH = 4
GATE_SOFTCAP = 15.0
NORM_EPS = 1e-6
QK_SCALE = HEAD_DIM ** -0.5
LOG2E = 1.4426950408889634

PROJ_W = 4 * MLSTM_W + 3 * FOX_W + 4 * GDN_W
GATE_ROWS = 32

LANES = 128
SUBLANES = 8
V7X_VMEM_BYTES = 64 * 1024 * 1024
VMEM_LIMIT = 56 * 1024 * 1024

CHUNK = 128
FOX_BLOCK = 512
FOX_GROUP = 4
FOX_PAIR = 2
GDN_STEP = 512
MLSTM_STEP = 512

MASK_STRICT, MASK_INCL, MASK_DIAG8, MASK_OFF0 = 0, 1, 2, 3
MASK_COUNT = MASK_OFF0 + 4

NT_DIMS = (((1,), (1,)), ((), ()))
TN_DIMS = (((0,), (0,)), ((), ()))


def _params(sem):
    return pltpu.CompilerParams(dimension_semantics=sem, vmem_limit_bytes=VMEM_LIMIT)


def _sigmoid(z):
    return 1.0 / (1.0 + jnp.exp(-z))


def _log_sigmoid(z):
    return jnp.minimum(z, 0.0) - jnp.log1p(jnp.exp(-jnp.abs(z)))


def _softplus(z):
    return jnp.maximum(z, 0.0) + jnp.log1p(jnp.exp(-jnp.abs(z)))


def _rms(t):
    return t * lax.rsqrt(jnp.mean(t * t, axis=-1, keepdims=True) + NORM_EPS)


def _inproj_body(*refs, row_chunk, n_side):
    x_ref, nw_ref, w_ref, wg_ref = refs[:4]
    side_in = refs[4:4 + n_side]
    proj_ref, gt_ref = refs[4 + n_side:6 + n_side]
    side_out = refs[6 + n_side:6 + 2 * n_side]
    h_scr = refs[6 + 2 * n_side]

    @pl.when(pl.program_id(1) == 0)
    def _():
        for r in range(x_ref.shape[0] // row_chunk):
            rows = slice(r * row_chunk, (r + 1) * row_chunk)
            h_scr[rows, :] = (_rms(x_ref[rows, :]) * nw_ref[...]).astype(BF16)
        gt_ref[...] = lax.dot_general(wg_ref[...], h_scr[...], NT_DIMS, preferred_element_type=F32)

    for src, dst in zip(side_in, side_out):
        dst[...] = src[...].astype(BF16)

    proj_ref[...] = lax.dot_general(h_scr[...], w_ref[...], NT_DIMS, preferred_element_type=F32).astype(BF16)


def _inproj(x, norm_w, w_big, w_gate_t, side_weights, layer, *, tm=1024, tn=1024):
    m = x.shape[0]
    ni, nj = m // tm, PROJ_W // tn
    side_in_specs, side_out_specs, side_out_shapes = [], [], []
    for w, band, first in side_weights:
        _, rows, cols = w.shape
        count = rows // band
        assert rows % band == 0 and first + count <= ni * nj

        def band_index(i, j, first=first, count=count):
            return jnp.clip(i * nj + j - first, 0, count - 1)

        side_in_specs.append(pl.BlockSpec((None, band, cols), lambda i, j, f=band_index: (layer, f(i, j), 0)))
        side_out_specs.append(pl.BlockSpec((band, cols), lambda i, j, f=band_index: (f(i, j), 0)))
        side_out_shapes.append(jax.ShapeDtypeStruct((rows, cols), BF16))
    outs = pl.pallas_call(
        functools.partial(_inproj_body, row_chunk=256, n_side=len(side_weights)),
        out_shape=(jax.ShapeDtypeStruct((m, PROJ_W), BF16), jax.ShapeDtypeStruct((GATE_ROWS, m), F32),
                   *side_out_shapes),
        grid=(ni, nj),
        in_specs=[
            pl.BlockSpec((tm, D_MODEL), lambda i, j: (i, 0)),
            pl.BlockSpec((1, D_MODEL), lambda i, j: (0, 0)),
            pl.BlockSpec((tn, D_MODEL), lambda i, j: (j, 0)),
            pl.BlockSpec((GATE_ROWS, D_MODEL), lambda i, j: (0, 0)),
            *side_in_specs,
        ],
        out_specs=(
            pl.BlockSpec((tm, tn), lambda i, j: (i, j)),
            pl.BlockSpec((GATE_ROWS, tm), lambda i, j: (0, i)),
            *side_out_specs,
        ),
        scratch_shapes=[pltpu.VMEM((tm, D_MODEL), BF16)],
        compiler_params=_params(("arbitrary", "arbitrary")),
        name="inproj",
    )(x, norm_w, w_big, w_gate_t, *[w for w, _, _ in side_weights])
    return outs[0], outs[1], outs[2:]


def _lane_cumsum(v, seg):
    pos = lax.broadcasted_iota(jnp.int32, v.shape, 1) & (seg - 1)
    shift = 1
    while shift < seg:
        v = v + jnp.where(pos >= shift, pltpu.roll(v, shift, 1), 0.0)
        shift *= 2
    return v


def _lane_cummax(v, seg):
    pos = lax.broadcasted_iota(jnp.int32, v.shape, 1) & (seg - 1)
    shift = 1
    while shift < seg:
        v = jnp.maximum(v, jnp.where(pos >= shift, pltpu.roll(v, shift, 1), -jnp.inf))
        shift *= 2
    return v


def _gates_body(gt_ref, bias_ref, alog_ref, out_ref, split_ref):
    seq = gt_ref.shape[1]
    first4 = lax.broadcasted_iota(jnp.int32, (SUBLANES, seq), 0) < 4
    z = gt_ref[0:8, :] + bias_ref[0:8, :]
    z = GATE_SOFTCAP * jnp.tanh(z / GATE_SOFTCAP)
    b_cum = _lane_cumsum(_log_sigmoid(z), CHUNK)
    out_ref[0:8, :] = jnp.where(first4, z, b_cum)
    d = z - pltpu.roll(b_cum, 4, 0)
    out_ref[24:32, :] = jnp.where(first4, d, pltpu.roll(_lane_cummax(d, CHUNK), 4, 0))

    raw = gt_ref[16:24, :]
    decay = _lane_cumsum(-jnp.exp(alog_ref[16:24, :]) * _softplus(raw + bias_ref[16:24, :]), CHUNK)
    out_ref[16:24, :] = jnp.where(first4, decay, _sigmoid(raw))

    fox = _lane_cumsum(LOG2E * _log_sigmoid(gt_ref[8:16, :] + bias_ref[8:16, :]), LANES)
    carry = jnp.zeros((SUBLANES, 1), F32)
    for blk in range(seq // LANES):
        lanes = slice(blk * LANES, (blk + 1) * LANES)
        tile = fox[:, lanes] + carry
        out_ref[8:16, lanes] = tile
        carry = tile[:, LANES - 1:LANES]

    value = out_ref[...]
    hi = value.astype(BF16)
    rest = value - hi.astype(F32)
    mid = rest.astype(BF16)
    split_ref[0:GATE_ROWS, :] = hi
    split_ref[GATE_ROWS:2 * GATE_ROWS, :] = mid
    split_ref[2 * GATE_ROWS:3 * GATE_ROWS, :] = (rest - mid.astype(F32)).astype(BF16)
    split_ref[3 * GATE_ROWS:, :] = jnp.zeros((LANES - 3 * GATE_ROWS, seq), BF16)


def _gates(gt, bias_col, alog_col, *, batch, seq):
    return pl.pallas_call(
        _gates_body,
        out_shape=(jax.ShapeDtypeStruct(gt.shape, F32), jax.ShapeDtypeStruct((LANES, gt.shape[1]), BF16)),
        grid=(batch,),
        in_specs=[
            pl.BlockSpec((GATE_ROWS, seq), lambda b: (0, b)),
            pl.BlockSpec((GATE_ROWS, 1), lambda b: (0, 0)),
            pl.BlockSpec((GATE_ROWS, 1), lambda b: (0, 0)),
        ],
        out_specs=(pl.BlockSpec((GATE_ROWS, seq), lambda b: (0, b)), pl.BlockSpec((LANES, seq), lambda b: (0, b))),
        compiler_params=_params(("parallel",)),
        name="gates",
    )(gt, bias_col, alog_col)


def _mlstm_body(q_ref, k_ref, v_ref, o_ref, gb_ref, gd_ref, gc3_ref, sel_ref, mean_ref, nw_ref, y_ref,
                ct_scr, m_scr):
    @pl.when(pl.program_id(1) == 0)
    def _():
        ct_scr[...] = jnp.zeros(ct_scr.shape, F32)
        m_scr[...] = jnp.zeros(m_scr.shape, F32)

    heads = range(MLSTM_HEADS)
    causal = (lax.broadcasted_iota(jnp.int32, (CHUNK, CHUNK), 0)
              >= lax.broadcasted_iota(jnp.int32, (CHUNK, CHUNK), 1))
    ones = jnp.ones((CHUNK, HEAD_DIM), BF16)
    cts = [ct_scr[h] for h in heads]
    ms = [m_scr[h:h + 1, 0:1] for h in heads]
    for r in range(q_ref.shape[0] // CHUNK):
        rows = slice(r * CHUNK, (r + 1) * CHUNK)
        hcols = [slice(h * HEAD_DIM, (h + 1) * HEAD_DIM) for h in heads]
        q = [q_ref[rows, c] for c in hcols]
        k = [k_ref[rows, c] for c in hcols]
        v_aug = [jnp.concatenate([v_ref[rows, c], ones], axis=1) for c in hcols]
        cols_rep = jnp.dot(gc3_ref[rows, :], sel_ref[...], preferred_element_type=F32)
        rep = lambda h, j: cols_rep[:, (3 * h + j) * LANES:(3 * h + j + 1) * LANES]
        a_col = [rep(h, 0) for h in heads]
        d_col = [rep(h, 1) for h in heads]
        b_col = [rep(h, 2) for h in heads]
        d_row = [gd_ref[h:h + 1, rows] for h in heads]
        b_last = [gb_ref[4 + h:5 + h, rows][:, CHUNK - 1:CHUNK] for h in heads]
        a_last = [gd_ref[4 + h:5 + h, rows][:, CHUNK - 1:CHUNK] for h in heads]

        qk = [lax.dot_general(q[h], k[h], NT_DIMS, preferred_element_type=F32) for h in heads]
        q_state = [lax.dot_general(q[h], cts[h].astype(BF16), NT_DIMS, preferred_element_type=F32)
                   for h in heads]
        m_rel = [jnp.maximum(ms[h], a_col[h]) for h in heads]
        p = [jnp.exp(jnp.where(causal, d_row[h] - m_rel[h], -jnp.inf)) * (qk[h] * QK_SCALE) for h in heads]
        intra = [jnp.dot(p[h].astype(BF16), v_aug[h], preferred_element_type=F32) for h in heads]
        for h in heads:
            w_inter = jnp.exp(ms[h] - m_rel[h])
            both = jnp.concatenate([w_inter, w_inter], axis=1) * q_state[h] + intra[h]
            floor = jnp.exp(-(b_col[h] + m_rel[h]))
            h_out = both[:, :HEAD_DIM] / jnp.maximum(jnp.abs(both[:, HEAD_DIM:]), floor)
            mean = jnp.dot(h_out.astype(BF16), mean_ref[...], preferred_element_type=F32)
            centered = h_out - mean
            var = jnp.dot((centered * centered).astype(BF16), mean_ref[...], preferred_element_type=F32)
            gate = _sigmoid(o_ref[rows, hcols[h]].astype(F32))
            y_ref[rows, hcols[h]] = (centered * lax.rsqrt(var + NORM_EPS) * nw_ref[:, hcols[h]]
                                     * gate).astype(BF16)

        m_end = [jnp.maximum(ms[h], a_last[h]) for h in heads]
        kw = [(k[h].astype(F32) * (jnp.exp(d_col[h] - m_end[h]) * QK_SCALE)).astype(BF16) for h in heads]
        cts = [jnp.exp(ms[h] - m_end[h]) * cts[h]
               + lax.dot_general(v_aug[h], kw[h], TN_DIMS, preferred_element_type=F32) for h in heads]
        ms = [b_last[h] + m_end[h] for h in heads]
    for h in heads:
        ct_scr[h] = cts[h]
        m_scr[h:h + 1, :] = jnp.broadcast_to(ms[h], (1, LANES))


def _one_hot_columns(gate_cols):
    sel = np.zeros((LANES, LANES * len(gate_cols)), np.float32)
    for j, col in enumerate(gate_cols):
        for part in range(3):
            sel[part * GATE_ROWS + col, j * LANES:(j + 1) * LANES] = 1.0
    return jnp.asarray(sel, BF16)


def _mlstm(proj, gr, gc3, norm_w, *, batch, seq):
    m = proj.shape[0]
    nc = seq // MLSTM_STEP
    row = lambda b, c: b * nc + c
    sel = _one_hot_columns([c for h in range(MLSTM_HEADS) for c in (28 + h, 24 + h, 4 + h)])
    mean_w = jnp.full((HEAD_DIM, HEAD_DIM), 1.0 / HEAD_DIM, BF16)
    return pl.pallas_call(
        _mlstm_body,
        out_shape=jax.ShapeDtypeStruct((m, MLSTM_W), BF16),
        grid=(batch, nc),
        in_specs=[
            pl.BlockSpec((MLSTM_STEP, MLSTM_W), lambda b, c: (row(b, c), 0)),
            pl.BlockSpec((MLSTM_STEP, MLSTM_W), lambda b, c: (row(b, c), 1)),
            pl.BlockSpec((MLSTM_STEP, MLSTM_W), lambda b, c: (row(b, c), 2)),
            pl.BlockSpec((MLSTM_STEP, MLSTM_W), lambda b, c: (row(b, c), 3)),
            pl.BlockSpec((SUBLANES, MLSTM_STEP), lambda b, c: (0, row(b, c))),
            pl.BlockSpec((SUBLANES, MLSTM_STEP), lambda b, c: (3, row(b, c))),
            pl.BlockSpec((MLSTM_STEP, LANES), lambda b, c: (row(b, c), 0)),
            pl.BlockSpec(sel.shape, lambda b, c: (0, 0)),
            pl.BlockSpec((HEAD_DIM, HEAD_DIM), lambda b, c: (0, 0)),
            pl.BlockSpec((1, MLSTM_W), lambda b, c: (0, 0)),
        ],
        out_specs=pl.BlockSpec((MLSTM_STEP, MLSTM_W), lambda b, c: (row(b, c), 0)),
        scratch_shapes=[
            pltpu.VMEM((MLSTM_HEADS, 2 * HEAD_DIM, HEAD_DIM), F32),
            pltpu.VMEM((SUBLANES, LANES), F32),
        ],
        compiler_params=_params(("parallel", "arbitrary")),
        name="mlstm",
    )(proj, proj, proj, proj, gr, gr, gc3, sel, mean_w, norm_w)


def _fox_body(q_ref, k_ref, v_ref, g3_ref, c_ref, nw_ref, y_ref, acc_scr, m_scr, l_scr, qa_scr, vt_scr):
    pair = pl.program_id(1)
    qi = pl.program_id(2)
    tq = q_ref.shape[0]
    heads = range(FOX_PAIR)
    hcols = [slice(e * HEAD_DIM, (e + 1) * HEAD_DIM) for e in heads]

    @pl.when(qi == 0)
    def _():
        for e in heads:
            for j in range(vt_scr.shape[1]):
                vt_scr[e, j] = v_ref[j * tq:(j + 1) * tq, hcols[e]].T

    lane = lax.broadcasted_iota(jnp.int32, (tq, LANES), 1)
    for e in heads:
        fox_row = 2 * MLSTM_HEADS + FOX_PAIR * pair + e
        split_lane = (lane == fox_row) | (lane == GATE_ROWS + fox_row) | (lane == 2 * GATE_ROWS + fox_row)
        qa_scr[e, :, 0:HEAD_DIM] = (q_ref[:, hcols[e]].astype(F32) * (QK_SCALE * LOG2E)).astype(BF16)
        qa_scr[e, :, HEAD_DIM:] = jnp.where(split_lane, -1.0, 0.0).astype(BF16)
    c_q = [c_ref[e, qi] for e in heads]
    m_scr[...] = jnp.full(m_scr.shape, -jnp.inf, F32)
    l_scr[...] = jnp.zeros(l_scr.shape, F32)
    acc_scr[...] = jnp.zeros(acc_scr.shape, F32)

    def logits(kb, diagonal):
        base = pl.multiple_of(kb * tq, tq)
        gates = g3_ref[pl.ds(base, tq), :]
        zs = []
        for e in heads:
            keys = jnp.concatenate([k_ref[pl.ds(base, tq), hcols[e]], gates], axis=1)
            z = lax.dot_general(keys, qa_scr[e], NT_DIMS, preferred_element_type=F32)
            if diagonal:
                visible = (lax.broadcasted_iota(jnp.int32, (tq, tq), 0)
                           <= lax.broadcasted_iota(jnp.int32, (tq, tq), 1))
                z = jnp.where(visible, z, -jnp.inf)
            zs.append(z)
        return zs

    def update(kb, zs):
        ps, alphas = [], []
        for e in heads:
            m_prev = m_scr[e, 0:1, :]
            m_new = jnp.maximum(m_prev, jnp.max(zs[e], axis=0, keepdims=True) + c_q[e])
            alpha = jnp.exp2(m_prev - m_new)
            p = jnp.exp2(zs[e] - (m_new - c_q[e]))
            l_scr[e, 0:1, :] = alpha * l_scr[e, 0:1, :] + jnp.sum(p, axis=0, keepdims=True)
            m_scr[e, 0:1, :] = m_new
            ps.append(p.astype(BF16))
            alphas.append(alpha)
        for e in heads:
            acc_scr[e] = alphas[e] * acc_scr[e] + jnp.dot(vt_scr[e, kb], ps[e], preferred_element_type=F32)

    def blocks(first, count):
        zs = [logits(first + j, False) for j in range(count)]
        for j in range(count):
            update(first + j, zs[j])

    def group(g, carry):
        blocks(g * FOX_GROUP, FOX_GROUP)
        return carry

    lax.fori_loop(0, qi // FOX_GROUP, group, 0)
    done = (qi // FOX_GROUP) * FOX_GROUP
    size = FOX_GROUP // 2
    while size >= 1:
        take = (qi & size) != 0
        pl.when(take)(functools.partial(blocks, done, size))
        done = done + jnp.where(take, size, 0)
        size //= 2
    update(qi, logits(qi, True))
    for e in heads:
        out = (acc_scr[e] / l_scr[e, 0:1, :]).T
        y_ref[:, hcols[e]] = (_rms(out) * nw_ref[:, hcols[e]]).astype(BF16)


def _fox(proj, g3, c_rows, norm_w, *, batch, seq):
    m = proj.shape[0]
    nq = seq // FOX_BLOCK
    width = FOX_PAIR * HEAD_DIM
    q_col = (4 * MLSTM_W) // width
    k_col = q_col + FOX_W // width
    v_col = k_col + FOX_W // width
    return pl.pallas_call(
        _fox_body,
        out_shape=jax.ShapeDtypeStruct((m, FOX_W), BF16),
        grid=(batch, FOX_HEADS // FOX_PAIR, nq),
        in_specs=[
            pl.BlockSpec((FOX_BLOCK, width), lambda b, h, i: (b * nq + i, q_col + h)),
            pl.BlockSpec((seq, width), lambda b, h, i: (b, k_col + h)),
            pl.BlockSpec((seq, width), lambda b, h, i: (b, v_col + h)),
            pl.BlockSpec((seq, LANES), lambda b, h, i: (b, 0)),
            pl.BlockSpec((FOX_PAIR, nq, 1, FOX_BLOCK), lambda b, h, i: (h, b, 0, 0)),
            pl.BlockSpec((1, width), lambda b, h, i: (0, h)),
        ],
        out_specs=pl.BlockSpec((FOX_BLOCK, width), lambda b, h, i: (b * nq + i, h)),
        scratch_shapes=[
            pltpu.VMEM((FOX_PAIR, HEAD_DIM, FOX_BLOCK), F32),
            pltpu.VMEM((FOX_PAIR, SUBLANES, FOX_BLOCK), F32),
            pltpu.VMEM((FOX_PAIR, SUBLANES, FOX_BLOCK), F32),
            pltpu.VMEM((FOX_PAIR, FOX_BLOCK, 2 * HEAD_DIM), BF16),
            pltpu.VMEM((FOX_PAIR, nq, HEAD_DIM, FOX_BLOCK), BF16),
        ],
        compiler_params=_params(("parallel", "parallel", "arbitrary")),
        name="fox",
    )(proj, proj, proj, g3, c_rows, norm_w)


def _mm(x, y):
    return jnp.dot(x.astype(BF16), y.astype(BF16), preferred_element_type=F32)


def _unit_lower_inverses(a_mats, mask_ref):
    eye = mask_ref[MASK_INCL] - mask_ref[MASK_STRICT]
    negs = [-(a * mask_ref[MASK_DIAG8]) for a in a_mats]
    invs = [eye + n for n in negs]
    for _ in range(2):
        negs = [_mm(n, n) for n in negs]
        invs = [i + _mm(i, n) for i, n in zip(invs, negs)]
    for level in range(MASK_OFF0, MASK_COUNT):
        tmps = [_mm(a * mask_ref[level], i) for a, i in zip(a_mats, invs)]
        invs = [i - _mm(i, t) for i, t in zip(invs, tmps)]
    return invs


def _gdn_body(q_ref, k_ref, v_ref, z_ref, cw_ref, gr_ref, gc3_ref, sel_ref, ones_ref, nw_ref, y_ref,
              s_scr, ext_scr, mask_scr):
    step = q_ref.shape[0]
    halo = SUBLANES
    nchunk = step // CHUNK

    @pl.when(pl.program_id(1) == 0)
    def _():
        s_scr[...] = jnp.zeros(s_scr.shape, F32)
        ext_scr[0:halo, :] = jnp.zeros((halo, 3 * GDN_W), F32)
        ri = lax.broadcasted_iota(jnp.int32, (CHUNK, CHUNK), 0)
        ci = lax.broadcasted_iota(jnp.int32, (CHUNK, CHUNK), 1)
        mask_scr[MASK_STRICT] = jnp.where(ri > ci, 1.0, 0.0)
        mask_scr[MASK_INCL] = jnp.where(ri >= ci, 1.0, 0.0)
        mask_scr[MASK_DIAG8] = jnp.where((ri >> 3) == (ci >> 3), 1.0, 0.0)
        for bits in range(3, 3 + MASK_COUNT - MASK_OFF0):
            sibling = ((ri >> (bits + 1)) == (ci >> (bits + 1))) & ((ri >> bits) != (ci >> bits))
            mask_scr[MASK_OFF0 + bits - 3] = jnp.where(sibling, 1.0, 0.0)

    ext_scr[halo:halo + step, 0:GDN_W] = q_ref[...].astype(F32)
    ext_scr[halo:halo + step, GDN_W:2 * GDN_W] = k_ref[...].astype(F32)
    ext_scr[halo:halo + step, 2 * GDN_W:3 * GDN_W] = v_ref[...].astype(F32)
    ext = ext_scr[...]
    conv = ext * cw_ref[0:1, :]
    for j in range(1, CONV_WIDTH):
        conv = pltpu.roll(conv, 1, 0) + ext * cw_ref[j:j + 1, :]
    conv = conv[halo:, :]
    ext_scr[0:halo, :] = ext_scr[step:step + halo, :]
    qkv = conv * _sigmoid(conv)

    def row_sum(t):
        return jnp.dot(t.astype(BF16), ones_ref[...], preferred_element_type=F32)

    def l2n(t):
        return t * lax.rsqrt(row_sum(t * t) + NORM_EPS)

    cols_rep = [jnp.dot(gc3_ref[r * CHUNK:(r + 1) * CHUNK, :], sel_ref[...], preferred_element_type=F32)
                for r in range(nchunk)]

    groups = [(h, r) for h in range(GDN_HEADS) for r in range(nchunk)]
    prep = {}
    for h, r in groups:
        rows = slice(r * CHUNK, (r + 1) * CHUNK)
        q = l2n(qkv[rows, h * HEAD_DIM:(h + 1) * HEAD_DIM]) * QK_SCALE
        k = l2n(qkv[rows, GDN_W + h * HEAD_DIM:GDN_W + (h + 1) * HEAD_DIM])
        v = qkv[rows, 2 * GDN_W + h * HEAD_DIM:2 * GDN_W + (h + 1) * HEAD_DIM]
        g_row = gr_ref[h:h + 1, rows]
        g_col = cols_rep[r][:, (2 * h) * LANES:(2 * h + 1) * LANES]
        beta = cols_rep[r][:, (2 * h + 1) * LANES:(2 * h + 2) * LANES]
        incl = mask_scr[MASK_INCL]
        decay = jnp.exp((g_col - g_row) * incl) * incl
        kb = k * beta
        e_g = jnp.exp(g_col)
        g_last = g_row[:, CHUNK - 1:CHUNK]
        kk_qk = lax.dot_general(jnp.concatenate([kb, q], axis=0).astype(BF16), k.astype(BF16), NT_DIMS,
                                preferred_element_type=F32)
        prep[h, r] = dict(
            a=kk_qk[:CHUNK] * (decay * mask_scr[MASK_STRICT]),
            attn=kk_qk[CHUNK:] * decay,
            rhs=jnp.concatenate([v * beta, kb * e_g], axis=1),
            q_dec=q * e_g,
            k_end=k * jnp.exp(g_last - g_col),
            s_decay=jnp.exp(g_last),
        )
    invs = _unit_lower_inverses([prep[g]["a"] for g in groups], mask_scr)
    for g, inv in zip(groups, invs):
        w = _mm(inv, prep[g]["rhs"])
        prep[g]["w_val"], prep[g]["w_key"] = w[:, :HEAD_DIM], w[:, HEAD_DIM:]

    states = [s_scr[h] for h in range(GDN_HEADS)]
    for r in range(nchunk):
        rows = slice(r * CHUNK, (r + 1) * CHUNK)
        ks = [_mm(jnp.concatenate([prep[h, r]["w_key"], prep[h, r]["q_dec"]], axis=0), states[h])
              for h in range(GDN_HEADS)]
        us = [prep[h, r]["w_val"] - ks[h][:CHUNK] for h in range(GDN_HEADS)]
        outs = [ks[h][CHUNK:] + _mm(prep[h, r]["attn"], us[h]) for h in range(GDN_HEADS)]
        states = [prep[h, r]["s_decay"] * states[h]
                  + lax.dot_general(prep[h, r]["k_end"].astype(BF16), us[h].astype(BF16), TN_DIMS,
                                    preferred_element_type=F32) for h in range(GDN_HEADS)]
        for h in range(GDN_HEADS):
            cols = slice(h * HEAD_DIM, (h + 1) * HEAD_DIM)
            gate = z_ref[rows, cols].astype(F32)
            normed = outs[h] * lax.rsqrt(row_sum(outs[h] * outs[h]) * (1.0 / HEAD_DIM) + NORM_EPS)
            y_ref[rows, cols] = (normed * nw_ref[:, cols] * (gate * _sigmoid(gate))).astype(BF16)
    for h in range(GDN_HEADS):
        s_scr[h] = states[h]


def _gdn(proj, conv_w, gr, gc3, norm_w, *, batch, seq):
    m = proj.shape[0]
    ns = seq // GDN_STEP
    sel = _one_hot_columns([c for h in range(GDN_HEADS) for c in (16 + h, 20 + h)])
    col0 = (4 * MLSTM_W + 3 * FOX_W) // GDN_W
    row = lambda b, s: b * ns + s
    return pl.pallas_call(
        _gdn_body,
        out_shape=jax.ShapeDtypeStruct((m, GDN_W), BF16),
        grid=(batch, ns),
        in_specs=[
            pl.BlockSpec((GDN_STEP, GDN_W), lambda b, s: (row(b, s), col0)),
            pl.BlockSpec((GDN_STEP, GDN_W), lambda b, s: (row(b, s), col0 + 1)),
            pl.BlockSpec((GDN_STEP, GDN_W), lambda b, s: (row(b, s), col0 + 2)),
            pl.BlockSpec((GDN_STEP, GDN_W), lambda b, s: (row(b, s), col0 + 3)),
            pl.BlockSpec((CONV_WIDTH, 3 * GDN_W), lambda b, s: (0, 0)),
            pl.BlockSpec((SUBLANES, GDN_STEP), lambda b, s: (2, row(b, s))),
            pl.BlockSpec((GDN_STEP, LANES), lambda b, s: (row(b, s), 0)),
            pl.BlockSpec(sel.shape, lambda b, s: (0, 0)),
            pl.BlockSpec((HEAD_DIM, HEAD_DIM), lambda b, s: (0, 0)),
            pl.BlockSpec((1, GDN_W), lambda b, s: (0, 0)),
        ],
        out_specs=pl.BlockSpec((GDN_STEP, GDN_W), lambda b, s: (row(b, s), 0)),
        scratch_shapes=[
            pltpu.VMEM((GDN_HEADS, HEAD_DIM, HEAD_DIM), F32),
            pltpu.VMEM((GDN_STEP + SUBLANES, 3 * GDN_W), F32),
            pltpu.VMEM((MASK_COUNT, CHUNK, CHUNK), F32),
        ],
        compiler_params=_params(("parallel", "arbitrary")),
        name="gdn",
    )(proj, proj, proj, proj, conv_w, gr, gc3, sel, jnp.ones((HEAD_DIM, HEAD_DIM), BF16), norm_w)


def _outproj_body(ym_ref, yf_ref, yg_ref, w_ref, x_ref, o_ref):
    acc = jnp.dot(ym_ref[...], w_ref[0:MLSTM_W, :], preferred_element_type=F32)
    acc += jnp.dot(yf_ref[...], w_ref[MLSTM_W:MLSTM_W + FOX_W, :], preferred_element_type=F32)
    acc += jnp.dot(yg_ref[...], w_ref[MLSTM_W + FOX_W:, :], preferred_element_type=F32)
    o_ref[...] = x_ref[...] + acc


def _outproj(y_m, y_f, y_g, w_out, x, *, tm=512):
    m = x.shape[0]
    return pl.pallas_call(
        _outproj_body,
        out_shape=jax.ShapeDtypeStruct((m, D_MODEL), F32),
        grid=(m // tm,),
        in_specs=[
            pl.BlockSpec((tm, MLSTM_W), lambda i: (i, 0)),
            pl.BlockSpec((tm, FOX_W), lambda i: (i, 0)),
            pl.BlockSpec((tm, GDN_W), lambda i: (i, 0)),
            pl.BlockSpec((D_MODEL, D_MODEL), lambda i: (0, 0)),
            pl.BlockSpec((tm, D_MODEL), lambda i: (i, 0)),
        ],
        out_specs=pl.BlockSpec((tm, D_MODEL), lambda i: (i, 0)),
        compiler_params=_params(("parallel",)),
        name="outproj",
    )(y_m, y_f, y_g, w_out, x)


def _ffn_body(x_ref, nw_ref, wg_ref, wu_ref, wd_ref, fw_ref, o_ref, h_scr, *, final_norm, row_chunk):
    j = pl.program_id(1)
    chunks = [slice(r * row_chunk, (r + 1) * row_chunk) for r in range(x_ref.shape[0] // row_chunk)]

    @pl.when(j == 0)
    def _():
        for rows in chunks:
            x = x_ref[rows, :]
            h_scr[rows, :] = (_rms(x) * nw_ref[...]).astype(BF16)
            o_ref[rows, :] = x

    h = h_scr[...]
    gate = jnp.dot(h, wg_ref[...], preferred_element_type=F32)
    up = jnp.dot(h, wu_ref[...], preferred_element_type=F32)
    act = (gate * _sigmoid(gate) * up).astype(BF16)
    o_ref[...] += jnp.dot(act, wd_ref[...], preferred_element_type=F32)

    if final_norm:
        @pl.when(j == pl.num_programs(1) - 1)
        def _():
            for rows in chunks:
                o_ref[rows, :] = _rms(o_ref[rows, :]) * fw_ref[...]


def _ffn(x, norm_w, w_gate, w_up, w_down, final_w, *, final_norm, tm=1024, tf=512):
    m = x.shape[0]
    return pl.pallas_call(
        functools.partial(_ffn_body, final_norm=final_norm, row_chunk=256),
        out_shape=jax.ShapeDtypeStruct((m, D_MODEL), F32),
        grid=(m // tm, D_FF // tf),
        in_specs=[
            pl.BlockSpec((tm, D_MODEL), lambda i, j: (i, 0)),
            pl.BlockSpec((1, D_MODEL), lambda i, j: (0, 0)),
            pl.BlockSpec((D_MODEL, tf), lambda i, j: (0, j)),
            pl.BlockSpec((D_MODEL, tf), lambda i, j: (0, j)),
            pl.BlockSpec((tf, D_MODEL), lambda i, j: (j, 0)),
            pl.BlockSpec((1, D_MODEL), lambda i, j: (0, 0)),
        ],
        out_specs=pl.BlockSpec((tm, D_MODEL), lambda i, j: (i, 0)),
        scratch_shapes=[pltpu.VMEM((tm, D_MODEL), BF16)],
        compiler_params=_params(("parallel", "arbitrary")),
        name="ffn",
    )(x, norm_w, w_gate, w_up, w_down, final_w)


def _prep_w_in_body(w_ref, wide_ref, gate_ref):
    o_mi = 4 * MLSTM_W
    o_fq = o_mi + 2 * MLSTM_HEADS
    o_ff = o_fq + 3 * FOX_W
    o_gq = o_ff + FOX_HEADS
    o_ga = o_gq + 4 * GDN_W
    wide_ref[0:o_mi, :] = w_ref[0:o_mi, :].astype(BF16)
    wide_ref[o_mi:o_mi + 3 * FOX_W, :] = w_ref[o_fq:o_ff, :].astype(BF16)
    wide_ref[o_mi + 3 * FOX_W:, :] = w_ref[o_gq:o_ga, :].astype(BF16)
    gate = jnp.concatenate([w_ref[o_mi:o_fq, :], w_ref[o_ff:o_gq, :], w_ref[o_ga:, :],
                            jnp.zeros((SUBLANES, w_ref.shape[1]), F32)], axis=0)
    gate_ref[...] = gate.astype(BF16)


def _prep_w_in(w_in_t, layer, *, cols=256):
    _, n_in, d = w_in_t.shape
    return pl.pallas_call(
        _prep_w_in_body,
        out_shape=(jax.ShapeDtypeStruct((PROJ_W, d), BF16), jax.ShapeDtypeStruct((GATE_ROWS, d), BF16)),
        grid=(d // cols,),
        in_specs=[pl.BlockSpec((None, n_in, cols), lambda i: (layer, 0, i))],
        out_specs=(pl.BlockSpec((PROJ_W, cols), lambda i: (0, i)), pl.BlockSpec((GATE_ROWS, cols), lambda i: (0, i))),
        compiler_params=_params(("parallel",)),
        name="prep_w_in",
    )(w_in_t)


def kernel(x, mix_norm_w, w_in, mlstm_i_bias, mlstm_f_bias, fox_f_bias, gdn_conv_w, gdn_a_log, gdn_dt_bias,
           mlstm_out_norm_w, fox_out_norm_w, gdn_out_norm_w, w_out, ffn_norm_w, w_gate, w_up, w_down,
           final_norm_w):
    batch, seq, d_model = x.shape
    depth = w_in.shape[0]
    assert d_model == D_MODEL and seq % FOX_BLOCK == 0 and seq % GDN_STEP == 0
    xf = x.reshape(batch * seq, d_model)
    zeros4 = jnp.zeros((GDN_HEADS,), F32)
    w_in_t = jnp.swapaxes(w_in, 1, 2)
    side_weights = [(w_gate, 64, 0), (w_up, 64, 24), (w_down, 128, 0), (w_out, 128, 40)]
    for l in range(depth):
        w_wide, w_gate_t = _prep_w_in(w_in_t, l)
        bias_col = jnp.pad(jnp.concatenate([mlstm_i_bias[l], mlstm_f_bias[l], fox_f_bias[l], gdn_dt_bias[l],
                                            zeros4]), (0, 8)).reshape(GATE_ROWS, 1)
        alog_col = jnp.pad(gdn_a_log[l], (16, 12)).reshape(GATE_ROWS, 1)

        proj, gt, (w_gate_b, w_up_b, w_down_b, w_out_b) = _inproj(
            xf, mix_norm_w[l].reshape(1, -1), w_wide, w_gate_t, side_weights, l)
        gr, gr3 = _gates(gt, bias_col, alog_col, batch=batch, seq=seq)
        gc3 = gr3.T
        c_rows = gr[8:16].reshape(FOX_HEADS, batch * (seq // FOX_BLOCK), 1, FOX_BLOCK)

        y_m = _mlstm(proj, gr, gc3, mlstm_out_norm_w[l].reshape(1, -1), batch=batch, seq=seq)
        y_f = _fox(proj, gc3, c_rows, fox_out_norm_w[l].reshape(1, -1), batch=batch, seq=seq)
        y_g = _gdn(proj, gdn_conv_w[l], gr, gc3, gdn_out_norm_w[l].reshape(1, -1), batch=batch, seq=seq)
        xf = _outproj(y_m, y_f, y_g, w_out_b, xf)
        xf = _ffn(xf, ffn_norm_w[l].reshape(1, -1), w_gate_b, w_up_b, w_down_b, final_norm_w.reshape(1, -1),
                  final_norm=(l == depth - 1))
    return xf.reshape(batch, seq, d_model)
```

```python
import functools

import jax
import jax.numpy as jnp
import numpy as np
from jax import lax
from jax.experimental import pallas as pl
from jax.experimental.pallas import tpu as pltpu

F32 = jnp.float32
BF16 = jnp.bfloat16

D_MODEL = 2048
HEAD_DIM = 128
MLSTM_HEADS = 4
FOX_HEADS = 8
GDN_HEADS = 4
MLSTM_W = MLSTM_HEADS * HEAD_DIM
FOX_W = FOX_HEADS * HEAD_DIM
GDN_W = GDN_HEADS * HEAD_DIM
D_FF = 5632
CONV_WIDTH = 4
GATE_SOFTCAP = 15.0
NORM_EPS = 1e-6
QK_SCALE = HEAD_DIM ** -0.5
LOG2E = 1.4426950408889634

PROJ_W = 4 * MLSTM_W + 3 * FOX_W + 4 * GDN_W
GATE_ROWS = 32

LANES = 128
SUBLANES = 8
V7X_VMEM_BYTES = 64 * 1024 * 1024
VMEM_LIMIT = 56 * 1024 * 1024

CHUNK = 128
FOX_BLOCK = 512
FOX_GROUP = 4
FOX_PAIR = 4
GDN_STEP = 512
MLSTM_STEP = 512

MASK_STRICT, MASK_INCL, MASK_DIAG8, MASK_OFF0 = 0, 1, 2, 3
MASK_COUNT = MASK_OFF0 + 4

NT_DIMS = (((1,), (1,)), ((), ()))
TN_DIMS = (((0,), (0,)), ((), ()))


def _params(sem):
    return pltpu.CompilerParams(dimension_semantics=sem, vmem_limit_bytes=VMEM_LIMIT)


def _sigmoid(z):
    return 1.0 / (1.0 + jnp.exp(-z))


def _log_sigmoid(z):
    return jnp.minimum(z, 0.0) - jnp.log1p(jnp.exp(-jnp.abs(z)))


def _softplus(z):
    return jnp.maximum(z, 0.0) + jnp.log1p(jnp.exp(-jnp.abs(z)))


def _rms(t):
    return t * lax.rsqrt(jnp.mean(t * t, axis=-1, keepdims=True) + NORM_EPS)


def _inproj_body(*refs, row_chunk, n_side):
    x_ref, nw_ref, w_ref, wg_ref = refs[:4]
    side_in = refs[4:4 + n_side]
    proj_ref, gt_ref = refs[4 + n_side:6 + n_side]
    side_out = refs[6 + n_side:6 + 2 * n_side]
    h_scr = refs[6 + 2 * n_side]

    @pl.when(pl.program_id(1) == 0)
    def _():
        for r in range(x_ref.shape[0] // row_chunk):
            rows = slice(r * row_chunk, (r + 1) * row_chunk)
            h_scr[rows, :] = (_rms(x_ref[rows, :]) * nw_ref[...]).astype(BF16)
        gt_ref[...] = lax.dot_general(wg_ref[...], h_scr[...], NT_DIMS, preferred_element_type=F32)

    for src, dst in zip(side_in, side_out):
        dst[...] = src[...].astype(BF16)

    proj_ref[...] = lax.dot_general(h_scr[...], w_ref[...], NT_DIMS, preferred_element_type=F32).astype(BF16)


def _inproj(x, norm_w, w_big, w_gate_t, side_weights, layer, *, tm=1024, tn=1024):
    m = x.shape[0]
    ni, nj = m // tm, PROJ_W // tn
    side_in_specs, side_out_specs, side_out_shapes = [], [], []
    for w, band, first in side_weights:
        _, rows, cols = w.shape
        count = rows // band
        assert rows % band == 0 and first + count <= ni * nj

        def band_index(i, j, first=first, count=count):
            return jnp.clip(i * nj + j - first, 0, count - 1)

        side_in_specs.append(pl.BlockSpec((None, band, cols), lambda i, j, f=band_index: (layer, f(i, j), 0)))
        side_out_specs.append(pl.BlockSpec((band, cols), lambda i, j, f=band_index: (f(i, j), 0)))
        side_out_shapes.append(jax.ShapeDtypeStruct((rows, cols), BF16))
    outs = pl.pallas_call(
        functools.partial(_inproj_body, row_chunk=256, n_side=len(side_weights)),
        out_shape=(jax.ShapeDtypeStruct((m, PROJ_W), BF16), jax.ShapeDtypeStruct((GATE_ROWS, m), F32),
                   *side_out_shapes),
        grid=(ni, nj),
        in_specs=[
            pl.BlockSpec((tm, D_MODEL), lambda i, j: (i, 0)),
            pl.BlockSpec((1, D_MODEL), lambda i, j: (0, 0)),
            pl.BlockSpec((tn, D_MODEL), lambda i, j: (j, 0)),
            pl.BlockSpec((GATE_ROWS, D_MODEL), lambda i, j: (0, 0)),
            *side_in_specs,
        ],
        out_specs=(
            pl.BlockSpec((tm, tn), lambda i, j: (i, j)),
            pl.BlockSpec((GATE_ROWS, tm), lambda i, j: (0, i)),
            *side_out_specs,
        ),
        scratch_shapes=[pltpu.VMEM((tm, D_MODEL), BF16)],
        compiler_params=_params(("arbitrary", "arbitrary")),
        name="inproj",
    )(x, norm_w, w_big, w_gate_t, *[w for w, _, _ in side_weights])
    return outs[0], outs[1], outs[2:]


def _lane_cumsum(v, seg):
    pos = lax.broadcasted_iota(jnp.int32, v.shape, 1) & (seg - 1)
    shift = 1
    while shift < seg:
        v = v + jnp.where(pos >= shift, pltpu.roll(v, shift, 1), 0.0)
        shift *= 2
    return v


def _lane_cummax(v, seg):
    pos = lax.broadcasted_iota(jnp.int32, v.shape, 1) & (seg - 1)
    shift = 1
    while shift < seg:
        v = jnp.maximum(v, jnp.where(pos >= shift, pltpu.roll(v, shift, 1), -jnp.inf))
        shift *= 2
    return v


def _gates_body(gt_ref, bias_ref, alog_ref, out_ref, split_ref):
    seq = gt_ref.shape[1]
    first4 = lax.broadcasted_iota(jnp.int32, (SUBLANES, seq), 0) < 4
    z = gt_ref[0:8, :] + bias_ref[0:8, :]
    z = GATE_SOFTCAP * jnp.tanh(z / GATE_SOFTCAP)
    b_cum = _lane_cumsum(_log_sigmoid(z), CHUNK)
    out_ref[0:8, :] = jnp.where(first4, z, b_cum)
    d = z - pltpu.roll(b_cum, 4, 0)
    out_ref[24:32, :] = jnp.where(first4, d, pltpu.roll(_lane_cummax(d, CHUNK), 4, 0))

    raw = gt_ref[16:24, :]
    decay = _lane_cumsum(-jnp.exp(alog_ref[16:24, :]) * _softplus(raw + bias_ref[16:24, :]), CHUNK)
    out_ref[16:24, :] = jnp.where(first4, decay, _sigmoid(raw))

    fox = _lane_cumsum(LOG2E * _log_sigmoid(gt_ref[8:16, :] + bias_ref[8:16, :]), LANES)
    carry = jnp.zeros((SUBLANES, 1), F32)
    for blk in range(seq // LANES):
        lanes = slice(blk * LANES, (blk + 1) * LANES)
        tile = fox[:, lanes] + carry
        out_ref[8:16, lanes] = tile
        carry = tile[:, LANES - 1:LANES]

    value = out_ref[...]
    hi = value.astype(BF16)
    rest = value - hi.astype(F32)
    mid = rest.astype(BF16)
    split_ref[0:GATE_ROWS, :] = hi
    split_ref[GATE_ROWS:2 * GATE_ROWS, :] = mid
    split_ref[2 * GATE_ROWS:3 * GATE_ROWS, :] = (rest - mid.astype(F32)).astype(BF16)
    split_ref[3 * GATE_ROWS:, :] = jnp.zeros((LANES - 3 * GATE_ROWS, seq), BF16)


def _gates(gt, bias_col, alog_col, *, batch, seq):
    return pl.pallas_call(
        _gates_body,
        out_shape=(jax.ShapeDtypeStruct(gt.shape, F32), jax.ShapeDtypeStruct((LANES, gt.shape[1]), BF16)),
        grid=(batch,),
        in_specs=[
            pl.BlockSpec((GATE_ROWS, seq), lambda b: (0, b)),
            pl.BlockSpec((GATE_ROWS, 1), lambda b: (0, 0)),
            pl.BlockSpec((GATE_ROWS, 1), lambda b: (0, 0)),
        ],
        out_specs=(pl.BlockSpec((GATE_ROWS, seq), lambda b: (0, b)), pl.BlockSpec((LANES, seq), lambda b: (0, b))),
        compiler_params=_params(("parallel",)),
        name="gates",
    )(gt, bias_col, alog_col)


def _mlstm_body(q_ref, k_ref, v_ref, o_ref, gb_ref, gd_ref, gc3_ref, sel_ref, mean_ref, nw_ref, y_ref,
                ct_scr, m_scr):
    @pl.when(pl.program_id(1) == 0)
    def _():
        ct_scr[...] = jnp.zeros(ct_scr.shape, F32)
        m_scr[...] = jnp.zeros(m_scr.shape, F32)

    heads = range(MLSTM_HEADS)
    causal = (lax.broadcasted_iota(jnp.int32, (CHUNK, CHUNK), 0)
              >= lax.broadcasted_iota(jnp.int32, (CHUNK, CHUNK), 1))
    ones = jnp.ones((CHUNK, HEAD_DIM), BF16)
    cts = [ct_scr[h] for h in heads]
    ms = [m_scr[h:h + 1, 0:1] for h in heads]
    for r in range(q_ref.shape[0] // CHUNK):
        rows = slice(r * CHUNK, (r + 1) * CHUNK)
        hcols = [slice(h * HEAD_DIM, (h + 1) * HEAD_DIM) for h in heads]
        q = [q_ref[rows, c] for c in hcols]
        k = [k_ref[rows, c] for c in hcols]
        v_aug = [jnp.concatenate([v_ref[rows, c], ones], axis=1) for c in hcols]
        cols_rep = jnp.dot(gc3_ref[rows, :], sel_ref[...], preferred_element_type=F32)
        rep = lambda h, j: cols_rep[:, (3 * h + j) * LANES:(3 * h + j + 1) * LANES]
        a_col = [rep(h, 0) for h in heads]
        d_col = [rep(h, 1) for h in heads]
        b_col = [rep(h, 2) for h in heads]
        d_row = [gd_ref[h:h + 1, rows] for h in heads]
        b_last = [gb_ref[4 + h:5 + h, rows][:, CHUNK - 1:CHUNK] for h in heads]
        a_last = [gd_ref[4 + h:5 + h, rows][:, CHUNK - 1:CHUNK] for h in heads]

        qk = [lax.dot_general(q[h], k[h], NT_DIMS, preferred_element_type=F32) for h in heads]
        q_state = [lax.dot_general(q[h], cts[h].astype(BF16), NT_DIMS, preferred_element_type=F32)
                   for h in heads]
        m_rel = [jnp.maximum(ms[h], a_col[h]) for h in heads]
        p = [jnp.exp(jnp.where(causal, d_row[h] - m_rel[h], -jnp.inf)) * (qk[h] * QK_SCALE) for h in heads]
        intra = [jnp.dot(p[h].astype(BF16), v_aug[h], preferred_element_type=F32) for h in heads]
        for h in heads:
            w_inter = jnp.exp(ms[h] - m_rel[h])
            both = jnp.concatenate([w_inter, w_inter], axis=1) * q_state[h] + intra[h]
            floor = jnp.exp(-(b_col[h] + m_rel[h]))
            h_out = both[:, :HEAD_DIM] / jnp.maximum(jnp.abs(both[:, HEAD_DIM:]), floor)
            mean = jnp.dot(h_out.astype(BF16), mean_ref[...], preferred_element_type=F32)
            centered = h_out - mean
            var = jnp.dot((centered * centered).astype(BF16), mean_ref[...], preferred_element_type=F32)
            gate = _sigmoid(o_ref[rows, hcols[h]].astype(F32))
            y_ref[rows, hcols[h]] = (centered * lax.rsqrt(var + NORM_EPS) * nw_ref[:, hcols[h]]
                                     * gate).astype(BF16)

        m_end = [jnp.maximum(ms[h], a_last[h]) for h in heads]
        kw = [(k[h].astype(F32) * (jnp.exp(d_col[h] - m_end[h]) * QK_SCALE)).astype(BF16) for h in heads]
        cts = [jnp.exp(ms[h] - m_end[h]) * cts[h]
               + lax.dot_general(v_aug[h], kw[h], TN_DIMS, preferred_element_type=F32) for h in heads]
        ms = [b_last[h] + m_end[h] for h in heads]
    for h in heads:
        ct_scr[h] = cts[h]
        m_scr[h:h + 1, :] = jnp.broadcast_to(ms[h], (1, LANES))


def _one_hot_columns(gate_cols):
    sel = np.zeros((LANES, LANES * len(gate_cols)), np.float32)
    for j, col in enumerate(gate_cols):
        for part in range(3):
            sel[part * GATE_ROWS + col, j * LANES:(j + 1) * LANES] = 1.0
    return jnp.asarray(sel, BF16)


def _mlstm(proj, gr, gc3, norm_w, *, batch, seq):
    m = proj.shape[0]
    nc = seq // MLSTM_STEP
    row = lambda b, c: b * nc + c
    sel = _one_hot_columns([c for h in range(MLSTM_HEADS) for c in (28 + h, 24 + h, 4 + h)])
    mean_w = jnp.full((HEAD_DIM, HEAD_DIM), 1.0 / HEAD_DIM, BF16)
    return pl.pallas_call(
        _mlstm_body,
        out_shape=jax.ShapeDtypeStruct((m, MLSTM_W), BF16),
        grid=(batch, nc),
        in_specs=[
            pl.BlockSpec((MLSTM_STEP, MLSTM_W), lambda b, c: (row(b, c), 0)),
            pl.BlockSpec((MLSTM_STEP, MLSTM_W), lambda b, c: (row(b, c), 1)),
            pl.BlockSpec((MLSTM_STEP, MLSTM_W), lambda b, c: (row(b, c), 2)),
            pl.BlockSpec((MLSTM_STEP, MLSTM_W), lambda b, c: (row(b, c), 3)),
            pl.BlockSpec((SUBLANES, MLSTM_STEP), lambda b, c: (0, row(b, c))),
            pl.BlockSpec((SUBLANES, MLSTM_STEP), lambda b, c: (3, row(b, c))),
            pl.BlockSpec((MLSTM_STEP, LANES), lambda b, c: (row(b, c), 0)),
            pl.BlockSpec(sel.shape, lambda b, c: (0, 0)),
            pl.BlockSpec((HEAD_DIM, HEAD_DIM), lambda b, c: (0, 0)),
            pl.BlockSpec((1, MLSTM_W), lambda b, c: (0, 0)),
        ],
        out_specs=pl.BlockSpec((MLSTM_STEP, MLSTM_W), lambda b, c: (row(b, c), 0)),
        scratch_shapes=[
            pltpu.VMEM((MLSTM_HEADS, 2 * HEAD_DIM, HEAD_DIM), F32),
            pltpu.VMEM((SUBLANES, LANES), F32),
        ],
        compiler_params=_params(("parallel", "arbitrary")),
        name="mlstm",
    )(proj, proj, proj, proj, gr, gr, gc3, sel, mean_w, norm_w)


def _fox_body(q_ref, k_ref, v_ref, g3_ref, c_ref, nw_ref, y_ref, acc_scr, m_scr, l_scr, qa_scr, vt_scr):
    pair = pl.program_id(1)
    qi = pl.program_id(2)
    tq = q_ref.shape[0]
    heads = range(FOX_PAIR)
    hcols = [slice(e * HEAD_DIM, (e + 1) * HEAD_DIM) for e in heads]

    @pl.when(qi == 0)
    def _():
        for e in heads:
            for j in range(vt_scr.shape[1]):
                vt_scr[e, j] = v_ref[j * tq:(j + 1) * tq, hcols[e]].T

    lane = lax.broadcasted_iota(jnp.int32, (tq, LANES), 1)
    for e in heads:
        fox_row = 2 * MLSTM_HEADS + FOX_PAIR * pair + e
        split_lane = (lane == fox_row) | (lane == GATE_ROWS + fox_row) | (lane == 2 * GATE_ROWS + fox_row)
        qa_scr[e, :, 0:HEAD_DIM] = (q_ref[:, hcols[e]].astype(F32) * (QK_SCALE * LOG2E)).astype(BF16)
        qa_scr[e, :, HEAD_DIM:] = jnp.where(split_lane, -1.0, 0.0).astype(BF16)
    c_q = [c_ref[e, qi] for e in heads]
    m_scr[...] = jnp.full(m_scr.shape, -jnp.inf, F32)
    l_scr[...] = jnp.zeros(l_scr.shape, F32)
    acc_scr[...] = jnp.zeros(acc_scr.shape, F32)

    def logits(kb, diagonal):
        base = pl.multiple_of(kb * tq, tq)
        gates = g3_ref[pl.ds(base, tq), :]
        zs = []
        for e in heads:
            keys = jnp.concatenate([k_ref[pl.ds(base, tq), hcols[e]], gates], axis=1)
            z = lax.dot_general(keys, qa_scr[e], NT_DIMS, preferred_element_type=F32)
            if diagonal:
                visible = (lax.broadcasted_iota(jnp.int32, (tq, tq), 0)
                           <= lax.broadcasted_iota(jnp.int32, (tq, tq), 1))
                z = jnp.where(visible, z, -jnp.inf)
            zs.append(z)
        return zs

    def update(kb, zs):
        ps, alphas = [], []
        for e in heads:
            m_prev = m_scr[e, 0:1, :]
            m_new = jnp.maximum(m_prev, jnp.max(zs[e], axis=0, keepdims=True) + c_q[e])
            alpha = jnp.exp2(m_prev - m_new)
            p = jnp.exp2(zs[e] - (m_new - c_q[e]))
            l_scr[e, 0:1, :] = alpha * l_scr[e, 0:1, :] + jnp.sum(p, axis=0, keepdims=True)
            m_scr[e, 0:1, :] = m_new
            ps.append(p.astype(BF16))
            alphas.append(alpha)
        for e in heads:
            acc_scr[e] = alphas[e] * acc_scr[e] + jnp.dot(vt_scr[e, kb], ps[e], preferred_element_type=F32)

    def blocks(first, count):
        zs = [logits(first + j, False) for j in range(count)]
        for j in range(count):
            update(first + j, zs[j])

    def group(g, carry):
        blocks(g * FOX_GROUP, FOX_GROUP)
        return carry

    lax.fori_loop(0, qi // FOX_GROUP, group, 0)
    done = (qi // FOX_GROUP) * FOX_GROUP
    size = FOX_GROUP // 2
    while size >= 1:
        take = (qi & size) != 0
        pl.when(take)(functools.partial(blocks, done, size))
        done = done + jnp.where(take, size, 0)
        size //= 2
    update(qi, logits(qi, True))
    for e in heads:
        out = (acc_scr[e] / l_scr[e, 0:1, :]).T
        y_ref[:, hcols[e]] = (_rms(out) * nw_ref[:, hcols[e]]).astype(BF16)


def _fox(proj, g3, c_rows, norm_w, *, batch, seq):
    m = proj.shape[0]
    nq = seq // FOX_BLOCK
    width = FOX_PAIR * HEAD_DIM
    q_col = (4 * MLSTM_W) // width
    k_col = q_col + FOX_W // width
    v_col = k_col + FOX_W // width
    return pl.pallas_call(
        _fox_body,
        out_shape=jax.ShapeDtypeStruct((m, FOX_W), BF16),
        grid=(batch, FOX_HEADS // FOX_PAIR, nq),
        in_specs=[
            pl.BlockSpec((FOX_BLOCK, width), lambda b, h, i: (b * nq + i, q_col + h)),
            pl.BlockSpec((seq, width), lambda b, h, i: (b, k_col + h)),
            pl.BlockSpec((seq, width), lambda b, h, i: (b, v_col + h)),
            pl.BlockSpec((seq, LANES), lambda b, h, i: (b, 0)),
            pl.BlockSpec((FOX_PAIR, nq, 1, FOX_BLOCK), lambda b, h, i: (h, b, 0, 0)),
            pl.BlockSpec((1, width), lambda b, h, i: (0, h)),
        ],
        out_specs=pl.BlockSpec((FOX_BLOCK, width), lambda b, h, i: (b * nq + i, h)),
        scratch_shapes=[
            pltpu.VMEM((FOX_PAIR, HEAD_DIM, FOX_BLOCK), F32),
            pltpu.VMEM((FOX_PAIR, SUBLANES, FOX_BLOCK), F32),
            pltpu.VMEM((FOX_PAIR, SUBLANES, FOX_BLOCK), F32),
            pltpu.VMEM((FOX_PAIR, FOX_BLOCK, 2 * HEAD_DIM), BF16),
            pltpu.VMEM((FOX_PAIR, nq, HEAD_DIM, FOX_BLOCK), BF16),
        ],
        compiler_params=_params(("parallel", "parallel", "arbitrary")),
        name="fox",
    )(proj, proj, proj, g3, c_rows, norm_w)


def _mm(x, y):
    return jnp.dot(x.astype(BF16), y.astype(BF16), preferred_element_type=F32)


def _unit_lower_inverses(a_mats, mask_ref):
    eye = mask_ref[MASK_INCL] - mask_ref[MASK_STRICT]
    negs = [-(a * mask_ref[MASK_DIAG8]) for a in a_mats]
    invs = [eye + n for n in negs]
    for _ in range(2):
        negs = [_mm(n, n) for n in negs]
        invs = [i + _mm(i, n) for i, n in zip(invs, negs)]
    for level in range(MASK_OFF0, MASK_COUNT):
        tmps = [_mm(a * mask_ref[level], i) for a, i in zip(a_mats, invs)]
        invs = [i - _mm(i, t) for i, t in zip(invs, tmps)]
    return invs


def _gdn_body(q_ref, k_ref, v_ref, z_ref, cw_ref, gr_ref, gc3_ref, sel_ref, ones_ref, nw_ref, y_ref,
              s_scr, ext_scr, mask_scr):
    step = q_ref.shape[0]
    halo = SUBLANES
    nchunk = step // CHUNK

    @pl.when(pl.program_id(1) == 0)
    def _():
        s_scr[...] = jnp.zeros(s_scr.shape, F32)
        ext_scr[0:halo, :] = jnp.zeros((halo, 3 * GDN_W), F32)
        ri = lax.broadcasted_iota(jnp.int32, (CHUNK, CHUNK), 0)
        ci = lax.broadcasted_iota(jnp.int32, (CHUNK, CHUNK), 1)
        mask_scr[MASK_STRICT] = jnp.where(ri > ci, 1.0, 0.0)
        mask_scr[MASK_INCL] = jnp.where(ri >= ci, 1.0, 0.0)
        mask_scr[MASK_DIAG8] = jnp.where((ri >> 3) == (ci >> 3), 1.0, 0.0)
        for bits in range(3, 3 + MASK_COUNT - MASK_OFF0):
            sibling = ((ri >> (bits + 1)) == (ci >> (bits + 1))) & ((ri >> bits) != (ci >> bits))
            mask_scr[MASK_OFF0 + bits - 3] = jnp.where(sibling, 1.0, 0.0)

    ext_scr[halo:halo + step, 0:GDN_W] = q_ref[...].astype(F32)
    ext_scr[halo:halo + step, GDN_W:2 * GDN_W] = k_ref[...].astype(F32)
    ext_scr[halo:halo + step, 2 * GDN_W:3 * GDN_W] = v_ref[...].astype(F32)
    ext = ext_scr[...]
    conv = ext * cw_ref[0:1, :]
    for j in range(1, CONV_WIDTH):
        conv = pltpu.roll(conv, 1, 0) + ext * cw_ref[j:j + 1, :]
    conv = conv[halo:, :]
    ext_scr[0:halo, :] = ext_scr[step:step + halo, :]
    qkv = conv * _sigmoid(conv)

    def row_sum(t):
        return jnp.dot(t.astype(BF16), ones_ref[...], preferred_element_type=F32)

    def l2n(t):
        return t * lax.rsqrt(row_sum(t * t) + NORM_EPS)

    cols_rep = [jnp.dot(gc3_ref[r * CHUNK:(r + 1) * CHUNK, :], sel_ref[...], preferred_element_type=F32)
                for r in range(nchunk)]

    groups = [(h, r) for h in range(GDN_HEADS) for r in range(nchunk)]
    prep = {}
    for h, r in groups:
        rows = slice(r * CHUNK, (r + 1) * CHUNK)
        q = l2n(qkv[rows, h * HEAD_DIM:(h + 1) * HEAD_DIM]) * QK_SCALE
        k = l2n(qkv[rows, GDN_W + h * HEAD_DIM:GDN_W + (h + 1) * HEAD_DIM])
        v = qkv[rows, 2 * GDN_W + h * HEAD_DIM:2 * GDN_W + (h + 1) * HEAD_DIM]
        g_row = gr_ref[h:h + 1, rows]
        g_col = cols_rep[r][:, (2 * h) * LANES:(2 * h + 1) * LANES]
        beta = cols_rep[r][:, (2 * h + 1) * LANES:(2 * h + 2) * LANES]
        incl = mask_scr[MASK_INCL]
        decay = jnp.exp((g_col - g_row) * incl) * incl
        kb = k * beta
        e_g = jnp.exp(g_col)
        g_last = g_row[:, CHUNK - 1:CHUNK]
        kk_qk = lax.dot_general(jnp.concatenate([kb, q], axis=0).astype(BF16), k.astype(BF16), NT_DIMS,
                                preferred_element_type=F32)
        prep[h, r] = dict(
            a=kk_qk[:CHUNK] * (decay * mask_scr[MASK_STRICT]),
            attn=kk_qk[CHUNK:] * decay,
            rhs=jnp.concatenate([v * beta, kb * e_g], axis=1),
            q_dec=q * e_g,
            k_end=k * jnp.exp(g_last - g_col),
            s_decay=jnp.exp(g_last),
        )
    invs = _unit_lower_inverses([prep[g]["a"] for g in groups], mask_scr)
    for g, inv in zip(groups, invs):
        w = _mm(inv, prep[g]["rhs"])
        prep[g]["w_val"], prep[g]["w_key"] = w[:, :HEAD_DIM], w[:, HEAD_DIM:]

    states = [s_scr[h] for h in range(GDN_HEADS)]
    for r in range(nchunk):
        rows = slice(r * CHUNK, (r + 1) * CHUNK)
        ks = [_mm(jnp.concatenate([prep[h, r]["w_key"], prep[h, r]["q_dec"]], axis=0), states[h])
              for h in range(GDN_HEADS)]
        us = [prep[h, r]["w_val"] - ks[h][:CHUNK] for h in range(GDN_HEADS)]
        outs = [ks[h][CHUNK:] + _mm(prep[h, r]["attn"], us[h]) for h in range(GDN_HEADS)]
        states = [prep[h, r]["s_decay"] * states[h]
                  + lax.dot_general(prep[h, r]["k_end"].astype(BF16), us[h].astype(BF16), TN_DIMS,
                                    preferred_element_type=F32) for h in range(GDN_HEADS)]
        for h in range(GDN_HEADS):
            cols = slice(h * HEAD_DIM, (h + 1) * HEAD_DIM)
            gate = z_ref[rows, cols].astype(F32)
            normed = outs[h] * lax.rsqrt(row_sum(outs[h] * outs[h]) * (1.0 / HEAD_DIM) + NORM_EPS)
            y_ref[rows, cols] = (normed * nw_ref[:, cols] * (gate * _sigmoid(gate))).astype(BF16)
    for h in range(GDN_HEADS):
        s_scr[h] = states[h]


def _gdn(proj, conv_w, gr, gc3, norm_w, *, batch, seq):
    m = proj.shape[0]
    ns = seq // GDN_STEP
    sel = _one_hot_columns([c for h in range(GDN_HEADS) for c in (16 + h, 20 + h)])
    col0 = (4 * MLSTM_W + 3 * FOX_W) // GDN_W
    row = lambda b, s: b * ns + s
    return pl.pallas_call(
        _gdn_body,
        out_shape=jax.ShapeDtypeStruct((m, GDN_W), BF16),
        grid=(batch, ns),
        in_specs=[
            pl.BlockSpec((GDN_STEP, GDN_W), lambda b, s: (row(b, s), col0)),
            pl.BlockSpec((GDN_STEP, GDN_W), lambda b, s: (row(b, s), col0 + 1)),
            pl.BlockSpec((GDN_STEP, GDN_W), lambda b, s: (row(b, s), col0 + 2)),
            pl.BlockSpec((GDN_STEP, GDN_W), lambda b, s: (row(b, s), col0 + 3)),
            pl.BlockSpec((CONV_WIDTH, 3 * GDN_W), lambda b, s: (0, 0)),
            pl.BlockSpec((SUBLANES, GDN_STEP), lambda b, s: (2, row(b, s))),
            pl.BlockSpec((GDN_STEP, LANES), lambda b, s: (row(b, s), 0)),
            pl.BlockSpec(sel.shape, lambda b, s: (0, 0)),
            pl.BlockSpec((HEAD_DIM, HEAD_DIM), lambda b, s: (0, 0)),
            pl.BlockSpec((1, GDN_W), lambda b, s: (0, 0)),
        ],
        out_specs=pl.BlockSpec((GDN_STEP, GDN_W), lambda b, s: (row(b, s), 0)),
        scratch_shapes=[
            pltpu.VMEM((GDN_HEADS, HEAD_DIM, HEAD_DIM), F32),
            pltpu.VMEM((GDN_STEP + SUBLANES, 3 * GDN_W), F32),
            pltpu.VMEM((MASK_COUNT, CHUNK, CHUNK), F32),
        ],
        compiler_params=_params(("parallel", "arbitrary")),
        name="gdn",
    )(proj, proj, proj, proj, conv_w, gr, gc3, sel, jnp.ones((HEAD_DIM, HEAD_DIM), BF16), norm_w)


def _outproj_body(ym_ref, yf_ref, yg_ref, w_ref, x_ref, *rest):
    o_ref = rest[-1] if len(rest) == 1 else rest[1]
    acc = jnp.dot(ym_ref[...], w_ref[0:MLSTM_W, :], preferred_element_type=F32)
    acc += jnp.dot(yf_ref[...], w_ref[MLSTM_W:MLSTM_W + FOX_W, :], preferred_element_type=F32)
    acc += jnp.dot(yg_ref[...], w_ref[MLSTM_W + FOX_W:, :], preferred_element_type=F32)
    o_ref[...] = x_ref[...] + acc
    if len(rest) > 1:
        w_next_ref, _, wide_ref, gate_ref = rest
        _prep_w_in_body(w_next_ref, wide_ref, gate_ref)


def _outproj(y_m, y_f, y_g, w_out, x, w_in_t=None, next_layer=None, *, tm=512):
    m = x.shape[0]
    steps = m // tm
    in_specs = [
        pl.BlockSpec((tm, MLSTM_W), lambda i: (i, 0)),
        pl.BlockSpec((tm, FOX_W), lambda i: (i, 0)),
        pl.BlockSpec((tm, GDN_W), lambda i: (i, 0)),
        pl.BlockSpec((D_MODEL, D_MODEL), lambda i: (0, 0)),
        pl.BlockSpec((tm, D_MODEL), lambda i: (i, 0)),
    ]
    out_shape = [jax.ShapeDtypeStruct((m, D_MODEL), F32)]
    out_specs = [pl.BlockSpec((tm, D_MODEL), lambda i: (i, 0))]
    operands = [y_m, y_f, y_g, w_out, x]
    if w_in_t is not None:
        _, n_in, d = w_in_t.shape
        cols = d // steps
        in_specs.append(pl.BlockSpec((None, n_in, cols), lambda i: (next_layer, 0, i)))
        out_shape += [jax.ShapeDtypeStruct((PROJ_W, d), BF16), jax.ShapeDtypeStruct((GATE_ROWS, d), BF16)]
        out_specs += [pl.BlockSpec((PROJ_W, cols), lambda i: (0, i)), pl.BlockSpec((GATE_ROWS, cols), lambda i: (0, i))]
        operands.append(w_in_t)
    outs = pl.pallas_call(
        _outproj_body,
        out_shape=tuple(out_shape),
        grid=(steps,),
        in_specs=in_specs,
        out_specs=tuple(out_specs),
        compiler_params=_params(("parallel",)),
        name="outproj",
    )(*operands)
    return outs[0], tuple(outs[1:])


def _ffn_body(x_ref, nw_ref, wg_ref, wu_ref, wd_ref, fw_ref, o_ref, h_scr, *, final_norm, row_chunk):
    j = pl.program_id(1)
    chunks = [slice(r * row_chunk, (r + 1) * row_chunk) for r in range(x_ref.shape[0] // row_chunk)]

    @pl.when(j == 0)
    def _():
        for rows in chunks:
            x = x_ref[rows, :]
            h_scr[rows, :] = (_rms(x) * nw_ref[...]).astype(BF16)
            o_ref[rows, :] = x

    h = h_scr[...]
    gate = jnp.dot(h, wg_ref[...], preferred_element_type=F32)
    up = jnp.dot(h, wu_ref[...], preferred_element_type=F32)
    act = (gate * _sigmoid(gate) * up).astype(BF16)
    o_ref[...] += jnp.dot(act, wd_ref[...], preferred_element_type=F32)

    if final_norm:
        @pl.when(j == pl.num_programs(1) - 1)
        def _():
            for rows in chunks:
                o_ref[rows, :] = _rms(o_ref[rows, :]) * fw_ref[...]


def _ffn(x, norm_w, w_gate, w_up, w_down, final_w, *, final_norm, tm=1024, tf=512):
    m = x.shape[0]
    return pl.pallas_call(
        functools.partial(_ffn_body, final_norm=final_norm, row_chunk=256),
        out_shape=jax.ShapeDtypeStruct((m, D_MODEL), F32),
        grid=(m // tm, D_FF // tf),
        in_specs=[
            pl.BlockSpec((tm, D_MODEL), lambda i, j: (i, 0)),
            pl.BlockSpec((1, D_MODEL), lambda i, j: (0, 0)),
            pl.BlockSpec((D_MODEL, tf), lambda i, j: (0, j)),
            pl.BlockSpec((D_MODEL, tf), lambda i, j: (0, j)),
            pl.BlockSpec((tf, D_MODEL), lambda i, j: (j, 0)),
            pl.BlockSpec((1, D_MODEL), lambda i, j: (0, 0)),
        ],
        out_specs=pl.BlockSpec((tm, D_MODEL), lambda i, j: (i, 0)),
        scratch_shapes=[pltpu.VMEM((tm, D_MODEL), BF16)],
        compiler_params=_params(("parallel", "arbitrary")),
        name="ffn",
    )(x, norm_w, w_gate, w_up, w_down, final_w)


def _prep_w_in_body(w_ref, wide_ref, gate_ref):
    o_mi = 4 * MLSTM_W
    o_fq = o_mi + 2 * MLSTM_HEADS
    o_ff = o_fq + 3 * FOX_W
    o_gq = o_ff + FOX_HEADS
    o_ga = o_gq + 4 * GDN_W
    wide_ref[0:o_mi, :] = w_ref[0:o_mi, :].astype(BF16)
    wide_ref[o_mi:o_mi + 3 * FOX_W, :] = w_ref[o_fq:o_ff, :].astype(BF16)
    wide_ref[o_mi + 3 * FOX_W:, :] = w_ref[o_gq:o_ga, :].astype(BF16)
    gate = jnp.concatenate([w_ref[o_mi:o_fq, :], w_ref[o_ff:o_gq, :], w_ref[o_ga:, :],
                            jnp.zeros((SUBLANES, w_ref.shape[1]), F32)], axis=0)
    gate_ref[...] = gate.astype(BF16)


def _prep_w_in(w_in_t, layer, *, cols=256):
    _, n_in, d = w_in_t.shape
    return pl.pallas_call(
        _prep_w_in_body,
        out_shape=(jax.ShapeDtypeStruct((PROJ_W, d), BF16), jax.ShapeDtypeStruct((GATE_ROWS, d), BF16)),
        grid=(d // cols,),
        in_specs=[pl.BlockSpec((None, n_in, cols), lambda i: (layer, 0, i))],
        out_specs=(pl.BlockSpec((PROJ_W, cols), lambda i: (0, i)), pl.BlockSpec((GATE_ROWS, cols), lambda i: (0, i))),
        compiler_params=_params(("parallel",)),
        name="prep_w_in",
    )(w_in_t)


def kernel(x, mix_norm_w, w_in, mlstm_i_bias, mlstm_f_bias, fox_f_bias, gdn_conv_w, gdn_a_log, gdn_dt_bias,
           mlstm_out_norm_w, fox_out_norm_w, gdn_out_norm_w, w_out, ffn_norm_w, w_gate, w_up, w_down,
           final_norm_w):
    batch, seq, d_model = x.shape
    depth = w_in.shape[0]
    assert d_model == D_MODEL and seq % FOX_BLOCK == 0 and seq % GDN_STEP == 0
    xf = x.reshape(batch * seq, d_model)
    zeros4 = jnp.zeros((GDN_HEADS,), F32)
    w_in_t = jnp.swapaxes(w_in, 1, 2)
    side_weights = [(w_gate, 64, 0), (w_up, 64, 24), (w_down, 128, 0), (w_out, 128, 40)]
    w_wide, w_gate_t = _prep_w_in(w_in_t, 0)
    for l in range(depth):
        bias_col = jnp.pad(jnp.concatenate([mlstm_i_bias[l], mlstm_f_bias[l], fox_f_bias[l], gdn_dt_bias[l],
                                            zeros4]), (0, 8)).reshape(GATE_ROWS, 1)
        alog_col = jnp.pad(gdn_a_log[l], (16, 12)).reshape(GATE_ROWS, 1)

        proj, gt, (w_gate_b, w_up_b, w_down_b, w_out_b) = _inproj(
            xf, mix_norm_w[l].reshape(1, -1), w_wide, w_gate_t, side_weights, l)
        gr, gr3 = _gates(gt, bias_col, alog_col, batch=batch, seq=seq)
        gc3 = gr3.T
        c_rows = gr[8:16].reshape(FOX_HEADS, batch * (seq // FOX_BLOCK), 1, FOX_BLOCK)

        y_m = _mlstm(proj, gr, gc3, mlstm_out_norm_w[l].reshape(1, -1), batch=batch, seq=seq)
        y_f = _fox(proj, gc3, c_rows, fox_out_norm_w[l].reshape(1, -1), batch=batch, seq=seq)
        y_g = _gdn(proj, gdn_conv_w[l], gr, gc3, gdn_out_norm_w[l].reshape(1, -1), batch=batch, seq=seq)
        if l + 1 < depth:
            xf, (w_wide, w_gate_t) = _outproj(y_m, y_f, y_g, w_out_b, xf, w_in_t, l + 1)
        else:
            xf, _ = _outproj(y_m, y_f, y_g, w_out_b, xf)
        xf = _ffn(xf, ffn_norm_w[l].reshape(1, -1), w_gate_b, w_up_b, w_down_b, final_norm_w.reshape(1, -1),
                  final_norm=(l == depth - 1))
    return xf.reshape(batch, seq, d_model)
```

```python
import functools

import jax
import jax.numpy as jnp
import numpy as np
from jax import lax
from jax.experimental import pallas as pl
from jax.experimental.pallas import tpu as pltpu

F32 = jnp.float32
BF16 = jnp.bfloat16

D_MODEL = 2048
HEAD_DIM = 128
MLSTM_HEADS = 4
FOX_HEADS = 8
GDN_HEADS = 4
MLSTM_W = MLSTM_HEADS * HEAD_DIM
FOX_W = FOX_HEADS * HEAD_DIM
GDN_W = GDN_HEADS * HEAD_DIM
D_FF = 5632
CONV_WIDTH = 4
GATE_SOFTCAP = 15.0
NORM_EPS = 1e-6
QK_SCALE = HEAD_DIM ** -0.5
LOG2E = 1.4426950408889634

PROJ_W = 4 * MLSTM_W + 3 * FOX_W + 4 * GDN_W
GATE_ROWS = 32

LANES = 128
SUBLANES = 8
V7X_VMEM_BYTES = 64 * 1024 * 1024
VMEM_LIMIT = 56 * 1024 * 1024

CHUNK = 128
FOX_BLOCK = 512
FOX_GROUP = 4
FOX_PAIR = 4
REC_STEP = 512

MASK_STRICT, MASK_INCL, MASK_DIAG8, MASK_OFF0 = 0, 1, 2, 3
MASK_COUNT = MASK_OFF0 + 4

NT_DIMS = (((1,), (1,)), ((), ()))
TN_DIMS = (((0,), (0,)), ((), ()))


def _params(sem):
    return pltpu.CompilerParams(dimension_semantics=sem, vmem_limit_bytes=VMEM_LIMIT)


def _sigmoid(z):
    return 1.0 / (1.0 + jnp.exp(-z))


def _log_sigmoid(z):
    return jnp.minimum(z, 0.0) - jnp.log1p(jnp.exp(-jnp.abs(z)))


def _softplus(z):
    return jnp.maximum(z, 0.0) + jnp.log1p(jnp.exp(-jnp.abs(z)))


def _rms(t):
    return t * lax.rsqrt(jnp.mean(t * t, axis=-1, keepdims=True) + NORM_EPS)


def _inproj_body(x_ref, nw_ref, w_ref, wg_ref, proj_ref, gt_ref, h_scr, *, row_chunk):
    @pl.when(pl.program_id(1) == 0)
    def _():
        for r in range(x_ref.shape[0] // row_chunk):
            rows = slice(r * row_chunk, (r + 1) * row_chunk)
            h_scr[rows, :] = (_rms(x_ref[rows, :]) * nw_ref[...]).astype(BF16)
        gt_ref[...] = lax.dot_general(wg_ref[...], h_scr[...], NT_DIMS, preferred_element_type=F32)

    proj_ref[...] = lax.dot_general(h_scr[...], w_ref[...], NT_DIMS, preferred_element_type=F32).astype(BF16)


def _inproj(x, norm_w, w_big, w_gate_t, *, tm=1024, tn=1792):
    m = x.shape[0]
    return pl.pallas_call(
        functools.partial(_inproj_body, row_chunk=256),
        out_shape=(jax.ShapeDtypeStruct((m, PROJ_W), BF16), jax.ShapeDtypeStruct((GATE_ROWS, m), F32)),
        grid=(m // tm, PROJ_W // tn),
        in_specs=[
            pl.BlockSpec((tm, D_MODEL), lambda i, j: (i, 0)),
            pl.BlockSpec((1, D_MODEL), lambda i, j: (0, 0)),
            pl.BlockSpec((tn, D_MODEL), lambda i, j: (j, 0)),
            pl.BlockSpec((GATE_ROWS, D_MODEL), lambda i, j: (0, 0)),
        ],
        out_specs=(
            pl.BlockSpec((tm, tn), lambda i, j: (i, j)),
            pl.BlockSpec((GATE_ROWS, tm), lambda i, j: (0, i)),
        ),
        scratch_shapes=[pltpu.VMEM((tm, D_MODEL), BF16)],
        compiler_params=_params(("parallel", "arbitrary")),
        name="inproj",
    )(x, norm_w, w_big, w_gate_t)


def _side_cast_specs(weights, layer, steps, step_of):
    in_specs, out_specs, out_shapes = [], [], []
    for w in weights:
        _, rows, cols = w.shape
        band = rows // steps
        assert rows % steps == 0 and band % (2 * SUBLANES) == 0
        in_specs.append(pl.BlockSpec((None, band, cols), lambda *g: (layer, step_of(*g), 0)))
        out_specs.append(pl.BlockSpec((band, cols), lambda *g: (step_of(*g), 0)))
        out_shapes.append(jax.ShapeDtypeStruct((rows, cols), BF16))
    return in_specs, out_specs, out_shapes


def _lane_cumsum(v, seg):
    pos = lax.broadcasted_iota(jnp.int32, v.shape, 1) & (seg - 1)
    shift = 1
    while shift < seg:
        v = v + jnp.where(pos >= shift, pltpu.roll(v, shift, 1), 0.0)
        shift *= 2
    return v


def _lane_cummax(v, seg):
    pos = lax.broadcasted_iota(jnp.int32, v.shape, 1) & (seg - 1)
    shift = 1
    while shift < seg:
        v = jnp.maximum(v, jnp.where(pos >= shift, pltpu.roll(v, shift, 1), -jnp.inf))
        shift *= 2
    return v


def _gates_body(gt_ref, bias_ref, alog_ref, out_ref, split_ref):
    seq = gt_ref.shape[1]
    first4 = lax.broadcasted_iota(jnp.int32, (SUBLANES, seq), 0) < 4
    z = gt_ref[0:8, :] + bias_ref[0:8, :]
    z = GATE_SOFTCAP * jnp.tanh(z / GATE_SOFTCAP)
    b_cum = _lane_cumsum(_log_sigmoid(z), CHUNK)
    out_ref[0:8, :] = jnp.where(first4, z, b_cum)
    d = z - pltpu.roll(b_cum, 4, 0)
    out_ref[24:32, :] = jnp.where(first4, d, pltpu.roll(_lane_cummax(d, CHUNK), 4, 0))

    raw = gt_ref[16:24, :]
    decay = _lane_cumsum(-jnp.exp(alog_ref[16:24, :]) * _softplus(raw + bias_ref[16:24, :]), CHUNK)
    out_ref[16:24, :] = jnp.where(first4, decay, _sigmoid(raw))

    fox = _lane_cumsum(LOG2E * _log_sigmoid(gt_ref[8:16, :] + bias_ref[8:16, :]), LANES)
    carry = jnp.zeros((SUBLANES, 1), F32)
    for blk in range(seq // LANES):
        lanes = slice(blk * LANES, (blk + 1) * LANES)
        tile = fox[:, lanes] + carry
        out_ref[8:16, lanes] = tile
        carry = tile[:, LANES - 1:LANES]

    value = out_ref[...]
    hi = value.astype(BF16)
    rest = value - hi.astype(F32)
    mid = rest.astype(BF16)
    split_ref[0:GATE_ROWS, :] = hi
    split_ref[GATE_ROWS:2 * GATE_ROWS, :] = mid
    split_ref[2 * GATE_ROWS:3 * GATE_ROWS, :] = (rest - mid.astype(F32)).astype(BF16)
    split_ref[3 * GATE_ROWS:, :] = jnp.zeros((LANES - 3 * GATE_ROWS, seq), BF16)


def _gates(gt, bias_col, alog_col, *, batch, seq):
    return pl.pallas_call(
        _gates_body,
        out_shape=(jax.ShapeDtypeStruct(gt.shape, F32), jax.ShapeDtypeStruct((LANES, gt.shape[1]), BF16)),
        grid=(batch,),
        in_specs=[
            pl.BlockSpec((GATE_ROWS, seq), lambda b: (0, b)),
            pl.BlockSpec((GATE_ROWS, 1), lambda b: (0, 0)),
            pl.BlockSpec((GATE_ROWS, 1), lambda b: (0, 0)),
        ],
        out_specs=(pl.BlockSpec((GATE_ROWS, seq), lambda b: (0, b)), pl.BlockSpec((LANES, seq), lambda b: (0, b))),
        compiler_params=_params(("parallel",)),
        name="gates",
    )(gt, bias_col, alog_col)


def _mlstm_init(ct_scr, m_scr):
    ct_scr[...] = jnp.zeros(ct_scr.shape, F32)
    m_scr[...] = jnp.zeros(m_scr.shape, F32)


def _mlstm_step(q_ref, k_ref, v_ref, o_ref, gb_ref, gd_ref, gc3_ref, sel_ref, mean_ref, nw_ref, y_ref,
                ct_scr, m_scr):
    heads = range(MLSTM_HEADS)
    causal = (lax.broadcasted_iota(jnp.int32, (CHUNK, CHUNK), 0)
              >= lax.broadcasted_iota(jnp.int32, (CHUNK, CHUNK), 1))
    ones = jnp.ones((CHUNK, HEAD_DIM), BF16)
    cts = [ct_scr[h] for h in heads]
    ms = [m_scr[h:h + 1, 0:1] for h in heads]
    for r in range(q_ref.shape[0] // CHUNK):
        rows = slice(r * CHUNK, (r + 1) * CHUNK)
        hcols = [slice(h * HEAD_DIM, (h + 1) * HEAD_DIM) for h in heads]
        q = [q_ref[rows, c] for c in hcols]
        k = [k_ref[rows, c] for c in hcols]
        v_aug = [jnp.concatenate([v_ref[rows, c], ones], axis=1) for c in hcols]
        cols_rep = jnp.dot(gc3_ref[rows, :], sel_ref[...], preferred_element_type=F32)
        rep = lambda h, j: cols_rep[:, (3 * h + j) * LANES:(3 * h + j + 1) * LANES]
        a_col = [rep(h, 0) for h in heads]
        d_col = [rep(h, 1) for h in heads]
        b_col = [rep(h, 2) for h in heads]
        d_row = [gd_ref[h:h + 1, rows] for h in heads]
        b_last = [gb_ref[4 + h:5 + h, rows][:, CHUNK - 1:CHUNK] for h in heads]
        a_last = [gd_ref[4 + h:5 + h, rows][:, CHUNK - 1:CHUNK] for h in heads]

        qk = [lax.dot_general(q[h], k[h], NT_DIMS, preferred_element_type=F32) for h in heads]
        q_state = [lax.dot_general(q[h], cts[h].astype(BF16), NT_DIMS, preferred_element_type=F32)
                   for h in heads]
        m_rel = [jnp.maximum(ms[h], a_col[h]) for h in heads]
        p = [jnp.exp(jnp.where(causal, d_row[h] - m_rel[h], -jnp.inf)) * (qk[h] * QK_SCALE) for h in heads]
        intra = [jnp.dot(p[h].astype(BF16), v_aug[h], preferred_element_type=F32) for h in heads]
        for h in heads:
            w_inter = jnp.exp(ms[h] - m_rel[h])
            both = jnp.concatenate([w_inter, w_inter], axis=1) * q_state[h] + intra[h]
            floor = jnp.exp(-(b_col[h] + m_rel[h]))
            h_out = both[:, :HEAD_DIM] / jnp.maximum(jnp.abs(both[:, HEAD_DIM:]), floor)
            mean = jnp.dot(h_out.astype(BF16), mean_ref[...], preferred_element_type=F32)
            centered = h_out - mean
            var = jnp.dot((centered * centered).astype(BF16), mean_ref[...], preferred_element_type=F32)
            gate = _sigmoid(o_ref[rows, hcols[h]].astype(F32))
            y_ref[rows, hcols[h]] = (centered * lax.rsqrt(var + NORM_EPS) * nw_ref[:, hcols[h]]
                                     * gate).astype(BF16)

        m_end = [jnp.maximum(ms[h], a_last[h]) for h in heads]
        kw = [(k[h].astype(F32) * (jnp.exp(d_col[h] - m_end[h]) * QK_SCALE)).astype(BF16) for h in heads]
        cts = [jnp.exp(ms[h] - m_end[h]) * cts[h]
               + lax.dot_general(v_aug[h], kw[h], TN_DIMS, preferred_element_type=F32) for h in heads]
        ms = [b_last[h] + m_end[h] for h in heads]
    for h in heads:
        ct_scr[h] = cts[h]
        m_scr[h:h + 1, :] = jnp.broadcast_to(ms[h], (1, LANES))


def _one_hot_columns(gate_cols):
    sel = np.zeros((LANES, LANES * len(gate_cols)), np.float32)
    for j, col in enumerate(gate_cols):
        for part in range(3):
            sel[part * GATE_ROWS + col, j * LANES:(j + 1) * LANES] = 1.0
    return jnp.asarray(sel, BF16)


def _fox_body(q_ref, k_ref, v_ref, g3_ref, c_ref, nw_ref, y_ref, acc_scr, m_scr, l_scr, qa_scr, vt_scr):
    pair = pl.program_id(1)
    qi = pl.program_id(2)
    tq = q_ref.shape[0]
    heads = range(FOX_PAIR)
    hcols = [slice(e * HEAD_DIM, (e + 1) * HEAD_DIM) for e in heads]

    @pl.when(qi == 0)
    def _():
        for e in heads:
            for j in range(vt_scr.shape[1]):
                vt_scr[e, j] = v_ref[j * tq:(j + 1) * tq, hcols[e]].T

    lane = lax.broadcasted_iota(jnp.int32, (tq, LANES), 1)
    for e in heads:
        fox_row = 2 * MLSTM_HEADS + FOX_PAIR * pair + e
        split_lane = (lane == fox_row) | (lane == GATE_ROWS + fox_row) | (lane == 2 * GATE_ROWS + fox_row)
        qa_scr[e, :, 0:HEAD_DIM] = (q_ref[:, hcols[e]].astype(F32) * (QK_SCALE * LOG2E)).astype(BF16)
        qa_scr[e, :, HEAD_DIM:] = jnp.where(split_lane, -1.0, 0.0).astype(BF16)
    c_q = [c_ref[e, qi] for e in heads]
    m_scr[...] = jnp.full(m_scr.shape, -jnp.inf, F32)
    l_scr[...] = jnp.zeros(l_scr.shape, F32)
    acc_scr[...] = jnp.zeros(acc_scr.shape, F32)

    def logits(kb, diagonal):
        base = pl.multiple_of(kb * tq, tq)
        gates = g3_ref[pl.ds(base, tq), :]
        zs = []
        for e in heads:
            keys = jnp.concatenate([k_ref[pl.ds(base, tq), hcols[e]], gates], axis=1)
            z = lax.dot_general(keys, qa_scr[e], NT_DIMS, preferred_element_type=F32)
            if diagonal:
                visible = (lax.broadcasted_iota(jnp.int32, (tq, tq), 0)
                           <= lax.broadcasted_iota(jnp.int32, (tq, tq), 1))
                z = jnp.where(visible, z, -jnp.inf)
            zs.append(z)
        return zs

    def update(kb, zs):
        ps, alphas = [], []
        for e in heads:
            m_prev = m_scr[e, 0:1, :]
            m_new = jnp.maximum(m_prev, jnp.max(zs[e], axis=0, keepdims=True) + c_q[e])
            alpha = jnp.exp2(m_prev - m_new)
            p = jnp.exp2(zs[e] - (m_new - c_q[e]))
            l_scr[e, 0:1, :] = alpha * l_scr[e, 0:1, :] + jnp.sum(p, axis=0, keepdims=True)
            m_scr[e, 0:1, :] = m_new
            ps.append(p.astype(BF16))
            alphas.append(alpha)
        for e in heads:
            acc_scr[e] = alphas[e] * acc_scr[e] + jnp.dot(vt_scr[e, kb], ps[e], preferred_element_type=F32)

    def blocks(first, count):
        zs = [logits(first + j, False) for j in range(count)]
        for j in range(count):
            update(first + j, zs[j])

    def group(g, carry):
        blocks(g * FOX_GROUP, FOX_GROUP)
        return carry

    lax.fori_loop(0, qi // FOX_GROUP, group, 0)
    done = (qi // FOX_GROUP) * FOX_GROUP
    size = FOX_GROUP // 2
    while size >= 1:
        take = (qi & size) != 0
        pl.when(take)(functools.partial(blocks, done, size))
        done = done + jnp.where(take, size, 0)
        size //= 2
    update(qi, logits(qi, True))
    for e in heads:
        out = (acc_scr[e] / l_scr[e, 0:1, :]).T
        y_ref[:, hcols[e]] = (_rms(out) * nw_ref[:, hcols[e]]).astype(BF16)


def _fox(proj, g3, c_rows, norm_w, *, batch, seq):
    m = proj.shape[0]
    nq = seq // FOX_BLOCK
    width = FOX_PAIR * HEAD_DIM
    q_col = (4 * MLSTM_W) // width
    k_col = q_col + FOX_W // width
    v_col = k_col + FOX_W // width
    return pl.pallas_call(
        _fox_body,
        out_shape=jax.ShapeDtypeStruct((m, FOX_W), BF16),
        grid=(batch, FOX_HEADS // FOX_PAIR, nq),
        in_specs=[
            pl.BlockSpec((FOX_BLOCK, width), lambda b, h, i: (b * nq + i, q_col + h)),
            pl.BlockSpec((seq, width), lambda b, h, i: (b, k_col + h)),
            pl.BlockSpec((seq, width), lambda b, h, i: (b, v_col + h)),
            pl.BlockSpec((seq, LANES), lambda b, h, i: (b, 0)),
            pl.BlockSpec((FOX_PAIR, nq, 1, FOX_BLOCK), lambda b, h, i: (h, b, 0, 0)),
            pl.BlockSpec((1, width), lambda b, h, i: (0, h)),
        ],
        out_specs=pl.BlockSpec((FOX_BLOCK, width), lambda b, h, i: (b * nq + i, h)),
        scratch_shapes=[
            pltpu.VMEM((FOX_PAIR, HEAD_DIM, FOX_BLOCK), F32),
            pltpu.VMEM((FOX_PAIR, SUBLANES, FOX_BLOCK), F32),
            pltpu.VMEM((FOX_PAIR, SUBLANES, FOX_BLOCK), F32),
            pltpu.VMEM((FOX_PAIR, FOX_BLOCK, 2 * HEAD_DIM), BF16),
            pltpu.VMEM((FOX_PAIR, nq, HEAD_DIM, FOX_BLOCK), BF16),
        ],
        compiler_params=_params(("parallel", "parallel", "arbitrary")),
        name="fox",
    )(proj, proj, proj, g3, c_rows, norm_w)


def _mm(x, y):
    return jnp.dot(x.astype(BF16), y.astype(BF16), preferred_element_type=F32)


def _unit_lower_inverses(a_mats, mask_ref):
    eye = mask_ref[MASK_INCL] - mask_ref[MASK_STRICT]
    negs = [-(a * mask_ref[MASK_DIAG8]) for a in a_mats]
    invs = [eye + n for n in negs]
    for _ in range(2):
        negs = [_mm(n, n) for n in negs]
        invs = [i + _mm(i, n) for i, n in zip(invs, negs)]
    for level in range(MASK_OFF0, MASK_COUNT):
        tmps = [_mm(a * mask_ref[level], i) for a, i in zip(a_mats, invs)]
        invs = [i - _mm(i, t) for i, t in zip(invs, tmps)]
    return invs


def _gdn_init(s_scr, ext_scr, mask_scr):
    s_scr[...] = jnp.zeros(s_scr.shape, F32)
    ext_scr[0:SUBLANES, :] = jnp.zeros((SUBLANES, 3 * GDN_W), F32)
    ri = lax.broadcasted_iota(jnp.int32, (CHUNK, CHUNK), 0)
    ci = lax.broadcasted_iota(jnp.int32, (CHUNK, CHUNK), 1)
    mask_scr[MASK_STRICT] = jnp.where(ri > ci, 1.0, 0.0)
    mask_scr[MASK_INCL] = jnp.where(ri >= ci, 1.0, 0.0)
    mask_scr[MASK_DIAG8] = jnp.where((ri >> 3) == (ci >> 3), 1.0, 0.0)
    for bits in range(3, 3 + MASK_COUNT - MASK_OFF0):
        sibling = ((ri >> (bits + 1)) == (ci >> (bits + 1))) & ((ri >> bits) != (ci >> bits))
        mask_scr[MASK_OFF0 + bits - 3] = jnp.where(sibling, 1.0, 0.0)


def _gdn_step(q_ref, k_ref, v_ref, z_ref, cw_ref, gr_ref, gc3_ref, sel_ref, ones_ref, nw_ref, y_ref,
              s_scr, ext_scr, mask_scr):
    step = q_ref.shape[0]
    halo = SUBLANES
    nchunk = step // CHUNK

    ext_scr[halo:halo + step, 0:GDN_W] = q_ref[...].astype(F32)
    ext_scr[halo:halo + step, GDN_W:2 * GDN_W] = k_ref[...].astype(F32)
    ext_scr[halo:halo + step, 2 * GDN_W:3 * GDN_W] = v_ref[...].astype(F32)
    ext = ext_scr[...]
    conv = ext * cw_ref[0:1, :]
    for j in range(1, CONV_WIDTH):
        conv = pltpu.roll(conv, 1, 0) + ext * cw_ref[j:j + 1, :]
    conv = conv[halo:, :]
    ext_scr[0:halo, :] = ext_scr[step:step + halo, :]
    qkv = conv * _sigmoid(conv)

    def row_sum(t):
        return jnp.dot(t.astype(BF16), ones_ref[...], preferred_element_type=F32)

    def l2n(t):
        return t * lax.rsqrt(row_sum(t * t) + NORM_EPS)

    cols_rep = [jnp.dot(gc3_ref[r * CHUNK:(r + 1) * CHUNK, :], sel_ref[...], preferred_element_type=F32)
                for r in range(nchunk)]

    groups = [(h, r) for h in range(GDN_HEADS) for r in range(nchunk)]
    prep = {}
    for h, r in groups:
        rows = slice(r * CHUNK, (r + 1) * CHUNK)
        q = l2n(qkv[rows, h * HEAD_DIM:(h + 1) * HEAD_DIM]) * QK_SCALE
        k = l2n(qkv[rows, GDN_W + h * HEAD_DIM:GDN_W + (h + 1) * HEAD_DIM])
        v = qkv[rows, 2 * GDN_W + h * HEAD_DIM:2 * GDN_W + (h + 1) * HEAD_DIM]
        g_row = gr_ref[h:h + 1, rows]
        g_col = cols_rep[r][:, (2 * h) * LANES:(2 * h + 1) * LANES]
        beta = cols_rep[r][:, (2 * h + 1) * LANES:(2 * h + 2) * LANES]
        incl = mask_scr[MASK_INCL]
        decay = jnp.exp((g_col - g_row) * incl) * incl
        kb = k * beta
        e_g = jnp.exp(g_col)
        g_last = g_row[:, CHUNK - 1:CHUNK]
        kk_qk = lax.dot_general(jnp.concatenate([kb, q], axis=0).astype(BF16), k.astype(BF16), NT_DIMS,
                                preferred_element_type=F32)
        prep[h, r] = dict(
            a=kk_qk[:CHUNK] * (decay * mask_scr[MASK_STRICT]),
            attn=kk_qk[CHUNK:] * decay,
            rhs=jnp.concatenate([v * beta, kb * e_g], axis=1),
            q_dec=q * e_g,
            k_end=k * jnp.exp(g_last - g_col),
            s_decay=jnp.exp(g_last),
        )
    invs = _unit_lower_inverses([prep[g]["a"] for g in groups], mask_scr)
    for g, inv in zip(groups, invs):
        w = _mm(inv, prep[g]["rhs"])
        prep[g]["w_val"], prep[g]["w_key"] = w[:, :HEAD_DIM], w[:, HEAD_DIM:]

    states = [s_scr[h] for h in range(GDN_HEADS)]
    for r in range(nchunk):
        rows = slice(r * CHUNK, (r + 1) * CHUNK)
        ks = [_mm(jnp.concatenate([prep[h, r]["w_key"], prep[h, r]["q_dec"]], axis=0), states[h])
              for h in range(GDN_HEADS)]
        us = [prep[h, r]["w_val"] - ks[h][:CHUNK] for h in range(GDN_HEADS)]
        outs = [ks[h][CHUNK:] + _mm(prep[h, r]["attn"], us[h]) for h in range(GDN_HEADS)]
        states = [prep[h, r]["s_decay"] * states[h]
                  + lax.dot_general(prep[h, r]["k_end"].astype(BF16), us[h].astype(BF16), TN_DIMS,
                                    preferred_element_type=F32) for h in range(GDN_HEADS)]
        for h in range(GDN_HEADS):
            cols = slice(h * HEAD_DIM, (h + 1) * HEAD_DIM)
            gate = z_ref[rows, cols].astype(F32)
            normed = outs[h] * lax.rsqrt(row_sum(outs[h] * outs[h]) * (1.0 / HEAD_DIM) + NORM_EPS)
            y_ref[rows, cols] = (normed * nw_ref[:, cols] * (gate * _sigmoid(gate))).astype(BF16)
    for h in range(GDN_HEADS):
        s_scr[h] = states[h]


N_MLSTM_INPUTS = 10


N_GDN_INPUTS = 9


def _recurrent_body(*refs, n_side):
    m_in = refs[:N_MLSTM_INPUTS]
    g_in = refs[N_MLSTM_INPUTS:N_MLSTM_INPUTS + N_GDN_INPUTS]
    side_in = refs[N_MLSTM_INPUTS + N_GDN_INPUTS:N_MLSTM_INPUTS + N_GDN_INPUTS + n_side]
    outs = refs[N_MLSTM_INPUTS + N_GDN_INPUTS + n_side:-5]
    ym_ref, yg_ref, side_out = outs[0], outs[1], outs[2:]
    ct_scr, m_scr, s_scr, ext_scr, mask_scr = refs[-5:]
    gc3_ref = m_in[6]

    @pl.when(pl.program_id(1) == 0)
    def _():
        _mlstm_init(ct_scr, m_scr)
        _gdn_init(s_scr, ext_scr, mask_scr)

    for src, dst in zip(side_in, side_out):
        dst[...] = src[...].astype(BF16)
    _mlstm_step(*m_in, ym_ref, ct_scr, m_scr)
    _gdn_step(*g_in[:6], gc3_ref, *g_in[6:], yg_ref, s_scr, ext_scr, mask_scr)


def _recurrent(proj, conv_w, gr, gc3, mlstm_norm_w, gdn_norm_w, side_weights, layer, *, batch, seq):
    m = proj.shape[0]
    ns = seq // REC_STEP
    row = lambda b, s: b * ns + s
    side_in_specs, side_out_specs, side_out_shapes = _side_cast_specs(side_weights, layer, batch * ns, row)
    m_sel = _one_hot_columns([c for h in range(MLSTM_HEADS) for c in (28 + h, 24 + h, 4 + h)])
    mean_w = jnp.full((HEAD_DIM, HEAD_DIM), 1.0 / HEAD_DIM, BF16)
    g_sel = _one_hot_columns([c for h in range(GDN_HEADS) for c in (16 + h, 20 + h)])
    col0 = (4 * MLSTM_W + 3 * FOX_W) // GDN_W
    wide = lambda c: pl.BlockSpec((REC_STEP, MLSTM_W), lambda b, s: (row(b, s), c))
    gate_rows = lambda r: pl.BlockSpec((SUBLANES, REC_STEP), lambda b, s: (r, row(b, s)))
    whole = lambda shape: pl.BlockSpec(shape, lambda b, s: (0,) * len(shape))
    outs = pl.pallas_call(
        functools.partial(_recurrent_body, n_side=len(side_weights)),
        out_shape=(jax.ShapeDtypeStruct((m, MLSTM_W), BF16), jax.ShapeDtypeStruct((m, GDN_W), BF16),
                   *side_out_shapes),
        grid=(batch, ns),
        in_specs=[
            wide(0), wide(1), wide(2), wide(3), gate_rows(0), gate_rows(3),
            pl.BlockSpec((REC_STEP, LANES), lambda b, s: (row(b, s), 0)),
            whole(m_sel.shape), whole((HEAD_DIM, HEAD_DIM)), whole((1, MLSTM_W)),
            wide(col0), wide(col0 + 1), wide(col0 + 2), wide(col0 + 3),
            whole((CONV_WIDTH, 3 * GDN_W)), gate_rows(2),
            whole(g_sel.shape), whole((HEAD_DIM, HEAD_DIM)), whole((1, GDN_W)),
            *side_in_specs,
        ],
        out_specs=(wide(0), wide(0), *side_out_specs),
        scratch_shapes=[
            pltpu.VMEM((MLSTM_HEADS, 2 * HEAD_DIM, HEAD_DIM), F32),
            pltpu.VMEM((SUBLANES, LANES), F32),
            pltpu.VMEM((GDN_HEADS, HEAD_DIM, HEAD_DIM), F32),
            pltpu.VMEM((REC_STEP + SUBLANES, 3 * GDN_W), F32),
            pltpu.VMEM((MASK_COUNT, CHUNK, CHUNK), F32),
        ],
        compiler_params=_params(("arbitrary", "arbitrary")),
        name="recurrent",
    )(proj, proj, proj, proj, gr, gr, gc3, m_sel, mean_w, mlstm_norm_w,
      proj, proj, proj, proj, conv_w, gr, g_sel, jnp.ones((HEAD_DIM, HEAD_DIM), BF16), gdn_norm_w,
      *side_weights)
    return outs[0], outs[1], outs[2:]


def _outproj_body(ym_ref, yf_ref, yg_ref, w_ref, x_ref, *rest):
    o_ref = rest[-1] if len(rest) == 1 else rest[1]
    acc = jnp.dot(ym_ref[...], w_ref[0:MLSTM_W, :], preferred_element_type=F32)
    acc += jnp.dot(yf_ref[...], w_ref[MLSTM_W:MLSTM_W + FOX_W, :], preferred_element_type=F32)
    acc += jnp.dot(yg_ref[...], w_ref[MLSTM_W + FOX_W:, :], preferred_element_type=F32)
    o_ref[...] = x_ref[...] + acc
    if len(rest) > 1:
        w_next_ref, _, wide_ref, gate_ref = rest
        _prep_w_in_body(w_next_ref, wide_ref, gate_ref)


def _outproj(y_m, y_f, y_g, w_out, x, w_in_t=None, next_layer=None, *, tm=512):
    m = x.shape[0]
    steps = m // tm
    in_specs = [
        pl.BlockSpec((tm, MLSTM_W), lambda i: (i, 0)),
        pl.BlockSpec((tm, FOX_W), lambda i: (i, 0)),
        pl.BlockSpec((tm, GDN_W), lambda i: (i, 0)),
        pl.BlockSpec((D_MODEL, D_MODEL), lambda i: (0, 0)),
        pl.BlockSpec((tm, D_MODEL), lambda i: (i, 0)),
    ]
    out_shape = [jax.ShapeDtypeStruct((m, D_MODEL), F32)]
    out_specs = [pl.BlockSpec((tm, D_MODEL), lambda i: (i, 0))]
    operands = [y_m, y_f, y_g, w_out, x]
    if w_in_t is not None:
        _, n_in, d = w_in_t.shape
        cols = d // steps
        in_specs.append(pl.BlockSpec((None, n_in, cols), lambda i: (next_layer, 0, i)))
        out_shape += [jax.ShapeDtypeStruct((PROJ_W, d), BF16), jax.ShapeDtypeStruct((GATE_ROWS, d), BF16)]
        out_specs += [pl.BlockSpec((PROJ_W, cols), lambda i: (0, i)), pl.BlockSpec((GATE_ROWS, cols), lambda i: (0, i))]
        operands.append(w_in_t)
    outs = pl.pallas_call(
        _outproj_body,
        out_shape=tuple(out_shape),
        grid=(steps,),
        in_specs=in_specs,
        out_specs=tuple(out_specs),
        compiler_params=_params(("parallel",)),
        name="outproj",
    )(*operands)
    return outs[0], tuple(outs[1:])


def _ffn_body(x_ref, nw_ref, wg_ref, wu_ref, wd_ref, fw_ref, o_ref, h_scr, *, final_norm, row_chunk):
    j = pl.program_id(1)
    chunks = [slice(r * row_chunk, (r + 1) * row_chunk) for r in range(x_ref.shape[0] // row_chunk)]

    @pl.when(j == 0)
    def _():
        for rows in chunks:
            x = x_ref[rows, :]
            h_scr[rows, :] = (_rms(x) * nw_ref[...]).astype(BF16)
            o_ref[rows, :] = x

    h = h_scr[...]
    gate = jnp.dot(h, wg_ref[...], preferred_element_type=F32)
    up = jnp.dot(h, wu_ref[...], preferred_element_type=F32)
    act = (gate * _sigmoid(gate) * up).astype(BF16)
    o_ref[...] += jnp.dot(act, wd_ref[...], preferred_element_type=F32)

    if final_norm:
        @pl.when(j == pl.num_programs(1) - 1)
        def _():
            for rows in chunks:
                o_ref[rows, :] = _rms(o_ref[rows, :]) * fw_ref[...]


def _ffn(x, norm_w, w_gate, w_up, w_down, final_w, *, final_norm, tm=1024, tf=512):
    m = x.shape[0]
    return pl.pallas_call(
        functools.partial(_ffn_body, final_norm=final_norm, row_chunk=256),
        out_shape=jax.ShapeDtypeStruct((m, D_MODEL), F32),
        grid=(m // tm, D_FF // tf),
        in_specs=[
            pl.BlockSpec((tm, D_MODEL), lambda i, j: (i, 0)),
            pl.BlockSpec((1, D_MODEL), lambda i, j: (0, 0)),
            pl.BlockSpec((D_MODEL, tf), lambda i, j: (0, j)),
            pl.BlockSpec((D_MODEL, tf), lambda i, j: (0, j)),
            pl.BlockSpec((tf, D_MODEL), lambda i, j: (j, 0)),
            pl.BlockSpec((1, D_MODEL), lambda i, j: (0, 0)),
        ],
        out_specs=pl.BlockSpec((tm, D_MODEL), lambda i, j: (i, 0)),
        scratch_shapes=[pltpu.VMEM((tm, D_MODEL), BF16)],
        compiler_params=_params(("parallel", "arbitrary")),
        name="ffn",
    )(x, norm_w, w_gate, w_up, w_down, final_w)


def _prep_w_in_body(w_ref, wide_ref, gate_ref):
    o_mi = 4 * MLSTM_W
    o_fq = o_mi + 2 * MLSTM_HEADS
    o_ff = o_fq + 3 * FOX_W
    o_gq = o_ff + FOX_HEADS
    o_ga = o_gq + 4 * GDN_W
    wide_ref[0:o_mi, :] = w_ref[0:o_mi, :].astype(BF16)
    wide_ref[o_mi:o_mi + 3 * FOX_W, :] = w_ref[o_fq:o_ff, :].astype(BF16)
    wide_ref[o_mi + 3 * FOX_W:, :] = w_ref[o_gq:o_ga, :].astype(BF16)
    gate = jnp.concatenate([w_ref[o_mi:o_fq, :], w_ref[o_ff:o_gq, :], w_ref[o_ga:, :],
                            jnp.zeros((SUBLANES, w_ref.shape[1]), F32)], axis=0)
    gate_ref[...] = gate.astype(BF16)


def _prep_w_in(w_in_t, layer, *, cols=256):
    _, n_in, d = w_in_t.shape
    return pl.pallas_call(
        _prep_w_in_body,
        out_shape=(jax.ShapeDtypeStruct((PROJ_W, d), BF16), jax.ShapeDtypeStruct((GATE_ROWS, d), BF16)),
        grid=(d // cols,),
        in_specs=[pl.BlockSpec((None, n_in, cols), lambda i: (layer, 0, i))],
        out_specs=(pl.BlockSpec((PROJ_W, cols), lambda i: (0, i)), pl.BlockSpec((GATE_ROWS, cols), lambda i: (0, i))),
        compiler_params=_params(("parallel",)),
        name="prep_w_in",
    )(w_in_t)


def kernel(x, mix_norm_w, w_in, mlstm_i_bias, mlstm_f_bias, fox_f_bias, gdn_conv_w, gdn_a_log, gdn_dt_bias,
           mlstm_out_norm_w, fox_out_norm_w, gdn_out_norm_w, w_out, ffn_norm_w, w_gate, w_up, w_down,
           final_norm_w):
    batch, seq, d_model = x.shape
    depth = w_in.shape[0]
    assert d_model == D_MODEL and seq % FOX_BLOCK == 0 and seq % REC_STEP == 0 and MLSTM_W == GDN_W
    xf = x.reshape(batch * seq, d_model)
    zeros4 = jnp.zeros((GDN_HEADS,), F32)
    w_in_t = jnp.swapaxes(w_in, 1, 2)
    side_weights = [w_gate, w_up, w_down, w_out]
    w_wide, w_gate_t = _prep_w_in(w_in_t, 0)
    for l in range(depth):
        bias_col = jnp.pad(jnp.concatenate([mlstm_i_bias[l], mlstm_f_bias[l], fox_f_bias[l], gdn_dt_bias[l],
                                            zeros4]), (0, 8)).reshape(GATE_ROWS, 1)
        alog_col = jnp.pad(gdn_a_log[l], (16, 12)).reshape(GATE_ROWS, 1)

        proj, gt = _inproj(xf, mix_norm_w[l].reshape(1, -1), w_wide, w_gate_t)
        gr, gr3 = _gates(gt, bias_col, alog_col, batch=batch, seq=seq)
        gc3 = gr3.T
        c_rows = gr[8:16].reshape(FOX_HEADS, batch * (seq // FOX_BLOCK), 1, FOX_BLOCK)

        y_m, y_g, (w_gate_b, w_up_b, w_down_b, w_out_b) = _recurrent(
            proj, gdn_conv_w[l], gr, gc3, mlstm_out_norm_w[l].reshape(1, -1), gdn_out_norm_w[l].reshape(1, -1),
            side_weights, l, batch=batch, seq=seq)
        y_f = _fox(proj, gc3, c_rows, fox_out_norm_w[l].reshape(1, -1), batch=batch, seq=seq)
        if l + 1 < depth:
            xf, (w_wide, w_gate_t) = _outproj(y_m, y_f, y_g, w_out_b, xf, w_in_t, l + 1)
        else:
            xf, _ = _outproj(y_m, y_f, y_g, w_out_b, xf)
        xf = _ffn(xf, ffn_norm_w[l].reshape(1, -1), w_gate_b, w_up_b, w_down_b, final_norm_w.reshape(1, -1),
                  final_norm=(l == depth - 1))
    return xf.reshape(batch, seq, d_model)
```

```python
import functools

import jax
import jax.numpy as jnp
import numpy as np
from jax import lax
from jax.experimental import pallas as pl
from jax.experimental.pallas import tpu as pltpu

F32 = jnp.float32
BF16 = jnp.bfloat16

D_MODEL = 2048
HEAD_DIM = 128
MLSTM_HEADS = 4
FOX_HEADS = 8
GDN_HEADS = 4
MLSTM_W = MLSTM_HEADS * HEAD_DIM
FOX_W = FOX_HEADS * HEAD_DIM
GDN_W = GDN_HEADS * HEAD_DIM
D_FF = 5632
CONV_WIDTH = 4
GATE_SOFTCAP = 15.0
NORM_EPS = 1e-6
QK_SCALE = HEAD_DIM ** -0.5
LOG2E = 1.4426950408889634

PROJ_W = 4 * MLSTM_W + 3 * FOX_W + 4 * GDN_W
GDN_Z_COL = 4 * MLSTM_W + 3 * FOX_W
GDN_QKV_COL = GDN_Z_COL + GDN_W
GATE_ROWS = 32

LANES = 128
SUBLANES = 8
V7X_VMEM_BYTES = 64 * 1024 * 1024
VMEM_LIMIT = 56 * 1024 * 1024

CHUNK = 128
FOX_BLOCK = 512
FOX_GROUP = 4
FOX_PAIR = 4
REC_STEP = 512

MASK_STRICT, MASK_INCL, MASK_DIAG8, MASK_OFF0 = 0, 1, 2, 3
MASK_COUNT = MASK_OFF0 + 4

NT_DIMS = (((1,), (1,)), ((), ()))
TN_DIMS = (((0,), (0,)), ((), ()))


def _params(sem):
    return pltpu.CompilerParams(dimension_semantics=sem, vmem_limit_bytes=VMEM_LIMIT)


def _sigmoid(z):
    return 1.0 / (1.0 + jnp.exp(-z))


def _log_sigmoid(z):
    return jnp.minimum(z, 0.0) - jnp.log1p(jnp.exp(-jnp.abs(z)))


def _softplus(z):
    return jnp.maximum(z, 0.0) + jnp.log1p(jnp.exp(-jnp.abs(z)))


def _rms(t):
    return t * lax.rsqrt(jnp.mean(t * t, axis=-1, keepdims=True) + NORM_EPS)


def _inproj_body(x_ref, nw_ref, w_ref, wg_ref, cw_ref, proj_ref, gt_ref, h_scr, halo_scr, *,
                 row_chunk, tiles_per_seq, conv_col):
    i, j = pl.program_id(0), pl.program_id(1)
    last = pl.num_programs(1) - 1
    halo = halo_scr.shape[0]

    @pl.when(j == 0)
    def _():
        for r in range(x_ref.shape[0] // row_chunk):
            rows = slice(r * row_chunk, (r + 1) * row_chunk)
            h_scr[rows, :] = (_rms(x_ref[rows, :]) * nw_ref[...]).astype(BF16)
        gt_ref[...] = lax.dot_general(wg_ref[...], h_scr[...], NT_DIMS, preferred_element_type=F32)

    @pl.when(j < last)
    def _():
        proj_ref[...] = lax.dot_general(h_scr[...], w_ref[...], NT_DIMS, preferred_element_type=F32).astype(BF16)

    @pl.when((j == last) & (lax.rem(i, tiles_per_seq) == 0))
    def _():
        halo_scr[...] = jnp.zeros(halo_scr.shape, F32)

    @pl.when(j == last)
    def _():
        acc = lax.dot_general(h_scr[...], w_ref[...], NT_DIMS, preferred_element_type=F32)
        proj_ref[:, :conv_col] = acc[:, :conv_col].astype(BF16)
        pre = acc[:, conv_col:]
        ext = jnp.concatenate([halo_scr[...], pre], axis=0)
        conv = ext * cw_ref[0:1, :]
        for tap in range(1, CONV_WIDTH):
            conv = pltpu.roll(conv, 1, 0) + ext * cw_ref[tap:tap + 1, :]
        conv = conv[halo:, :]
        halo_scr[...] = pre[pre.shape[0] - halo:, :]
        proj_ref[:, conv_col:] = (conv * _sigmoid(conv)).astype(BF16)


def _inproj(x, norm_w, w_big, w_gate_t, conv_w, *, seq, tm=1024, tn=1792):
    m = x.shape[0]
    nj = PROJ_W // tn
    assert seq % tm == 0 and GDN_QKV_COL >= (nj - 1) * tn
    return pl.pallas_call(
        functools.partial(_inproj_body, row_chunk=256, tiles_per_seq=seq // tm,
                          conv_col=GDN_QKV_COL - (nj - 1) * tn),
        out_shape=(jax.ShapeDtypeStruct((m, PROJ_W), BF16), jax.ShapeDtypeStruct((GATE_ROWS, m), F32)),
        grid=(m // tm, nj),
        in_specs=[
            pl.BlockSpec((tm, D_MODEL), lambda i, j: (i, 0)),
            pl.BlockSpec((1, D_MODEL), lambda i, j: (0, 0)),
            pl.BlockSpec((tn, D_MODEL), lambda i, j: (j, 0)),
            pl.BlockSpec((GATE_ROWS, D_MODEL), lambda i, j: (0, 0)),
            pl.BlockSpec((CONV_WIDTH, 3 * GDN_W), lambda i, j: (0, 0)),
        ],
        out_specs=(
            pl.BlockSpec((tm, tn), lambda i, j: (i, j)),
            pl.BlockSpec((GATE_ROWS, tm), lambda i, j: (0, i)),
        ),
        scratch_shapes=[pltpu.VMEM((tm, D_MODEL), BF16), pltpu.VMEM((SUBLANES, 3 * GDN_W), F32)],
        compiler_params=_params(("arbitrary", "arbitrary")),
        name="inproj",
    )(x, norm_w, w_big, w_gate_t, conv_w)


def _side_cast_specs(weights, layer, steps, step_of):
    in_specs, out_specs, out_shapes = [], [], []
    for w in weights:
        _, rows, cols = w.shape
        band = rows // steps
        assert rows % steps == 0 and band % (2 * SUBLANES) == 0
        in_specs.append(pl.BlockSpec((None, band, cols), lambda *g: (layer, step_of(*g), 0)))
        out_specs.append(pl.BlockSpec((band, cols), lambda *g: (step_of(*g), 0)))
        out_shapes.append(jax.ShapeDtypeStruct((rows, cols), BF16))
    return in_specs, out_specs, out_shapes


def _lane_cumsum(v, seg):
    pos = lax.broadcasted_iota(jnp.int32, v.shape, 1) & (seg - 1)
    shift = 1
    while shift < seg:
        v = v + jnp.where(pos >= shift, pltpu.roll(v, shift, 1), 0.0)
        shift *= 2
    return v


def _lane_cummax(v, seg):
    pos = lax.broadcasted_iota(jnp.int32, v.shape, 1) & (seg - 1)
    shift = 1
    while shift < seg:
        v = jnp.maximum(v, jnp.where(pos >= shift, pltpu.roll(v, shift, 1), -jnp.inf))
        shift *= 2
    return v


def _gates_body(gt_ref, bias_ref, alog_ref, out_ref, split_ref):
    seq = gt_ref.shape[1]
    first4 = lax.broadcasted_iota(jnp.int32, (SUBLANES, seq), 0) < 4
    z = gt_ref[0:8, :] + bias_ref[0:8, :]
    z = GATE_SOFTCAP * jnp.tanh(z / GATE_SOFTCAP)
    b_cum = _lane_cumsum(_log_sigmoid(z), CHUNK)
    out_ref[0:8, :] = jnp.where(first4, z, b_cum)
    d = z - pltpu.roll(b_cum, 4, 0)
    out_ref[24:32, :] = jnp.where(first4, d, pltpu.roll(_lane_cummax(d, CHUNK), 4, 0))

    raw = gt_ref[16:24, :]
    decay = _lane_cumsum(-jnp.exp(alog_ref[16:24, :]) * _softplus(raw + bias_ref[16:24, :]), CHUNK)
    out_ref[16:24, :] = jnp.where(first4, decay, _sigmoid(raw))

    fox = _lane_cumsum(LOG2E * _log_sigmoid(gt_ref[8:16, :] + bias_ref[8:16, :]), LANES)
    carry = jnp.zeros((SUBLANES, 1), F32)
    for blk in range(seq // LANES):
        lanes = slice(blk * LANES, (blk + 1) * LANES)
        tile = fox[:, lanes] + carry
        out_ref[8:16, lanes] = tile
        carry = tile[:, LANES - 1:LANES]

    value = out_ref[...]
    hi = value.astype(BF16)
    rest = value - hi.astype(F32)
    mid = rest.astype(BF16)
    split_ref[0:GATE_ROWS, :] = hi
    split_ref[GATE_ROWS:2 * GATE_ROWS, :] = mid
    split_ref[2 * GATE_ROWS:3 * GATE_ROWS, :] = (rest - mid.astype(F32)).astype(BF16)
    split_ref[3 * GATE_ROWS:, :] = jnp.zeros((LANES - 3 * GATE_ROWS, seq), BF16)


def _gates(gt, bias_col, alog_col, *, batch, seq):
    return pl.pallas_call(
        _gates_body,
        out_shape=(jax.ShapeDtypeStruct(gt.shape, F32), jax.ShapeDtypeStruct((LANES, gt.shape[1]), BF16)),
        grid=(batch,),
        in_specs=[
            pl.BlockSpec((GATE_ROWS, seq), lambda b: (0, b)),
            pl.BlockSpec((GATE_ROWS, 1), lambda b: (0, 0)),
            pl.BlockSpec((GATE_ROWS, 1), lambda b: (0, 0)),
        ],
        out_specs=(pl.BlockSpec((GATE_ROWS, seq), lambda b: (0, b)), pl.BlockSpec((LANES, seq), lambda b: (0, b))),
        compiler_params=_params(("parallel",)),
        name="gates",
    )(gt, bias_col, alog_col)


def _mlstm_init(ct_scr, m_scr):
    ct_scr[...] = jnp.zeros(ct_scr.shape, F32)
    m_scr[...] = jnp.zeros(m_scr.shape, F32)


def _mlstm_step(q_ref, k_ref, v_ref, o_ref, gb_ref, gd_ref, gc3_ref, sel_ref, mean_ref, nw_ref, y_ref,
                ct_scr, m_scr):
    heads = range(MLSTM_HEADS)
    causal = (lax.broadcasted_iota(jnp.int32, (CHUNK, CHUNK), 0)
              >= lax.broadcasted_iota(jnp.int32, (CHUNK, CHUNK), 1))
    ones = jnp.ones((CHUNK, HEAD_DIM), BF16)
    cts = [ct_scr[h] for h in heads]
    ms = [m_scr[h:h + 1, 0:1] for h in heads]
    for r in range(q_ref.shape[0] // CHUNK):
        rows = slice(r * CHUNK, (r + 1) * CHUNK)
        hcols = [slice(h * HEAD_DIM, (h + 1) * HEAD_DIM) for h in heads]
        q = [q_ref[rows, c] for c in hcols]
        k = [k_ref[rows, c] for c in hcols]
        v_aug = [jnp.concatenate([v_ref[rows, c], ones], axis=1) for c in hcols]
        cols_rep = jnp.dot(gc3_ref[rows, :], sel_ref[...], preferred_element_type=F32)
        rep = lambda h, j: cols_rep[:, (3 * h + j) * LANES:(3 * h + j + 1) * LANES]
        a_col = [rep(h, 0) for h in heads]
        d_col = [rep(h, 1) for h in heads]
        b_col = [rep(h, 2) for h in heads]
        d_row = [gd_ref[h:h + 1, rows] for h in heads]
        b_last = [gb_ref[4 + h:5 + h, rows][:, CHUNK - 1:CHUNK] for h in heads]
        a_last = [gd_ref[4 + h:5 + h, rows][:, CHUNK - 1:CHUNK] for h in heads]

        qk = [lax.dot_general(q[h], k[h], NT_DIMS, preferred_element_type=F32) for h in heads]
        q_state = [lax.dot_general(q[h], cts[h].astype(BF16), NT_DIMS, preferred_element_type=F32)
                   for h in heads]
        m_rel = [jnp.maximum(ms[h], a_col[h]) for h in heads]
        p = [jnp.exp(jnp.where(causal, d_row[h] - m_rel[h], -jnp.inf)) * (qk[h] * QK_SCALE) for h in heads]
        intra = [jnp.dot(p[h].astype(BF16), v_aug[h], preferred_element_type=F32) for h in heads]
        for h in heads:
            w_inter = jnp.exp(ms[h] - m_rel[h])
            both = jnp.concatenate([w_inter, w_inter], axis=1) * q_state[h] + intra[h]
            floor = jnp.exp(-(b_col[h] + m_rel[h]))
            h_out = both[:, :HEAD_DIM] / jnp.maximum(jnp.abs(both[:, HEAD_DIM:]), floor)
            mean = jnp.dot(h_out.astype(BF16), mean_ref[...], preferred_element_type=F32)
            centered = h_out - mean
            var = jnp.dot((centered * centered).astype(BF16), mean_ref[...], preferred_element_type=F32)
            gate = _sigmoid(o_ref[rows, hcols[h]].astype(F32))
            y_ref[rows, hcols[h]] = (centered * lax.rsqrt(var + NORM_EPS) * nw_ref[:, hcols[h]]
                                     * gate).astype(BF16)

        m_end = [jnp.maximum(ms[h], a_last[h]) for h in heads]
        kw = [(k[h].astype(F32) * (jnp.exp(d_col[h] - m_end[h]) * QK_SCALE)).astype(BF16) for h in heads]
        cts = [jnp.exp(ms[h] - m_end[h]) * cts[h]
               + lax.dot_general(v_aug[h], kw[h], TN_DIMS, preferred_element_type=F32) for h in heads]
        ms = [b_last[h] + m_end[h] for h in heads]
    for h in heads:
        ct_scr[h] = cts[h]
        m_scr[h:h + 1, :] = jnp.broadcast_to(ms[h], (1, LANES))


def _one_hot_columns(gate_cols):
    sel = np.zeros((LANES, LANES * len(gate_cols)), np.float32)
    for j, col in enumerate(gate_cols):
        for part in range(3):
            sel[part * GATE_ROWS + col, j * LANES:(j + 1) * LANES] = 1.0
    return jnp.asarray(sel, BF16)


def _fox_body(q_ref, k_ref, v_ref, g3_ref, c_ref, nw_ref, y_ref, acc_scr, m_scr, l_scr, qa_scr, vt_scr):
    pair = pl.program_id(1)
    qi = pl.program_id(2)
    tq = q_ref.shape[0]
    heads = range(FOX_PAIR)
    hcols = [slice(e * HEAD_DIM, (e + 1) * HEAD_DIM) for e in heads]

    @pl.when(qi == 0)
    def _():
        for e in heads:
            for j in range(vt_scr.shape[1]):
                vt_scr[e, j] = v_ref[j * tq:(j + 1) * tq, hcols[e]].T

    lane = lax.broadcasted_iota(jnp.int32, (tq, LANES), 1)
    for e in heads:
        fox_row = 2 * MLSTM_HEADS + FOX_PAIR * pair + e
        split_lane = (lane == fox_row) | (lane == GATE_ROWS + fox_row) | (lane == 2 * GATE_ROWS + fox_row)
        qa_scr[e, :, 0:HEAD_DIM] = (q_ref[:, hcols[e]].astype(F32) * (QK_SCALE * LOG2E)).astype(BF16)
        qa_scr[e, :, HEAD_DIM:] = jnp.where(split_lane, -1.0, 0.0).astype(BF16)
    c_q = [c_ref[e, qi] for e in heads]
    m_scr[...] = jnp.full(m_scr.shape, -jnp.inf, F32)
    l_scr[...] = jnp.zeros(l_scr.shape, F32)
    acc_scr[...] = jnp.zeros(acc_scr.shape, F32)

    def logits(kb, diagonal):
        base = pl.multiple_of(kb * tq, tq)
        gates = g3_ref[pl.ds(base, tq), :]
        zs = []
        for e in heads:
            keys = jnp.concatenate([k_ref[pl.ds(base, tq), hcols[e]], gates], axis=1)
            z = lax.dot_general(keys, qa_scr[e], NT_DIMS, preferred_element_type=F32)
            if diagonal:
                visible = (lax.broadcasted_iota(jnp.int32, (tq, tq), 0)
                           <= lax.broadcasted_iota(jnp.int32, (tq, tq), 1))
                z = jnp.where(visible, z, -jnp.inf)
            zs.append(z)
        return zs

    def update(kb, zs):
        ps, alphas = [], []
        for e in heads:
            m_prev = m_scr[e, 0:1, :]
            m_new = jnp.maximum(m_prev, jnp.max(zs[e], axis=0, keepdims=True) + c_q[e])
            alpha = jnp.exp2(m_prev - m_new)
            p = jnp.exp2(zs[e] - (m_new - c_q[e]))
            l_scr[e, 0:1, :] = alpha * l_scr[e, 0:1, :] + jnp.sum(p, axis=0, keepdims=True)
            m_scr[e, 0:1, :] = m_new
            ps.append(p.astype(BF16))
            alphas.append(alpha)
        for e in heads:
            acc_scr[e] = alphas[e] * acc_scr[e] + jnp.dot(vt_scr[e, kb], ps[e], preferred_element_type=F32)

    def blocks(first, count):
        zs = [logits(first + j, False) for j in range(count)]
        for j in range(count):
            update(first + j, zs[j])

    def group(g, carry):
        blocks(g * FOX_GROUP, FOX_GROUP)
        return carry

    lax.fori_loop(0, qi // FOX_GROUP, group, 0)
    done = (qi // FOX_GROUP) * FOX_GROUP
    size = FOX_GROUP // 2
    while size >= 1:
        take = (qi & size) != 0
        pl.when(take)(functools.partial(blocks, done, size))
        done = done + jnp.where(take, size, 0)
        size //= 2
    update(qi, logits(qi, True))
    for e in heads:
        out = (acc_scr[e] / l_scr[e, 0:1, :]).T
        y_ref[:, hcols[e]] = (_rms(out) * nw_ref[:, hcols[e]]).astype(BF16)


def _fox(proj, g3, c_rows, norm_w, *, batch, seq):
    m = proj.shape[0]
    nq = seq // FOX_BLOCK
    width = FOX_PAIR * HEAD_DIM
    q_col = (4 * MLSTM_W) // width
    k_col = q_col + FOX_W // width
    v_col = k_col + FOX_W // width
    return pl.pallas_call(
        _fox_body,
        out_shape=jax.ShapeDtypeStruct((m, FOX_W), BF16),
        grid=(batch, FOX_HEADS // FOX_PAIR, nq),
        in_specs=[
            pl.BlockSpec((FOX_BLOCK, width), lambda b, h, i: (b * nq + i, q_col + h)),
            pl.BlockSpec((seq, width), lambda b, h, i: (b, k_col + h)),
            pl.BlockSpec((seq, width), lambda b, h, i: (b, v_col + h)),
            pl.BlockSpec((seq, LANES), lambda b, h, i: (b, 0)),
            pl.BlockSpec((FOX_PAIR, nq, 1, FOX_BLOCK), lambda b, h, i: (h, b, 0, 0)),
            pl.BlockSpec((1, width), lambda b, h, i: (0, h)),
        ],
        out_specs=pl.BlockSpec((FOX_BLOCK, width), lambda b, h, i: (b * nq + i, h)),
        scratch_shapes=[
            pltpu.VMEM((FOX_PAIR, HEAD_DIM, FOX_BLOCK), F32),
            pltpu.VMEM((FOX_PAIR, SUBLANES, FOX_BLOCK), F32),
            pltpu.VMEM((FOX_PAIR, SUBLANES, FOX_BLOCK), F32),
            pltpu.VMEM((FOX_PAIR, FOX_BLOCK, 2 * HEAD_DIM), BF16),
            pltpu.VMEM((FOX_PAIR, nq, HEAD_DIM, FOX_BLOCK), BF16),
        ],
        compiler_params=_params(("parallel", "parallel", "arbitrary")),
        name="fox",
    )(proj, proj, proj, g3, c_rows, norm_w)


def _mm(x, y):
    return jnp.dot(x.astype(BF16), y.astype(BF16), preferred_element_type=F32)


def _unit_lower_inverses(a_mats, mask_ref):
    eye = mask_ref[MASK_INCL] - mask_ref[MASK_STRICT]
    negs = [-(a * mask_ref[MASK_DIAG8]) for a in a_mats]
    invs = [eye + n for n in negs]
    for _ in range(2):
        negs = [_mm(n, n) for n in negs]
        invs = [i + _mm(i, n) for i, n in zip(invs, negs)]
    for level in range(MASK_OFF0, MASK_COUNT):
        tmps = [_mm(a * mask_ref[level], i) for a, i in zip(a_mats, invs)]
        invs = [i - _mm(i, t) for i, t in zip(invs, tmps)]
    return invs


def _gdn_init(s_scr, mask_scr):
    s_scr[...] = jnp.zeros(s_scr.shape, F32)
    ri =lax.broadcasted_iota(jnp.int32, (CHUNK, CHUNK), 0)
    ci = lax.broadcasted_iota(jnp.int32, (CHUNK, CHUNK), 1)
    mask_scr[MASK_STRICT] = jnp.where(ri > ci, 1.0, 0.0)
    mask_scr[MASK_INCL] = jnp.where(ri >= ci, 1.0, 0.0)
    mask_scr[MASK_DIAG8] = jnp.where((ri >> 3) == (ci >> 3), 1.0, 0.0)
    for bits in range(3, 3 + MASK_COUNT - MASK_OFF0):
        sibling = ((ri >> (bits + 1)) == (ci >> (bits + 1))) & ((ri >> bits) != (ci >> bits))
        mask_scr[MASK_OFF0 + bits - 3] = jnp.where(sibling, 1.0, 0.0)


def _gdn_step(q_ref, k_ref, v_ref, z_ref, gr_ref, gc3_ref, sel_ref, ones_ref, nw_ref, y_ref, s_scr, mask_scr):
    nchunk = q_ref.shape[0] // CHUNK

    def row_sum(t):
        return jnp.dot(t.astype(BF16), ones_ref[...], preferred_element_type=F32)

    def l2n(t):
        return t * lax.rsqrt(row_sum(t * t) + NORM_EPS)

    cols_rep = [jnp.dot(gc3_ref[r * CHUNK:(r + 1) * CHUNK, :], sel_ref[...], preferred_element_type=F32)
                for r in range(nchunk)]

    groups = [(h, r) for h in range(GDN_HEADS) for r in range(nchunk)]
    prep = {}
    for h, r in groups:
        rows = slice(r * CHUNK, (r + 1) * CHUNK)
        hcols = slice(h * HEAD_DIM, (h + 1) * HEAD_DIM)
        q = l2n(q_ref[rows, hcols].astype(F32)) * QK_SCALE
        k = l2n(k_ref[rows, hcols].astype(F32))
        v = v_ref[rows, hcols].astype(F32)
        g_row = gr_ref[h:h + 1, rows]
        g_col = cols_rep[r][:, (2 * h) * LANES:(2 * h + 1) * LANES]
        beta = cols_rep[r][:, (2 * h + 1) * LANES:(2 * h + 2) * LANES]
        incl = mask_scr[MASK_INCL]
        decay = jnp.exp((g_col - g_row) * incl) * incl
        kb = k * beta
        e_g = jnp.exp(g_col)
        g_last = g_row[:, CHUNK - 1:CHUNK]
        kk_qk = lax.dot_general(jnp.concatenate([kb, q], axis=0).astype(BF16), k.astype(BF16), NT_DIMS,
                                preferred_element_type=F32)
        prep[h, r] = dict(
            a=kk_qk[:CHUNK] * (decay * mask_scr[MASK_STRICT]),
            attn=kk_qk[CHUNK:] * decay,
            rhs=jnp.concatenate([v * beta, kb * e_g], axis=1),
            q_dec=q * e_g,
            k_end=k * jnp.exp(g_last - g_col),
            s_decay=jnp.exp(g_last),
        )
    invs = _unit_lower_inverses([prep[g]["a"] for g in groups], mask_scr)
    for g, inv in zip(groups, invs):
        w = _mm(inv, prep[g]["rhs"])
        prep[g]["w_val"], prep[g]["w_key"] = w[:, :HEAD_DIM], w[:, HEAD_DIM:]

    states = [s_scr[h] for h in range(GDN_HEADS)]
    for r in range(nchunk):
        rows = slice(r * CHUNK, (r + 1) * CHUNK)
        ks = [_mm(jnp.concatenate([prep[h, r]["w_key"], prep[h, r]["q_dec"]], axis=0), states[h])
              for h in range(GDN_HEADS)]
        us = [prep[h, r]["w_val"] - ks[h][:CHUNK] for h in range(GDN_HEADS)]
        outs = [ks[h][CHUNK:] + _mm(prep[h, r]["attn"], us[h]) for h in range(GDN_HEADS)]
        states = [prep[h, r]["s_decay"] * states[h]
                  + lax.dot_general(prep[h, r]["k_end"].astype(BF16), us[h].astype(BF16), TN_DIMS,
                                    preferred_element_type=F32) for h in range(GDN_HEADS)]
        for h in range(GDN_HEADS):
            cols = slice(h * HEAD_DIM, (h + 1) * HEAD_DIM)
            gate = z_ref[rows, cols].astype(F32)
            normed = outs[h] * lax.rsqrt(row_sum(outs[h] * outs[h]) * (1.0 / HEAD_DIM) + NORM_EPS)
            y_ref[rows, cols] = (normed * nw_ref[:, cols] * (gate * _sigmoid(gate))).astype(BF16)
    for h in range(GDN_HEADS):
        s_scr[h] = states[h]


N_MLSTM_INPUTS = 10


N_GDN_INPUTS = 8


def _recurrent_body(*refs, n_side):
    m_in = refs[:N_MLSTM_INPUTS]
    g_in = refs[N_MLSTM_INPUTS:N_MLSTM_INPUTS + N_GDN_INPUTS]
    side_in = refs[N_MLSTM_INPUTS + N_GDN_INPUTS:N_MLSTM_INPUTS + N_GDN_INPUTS + n_side]
    outs = refs[N_MLSTM_INPUTS + N_GDN_INPUTS + n_side:-4]
    ym_ref, yg_ref, side_out = outs[0], outs[1], outs[2:]
    ct_scr, m_scr, s_scr, mask_scr = refs[-4:]
    gc3_ref = m_in[6]

    @pl.when(pl.program_id(1) == 0)
    def _():
        _mlstm_init(ct_scr, m_scr)
        _gdn_init(s_scr, mask_scr)

    for src, dst in zip(side_in, side_out):
        dst[...] = src[...].astype(BF16)
    _mlstm_step(*m_in, ym_ref, ct_scr, m_scr)
    _gdn_step(*g_in[:5], gc3_ref, *g_in[5:], yg_ref, s_scr, mask_scr)


def _recurrent(proj, gr, gc3, mlstm_norm_w, gdn_norm_w, side_weights, layer, *, batch, seq):
    m = proj.shape[0]
    ns = seq // REC_STEP
    row = lambda b, s: b * ns + s
    side_in_specs, side_out_specs, side_out_shapes = _side_cast_specs(side_weights, layer, batch * ns, row)
    m_sel = _one_hot_columns([c for h in range(MLSTM_HEADS) for c in (28 + h, 24 + h, 4 + h)])
    mean_w = jnp.full((HEAD_DIM, HEAD_DIM), 1.0 / HEAD_DIM, BF16)
    g_sel = _one_hot_columns([c for h in range(GDN_HEADS) for c in (16 + h, 20 + h)])
    z_col, qkv_col = GDN_Z_COL // GDN_W, GDN_QKV_COL // GDN_W
    wide = lambda c: pl.BlockSpec((REC_STEP, MLSTM_W), lambda b, s: (row(b, s), c))
    gate_rows = lambda r: pl.BlockSpec((SUBLANES, REC_STEP), lambda b, s: (r, row(b, s)))
    whole = lambda shape: pl.BlockSpec(shape, lambda b, s: (0,) * len(shape))
    outs = pl.pallas_call(
        functools.partial(_recurrent_body, n_side=len(side_weights)),
        out_shape=(jax.ShapeDtypeStruct((m, MLSTM_W), BF16), jax.ShapeDtypeStruct((m, GDN_W), BF16),
                   *side_out_shapes),
        grid=(batch, ns),
        in_specs=[
            wide(0), wide(1), wide(2), wide(3), gate_rows(0), gate_rows(3),
            pl.BlockSpec((REC_STEP, LANES), lambda b, s: (row(b, s), 0)),
            whole(m_sel.shape), whole((HEAD_DIM, HEAD_DIM)), whole((1, MLSTM_W)),
            wide(qkv_col), wide(qkv_col + 1), wide(qkv_col + 2), wide(z_col), gate_rows(2),
            whole(g_sel.shape), whole((HEAD_DIM, HEAD_DIM)), whole((1, GDN_W)),
            *side_in_specs,
        ],
        out_specs=(wide(0), wide(0), *side_out_specs),
        scratch_shapes=[
            pltpu.VMEM((MLSTM_HEADS, 2 * HEAD_DIM, HEAD_DIM), F32),
            pltpu.VMEM((SUBLANES, LANES), F32),
            pltpu.VMEM((GDN_HEADS, HEAD_DIM, HEAD_DIM), F32),
            pltpu.VMEM((MASK_COUNT, CHUNK, CHUNK), F32),
        ],
        compiler_params=_params(("arbitrary", "arbitrary")),
        name="recurrent",
    )(proj, proj, proj, proj, gr, gr, gc3, m_sel, mean_w, mlstm_norm_w,
      proj, proj, proj, proj, gr, g_sel, jnp.ones((HEAD_DIM, HEAD_DIM), BF16), gdn_norm_w,
      *side_weights)
    return outs[0], outs[1], outs[2:]


def _outproj_body(ym_ref, yf_ref, yg_ref, w_ref, x_ref, *rest):
    o_ref = rest[-1] if len(rest) == 1 else rest[1]
    acc = jnp.dot(ym_ref[...], w_ref[0:MLSTM_W, :], preferred_element_type=F32)
    acc += jnp.dot(yf_ref[...], w_ref[MLSTM_W:MLSTM_W + FOX_W, :], preferred_element_type=F32)
    acc += jnp.dot(yg_ref[...], w_ref[MLSTM_W + FOX_W:, :], preferred_element_type=F32)
    o_ref[...] = x_ref[...] + acc
    if len(rest) > 1:
        w_next_ref, _, wide_ref, gate_ref = rest
        _prep_w_in_body(w_next_ref, wide_ref, gate_ref)


def _outproj(y_m, y_f, y_g, w_out, x, w_in_t=None, next_layer=None, *, tm=512):
    m = x.shape[0]
    steps = m // tm
    in_specs = [
        pl.BlockSpec((tm, MLSTM_W), lambda i: (i, 0)),
        pl.BlockSpec((tm, FOX_W), lambda i: (i, 0)),
        pl.BlockSpec((tm, GDN_W), lambda i: (i, 0)),
        pl.BlockSpec((D_MODEL, D_MODEL), lambda i: (0, 0)),
        pl.BlockSpec((tm, D_MODEL), lambda i: (i, 0)),
    ]
    out_shape = [jax.ShapeDtypeStruct((m, D_MODEL), F32)]
    out_specs = [pl.BlockSpec((tm, D_MODEL), lambda i: (i, 0))]
    operands = [y_m, y_f, y_g, w_out, x]
    if w_in_t is not None:
        _, n_in, d = w_in_t.shape
        cols = d // steps
        in_specs.append(pl.BlockSpec((None, n_in, cols), lambda i: (next_layer, 0, i)))
        out_shape += [jax.ShapeDtypeStruct((PROJ_W, d), BF16), jax.ShapeDtypeStruct((GATE_ROWS, d), BF16)]
        out_specs += [pl.BlockSpec((PROJ_W, cols), lambda i: (0, i)), pl.BlockSpec((GATE_ROWS, cols), lambda i: (0, i))]
        operands.append(w_in_t)
    outs = pl.pallas_call(
        _outproj_body,
        out_shape=tuple(out_shape),
        grid=(steps,),
        in_specs=in_specs,
        out_specs=tuple(out_specs),
        compiler_params=_params(("parallel",)),
        name="outproj",
    )(*operands)
    return outs[0], tuple(outs[1:])


def _ffn_body(x_ref, nw_ref, wg_ref, wu_ref, wd_ref, fw_ref, o_ref, h_scr, *, final_norm, row_chunk):
    j = pl.program_id(1)
    chunks = [slice(r * row_chunk, (r + 1) * row_chunk) for r in range(x_ref.shape[0] // row_chunk)]

    @pl.when(j == 0)
    def _():
        for rows in chunks:
            x = x_ref[rows, :]
            h_scr[rows, :] = (_rms(x) * nw_ref[...]).astype(BF16)
            o_ref[rows, :] = x

    h = h_scr[...]
    gate = jnp.dot(h, wg_ref[...], preferred_element_type=F32)
    up = jnp.dot(h, wu_ref[...], preferred_element_type=F32)
    act = (gate * _sigmoid(gate) * up).astype(BF16)
    o_ref[...] += jnp.dot(act, wd_ref[...], preferred_element_type=F32)

    if final_norm:
        @pl.when(j == pl.num_programs(1) - 1)
        def _():
            for rows in chunks:
                o_ref[rows, :] = _rms(o_ref[rows, :]) * fw_ref[...]


def _ffn(x, norm_w, w_gate, w_up, w_down, final_w, *, final_norm, tm=1024, tf=512):
    m = x.shape[0]
    return pl.pallas_call(
        functools.partial(_ffn_body, final_norm=final_norm, row_chunk=256),
        out_shape=jax.ShapeDtypeStruct((m, D_MODEL), F32),
        grid=(m // tm, D_FF // tf),
        in_specs=[
            pl.BlockSpec((tm, D_MODEL), lambda i, j: (i, 0)),
            pl.BlockSpec((1, D_MODEL), lambda i, j: (0, 0)),
            pl.BlockSpec((D_MODEL, tf), lambda i, j: (0, j)),
            pl.BlockSpec((D_MODEL, tf), lambda i, j: (0, j)),
            pl.BlockSpec((tf, D_MODEL), lambda i, j: (j, 0)),
            pl.BlockSpec((1, D_MODEL), lambda i, j: (0, 0)),
        ],
        out_specs=pl.BlockSpec((tm, D_MODEL), lambda i, j: (i, 0)),
        scratch_shapes=[pltpu.VMEM((tm, D_MODEL), BF16)],
        compiler_params=_params(("parallel", "arbitrary")),
        name="ffn",
    )(x, norm_w, w_gate, w_up, w_down, final_w)


def _prep_w_in_body(w_ref, wide_ref, gate_ref):
    o_mi = 4 * MLSTM_W
    o_fq = o_mi + 2 * MLSTM_HEADS
    o_ff = o_fq + 3 * FOX_W
    o_gq = o_ff + FOX_HEADS
    o_ga = o_gq + 4 * GDN_W
    wide_ref[0:o_mi, :] = w_ref[0:o_mi, :].astype(BF16)
    wide_ref[o_mi:o_mi + 3 * FOX_W, :] = w_ref[o_fq:o_ff, :].astype(BF16)
    o_gz = o_gq + 3 * GDN_W
    wide_ref[GDN_Z_COL:GDN_QKV_COL, :] = w_ref[o_gz:o_ga, :].astype(BF16)
    wide_ref[GDN_QKV_COL:, :] = w_ref[o_gq:o_gz, :].astype(BF16)
    gate = jnp.concatenate([w_ref[o_mi:o_fq, :], w_ref[o_ff:o_gq, :], w_ref[o_ga:, :],
                            jnp.zeros((SUBLANES, w_ref.shape[1]), F32)], axis=0)
    gate_ref[...] = gate.astype(BF16)


def _prep_w_in(w_in_t, layer, *, cols=256):
    _, n_in, d = w_in_t.shape
    return pl.pallas_call(
        _prep_w_in_body,
        out_shape=(jax.ShapeDtypeStruct((PROJ_W, d), BF16), jax.ShapeDtypeStruct((GATE_ROWS, d), BF16)),
        grid=(d // cols,),
        in_specs=[pl.BlockSpec((None, n_in, cols), lambda i: (layer, 0, i))],
        out_specs=(pl.BlockSpec((PROJ_W, cols), lambda i: (0, i)), pl.BlockSpec((GATE_ROWS, cols), lambda i: (0, i))),
        compiler_params=_params(("parallel",)),
        name="prep_w_in",
    )(w_in_t)


def kernel(x, mix_norm_w, w_in, mlstm_i_bias, mlstm_f_bias, fox_f_bias, gdn_conv_w, gdn_a_log, gdn_dt_bias,
           mlstm_out_norm_w, fox_out_norm_w, gdn_out_norm_w, w_out, ffn_norm_w, w_gate, w_up, w_down,
           final_norm_w):
    batch, seq, d_model = x.shape
    depth = w_in.shape[0]
    assert d_model == D_MODEL and seq % FOX_BLOCK == 0 and seq % REC_STEP == 0 and MLSTM_W == GDN_W
    xf = x.reshape(batch * seq, d_model)
    zeros4 = jnp.zeros((GDN_HEADS,), F32)
    w_in_t = jnp.swapaxes(w_in, 1, 2)
    side_weights = [w_gate, w_up, w_down, w_out]
    w_wide, w_gate_t = _prep_w_in(w_in_t, 0)
    for l in range(depth):
        bias_col = jnp.pad(jnp.concatenate([mlstm_i_bias[l], mlstm_f_bias[l], fox_f_bias[l], gdn_dt_bias[l],
                                            zeros4]), (0, 8)).reshape(GATE_ROWS, 1)
        alog_col = jnp.pad(gdn_a_log[l], (16, 12)).reshape(GATE_ROWS, 1)

        proj, gt = _inproj(xf, mix_norm_w[l].reshape(1, -1), w_wide, w_gate_t, gdn_conv_w[l], seq=seq)
        gr, gr3 = _gates(gt, bias_col, alog_col, batch=batch, seq=seq)
        gc3 = gr3.T
        c_rows = gr[8:16].reshape(FOX_HEADS, batch * (seq // FOX_BLOCK), 1, FOX_BLOCK)

        y_m, y_g, (w_gate_b, w_up_b, w_down_b, w_out_b) = _recurrent(
            proj, gr, gc3, mlstm_out_norm_w[l].reshape(1, -1), gdn_out_norm_w[l].reshape(1, -1),
            side_weights, l, batch=batch, seq=seq)
        y_f = _fox(proj, gc3, c_rows, fox_out_norm_w[l].reshape(1, -1), batch=batch, seq=seq)
        if l + 1 < depth:
            xf, (w_wide, w_gate_t) = _outproj(y_m, y_f, y_g, w_out_b, xf, w_in_t, l + 1)
        else:
            xf, _ = _outproj(y_m, y_f, y_g, w_out_b, xf)
        xf = _ffn(xf, ffn_norm_w[l].reshape(1, -1), w_gate_b, w_up_b, w_down_b, final_norm_w.reshape(1, -1),
                  final_norm=(l == depth - 1))
    return xf.reshape(batch, seq, d_model)
```

```python
import functools

import jax
import jax.numpy as jnp
import numpy as np
from jax import lax
from jax.experimental import pallas as pl
from jax.experimental.pallas import tpu as pltpu

F32 = jnp.float32
BF16 = jnp.bfloat16

D_MODEL = 2048
HEAD_DIM = 128
MLSTM_HEADS = 4
FOX_HEADS = 8
GDN_HEADS = 4
MLSTM_W = MLSTM_HEADS * HEAD_DIM
FOX_W = FOX_HEADS * HEAD_DIM
GDN_W = GDN_HEADS * HEAD_DIM
D_FF = 5632
CONV_WIDTH = 4
GATE_SOFTCAP = 15.0
NORM_EPS = 1e-6
QK_SCALE = HEAD_DIM ** -0.5
LOG2E = 1.4426950408889634

PROJ_W = 4 * MLSTM_W + 3 * FOX_W + 4 * GDN_W
GDN_Z_COL = 4 * MLSTM_W + 3 * FOX_W
GDN_QKV_COL = GDN_Z_COL + GDN_W
GATE_ROWS = 32
FOX_GATE_ROW = 2 * MLSTM_HEADS

LANES = 128
SUBLANES = 8
V7X_VMEM_BYTES = 64 * 1024 * 1024
VMEM_LIMIT = V7X_VMEM_BYTES * 7 // 8

CHUNK = 128
FOX_BLOCK = 512
FOX_GROUP = 4
FOX_PAIR = 4
REC_STEP = 512
GATE_T_CHUNK = 512

MASK_STRICT, MASK_INCL, MASK_DIAG8, MASK_OFF0 = 0, 1, 2, 3
MASK_COUNT = MASK_OFF0 + 4

NT_DIMS = (((1,), (1,)), ((), ()))
TN_DIMS = (((0,), (0,)), ((), ()))


def _params(sem):
    return pltpu.CompilerParams(dimension_semantics=sem, vmem_limit_bytes=VMEM_LIMIT)


def _sigmoid(z):
    return 1.0 / (1.0 + jnp.exp(-z))


def _log_sigmoid(z):
    return jnp.minimum(z, 0.0) - jnp.log1p(jnp.exp(-jnp.abs(z)))


def _softplus(z):
    return jnp.maximum(z, 0.0) + jnp.log1p(jnp.exp(-jnp.abs(z)))


def _rms(t):
    return t * lax.rsqrt(jnp.mean(t * t, axis=-1, keepdims=True) + NORM_EPS)


def _inproj_body(x_ref, nw_ref, w_ref, wg_ref, cw_ref, proj_ref, gt_ref, h_scr, halo_scr, *,
                 row_chunk, tiles_per_seq, conv_col):
    i, j = pl.program_id(0), pl.program_id(1)
    last = pl.num_programs(1) - 1
    halo = halo_scr.shape[0]

    @pl.when(j == 0)
    def _():
        for r in range(x_ref.shape[0] // row_chunk):
            rows = slice(r * row_chunk, (r + 1) * row_chunk)
            h_scr[rows, :] = (_rms(x_ref[rows, :]) * nw_ref[...]).astype(BF16)
        gt_ref[...] = lax.dot_general(wg_ref[...], h_scr[...], NT_DIMS, preferred_element_type=F32)

    @pl.when(j < last)
    def _():
        proj_ref[...] = lax.dot_general(h_scr[...], w_ref[...], NT_DIMS, preferred_element_type=F32).astype(BF16)

    @pl.when((j == last) & (lax.rem(i, tiles_per_seq) == 0))
    def _():
        halo_scr[...] = jnp.zeros(halo_scr.shape, F32)

    @pl.when(j == last)
    def _():
        acc = lax.dot_general(h_scr[...], w_ref[...], NT_DIMS, preferred_element_type=F32)
        proj_ref[:, :conv_col] = acc[:, :conv_col].astype(BF16)
        pre = acc[:, conv_col:]
        ext = jnp.concatenate([halo_scr[...], pre], axis=0)
        conv = ext * cw_ref[0:1, :]
        for tap in range(1, CONV_WIDTH):
            conv = pltpu.roll(conv, 1, 0) + ext * cw_ref[tap:tap + 1, :]
        conv = conv[halo:, :]
        halo_scr[...] = pre[pre.shape[0] - halo:, :]
        proj_ref[:, conv_col:] = (conv * _sigmoid(conv)).astype(BF16)


def _inproj(x, norm_w, w_big, w_gate_t, conv_w, *, seq, tm=1024, tn=1792):
    m = x.shape[0]
    nj = PROJ_W // tn
    assert seq % tm == 0 and GDN_QKV_COL >= (nj - 1) * tn
    return pl.pallas_call(
        functools.partial(_inproj_body, row_chunk=256, tiles_per_seq=seq // tm,
                          conv_col=GDN_QKV_COL - (nj - 1) * tn),
        out_shape=(jax.ShapeDtypeStruct((m, PROJ_W), BF16), jax.ShapeDtypeStruct((GATE_ROWS, m), F32)),
        grid=(m // tm, nj),
        in_specs=[
            pl.BlockSpec((tm, D_MODEL), lambda i, j: (i, 0)),
            pl.BlockSpec((1, D_MODEL), lambda i, j: (0, 0)),
            pl.BlockSpec((tn, D_MODEL), lambda i, j: (j, 0)),
            pl.BlockSpec((GATE_ROWS, D_MODEL), lambda i, j: (0, 0)),
            pl.BlockSpec((CONV_WIDTH, 3 * GDN_W), lambda i, j: (0, 0)),
        ],
        out_specs=(
            pl.BlockSpec((tm, tn), lambda i, j: (i, j)),
            pl.BlockSpec((GATE_ROWS, tm), lambda i, j: (0, i)),
        ),
        scratch_shapes=[pltpu.VMEM((tm, D_MODEL), BF16), pltpu.VMEM((SUBLANES, 3 * GDN_W), F32)],
        compiler_params=_params(("arbitrary", "arbitrary")),
        name="inproj",
    )(x, norm_w, w_big, w_gate_t, conv_w)


def _side_cast_specs(weights, layer, steps, step_of):
    in_specs, out_specs, out_shapes = [], [], []
    for w in weights:
        _, rows, cols = w.shape
        band = rows // steps
        assert rows % steps == 0 and band % (2 * SUBLANES) == 0
        in_specs.append(pl.BlockSpec((None, band, cols), lambda *g: (layer, step_of(*g), 0)))
        out_specs.append(pl.BlockSpec((band, cols), lambda *g: (step_of(*g), 0)))
        out_shapes.append(jax.ShapeDtypeStruct((rows, cols), BF16))
    return in_specs, out_specs, out_shapes


def _lane_cumsum(v, seg):
    pos = lax.broadcasted_iota(jnp.int32, v.shape, 1) & (seg - 1)
    shift = 1
    while shift < seg:
        v = v + jnp.where(pos >= shift, pltpu.roll(v, shift, 1), 0.0)
        shift *= 2
    return v


def _lane_cummax(v, seg):
    pos = lax.broadcasted_iota(jnp.int32, v.shape, 1) & (seg - 1)
    shift = 1
    while shift < seg:
        v = jnp.maximum(v, jnp.where(pos >= shift, pltpu.roll(v, shift, 1), -jnp.inf))
        shift *= 2
    return v


def _gates_body(gt_ref, bias_ref, alog_ref, out_ref, split_ref):
    seq = gt_ref.shape[1]
    first4 = lax.broadcasted_iota(jnp.int32, (SUBLANES, seq), 0) < 4
    z = gt_ref[0:8, :] + bias_ref[0:8, :]
    z = GATE_SOFTCAP * jnp.tanh(z / GATE_SOFTCAP)
    b_cum = _lane_cumsum(_log_sigmoid(z), CHUNK)
    out_ref[0:8, :] = jnp.where(first4, z, b_cum)
    d = z - pltpu.roll(b_cum, 4, 0)
    out_ref[24:32, :] = jnp.where(first4, d, pltpu.roll(_lane_cummax(d, CHUNK), 4, 0))

    raw = gt_ref[16:24, :]
    decay = _lane_cumsum(-jnp.exp(alog_ref[16:24, :]) * _softplus(raw + bias_ref[16:24, :]), CHUNK)
    out_ref[16:24, :] = jnp.where(first4, decay, _sigmoid(raw))

    fox = _lane_cumsum(LOG2E * _log_sigmoid(gt_ref[8:16, :] + bias_ref[8:16, :]), LANES)
    carry = jnp.zeros((SUBLANES, 1), F32)
    for blk in range(seq // LANES):
        lanes = slice(blk * LANES, (blk + 1) * LANES)
        tile = fox[:, lanes] + carry
        out_ref[8:16, lanes] = tile
        carry = tile[:, LANES - 1:LANES]

    value = out_ref[...]
    hi = value.astype(BF16)
    rest = value - hi.astype(F32)
    mid = rest.astype(BF16)
    lo = (rest - mid.astype(F32)).astype(BF16)
    parts = jnp.concatenate([hi, mid, lo, jnp.zeros((LANES - 3 * GATE_ROWS, seq), BF16)], axis=0)
    for c in range(seq // GATE_T_CHUNK):
        cols = slice(c * GATE_T_CHUNK, (c + 1) * GATE_T_CHUNK)
        split_ref[cols, :] = parts[:, cols].T


def _gates(gt, bias_col, alog_col, *, batch, seq):
    return pl.pallas_call(
        _gates_body,
        out_shape=(jax.ShapeDtypeStruct(gt.shape, F32), jax.ShapeDtypeStruct((gt.shape[1], LANES), BF16)),
        grid=(batch,),
        in_specs=[
            pl.BlockSpec((GATE_ROWS, seq), lambda b: (0, b)),
            pl.BlockSpec((GATE_ROWS, 1), lambda b: (0, 0)),
            pl.BlockSpec((GATE_ROWS, 1), lambda b: (0, 0)),
        ],
        out_specs=(pl.BlockSpec((GATE_ROWS, seq), lambda b: (0, b)), pl.BlockSpec((seq, LANES), lambda b: (b, 0))),
        compiler_params=_params(("parallel",)),
        name="gates",
    )(gt, bias_col, alog_col)


def _mlstm_init(ct_scr, m_scr):
    ct_scr[...] = jnp.zeros(ct_scr.shape, F32)
    m_scr[...] = jnp.zeros(m_scr.shape, F32)


def _mlstm_step(q_ref, k_ref, v_ref, o_ref, gb_ref, gd_ref, gc3_ref, sel_ref, mean_ref, nw_ref, y_ref,
                ct_scr, m_scr):
    heads = range(MLSTM_HEADS)
    causal = (lax.broadcasted_iota(jnp.int32, (CHUNK, CHUNK), 0)
              >= lax.broadcasted_iota(jnp.int32, (CHUNK, CHUNK), 1))
    ones = jnp.ones((CHUNK, HEAD_DIM), BF16)
    cts = [ct_scr[h] for h in heads]
    ms = [m_scr[h:h + 1, 0:1] for h in heads]
    for r in range(q_ref.shape[0] // CHUNK):
        rows = slice(r * CHUNK, (r + 1) * CHUNK)
        hcols = [slice(h * HEAD_DIM, (h + 1) * HEAD_DIM) for h in heads]
        q = [q_ref[rows, c] for c in hcols]
        k = [k_ref[rows, c] for c in hcols]
        v_aug = [jnp.concatenate([v_ref[rows, c], ones], axis=1) for c in hcols]
        cols_rep = jnp.dot(gc3_ref[rows, :], sel_ref[...], preferred_element_type=F32)
        rep = lambda h, j: cols_rep[:, (3 * h + j) * LANES:(3 * h + j + 1) * LANES]
        a_col = [rep(h, 0) for h in heads]
        d_col = [rep(h, 1) for h in heads]
        b_col = [rep(h, 2) for h in heads]
        d_row = [gd_ref[h:h + 1, rows] for h in heads]
        b_last = [gb_ref[4 + h:5 + h, rows][:, CHUNK - 1:CHUNK] for h in heads]
        a_last = [gd_ref[4 + h:5 + h, rows][:, CHUNK - 1:CHUNK] for h in heads]

        qk = [lax.dot_general(q[h], k[h], NT_DIMS, preferred_element_type=F32) for h in heads]
        q_state = [lax.dot_general(q[h], cts[h].astype(BF16), NT_DIMS, preferred_element_type=F32)
                   for h in heads]
        m_rel = [jnp.maximum(ms[h], a_col[h]) for h in heads]
        p = [jnp.exp(jnp.where(causal, d_row[h] - m_rel[h], -jnp.inf)) * (qk[h] * QK_SCALE) for h in heads]
        intra = [jnp.dot(p[h].astype(BF16), v_aug[h], preferred_element_type=F32) for h in heads]
        for h in heads:
            w_inter = jnp.exp(ms[h] - m_rel[h])
            both = jnp.concatenate([w_inter, w_inter], axis=1) * q_state[h] + intra[h]
            floor = jnp.exp(-(b_col[h] + m_rel[h]))
            h_out = both[:, :HEAD_DIM] / jnp.maximum(jnp.abs(both[:, HEAD_DIM:]), floor)
            mean = jnp.dot(h_out.astype(BF16), mean_ref[...], preferred_element_type=F32)
            centered = h_out - mean
            var = jnp.dot((centered * centered).astype(BF16), mean_ref[...], preferred_element_type=F32)
            gate = _sigmoid(o_ref[rows, hcols[h]].astype(F32))
            y_ref[rows, hcols[h]] = (centered * lax.rsqrt(var + NORM_EPS) * nw_ref[:, hcols[h]]
                                     * gate).astype(BF16)

        m_end = [jnp.maximum(ms[h], a_last[h]) for h in heads]
        kw = [(k[h].astype(F32) * (jnp.exp(d_col[h] - m_end[h]) * QK_SCALE)).astype(BF16) for h in heads]
        cts = [jnp.exp(ms[h] - m_end[h]) * cts[h]
               + lax.dot_general(v_aug[h], kw[h], TN_DIMS, preferred_element_type=F32) for h in heads]
        ms = [b_last[h] + m_end[h] for h in heads]
    for h in heads:
        ct_scr[h] = cts[h]
        m_scr[h:h + 1, :] = jnp.broadcast_to(ms[h], (1, LANES))


def _one_hot_columns(gate_cols):
    sel = np.zeros((LANES, LANES * len(gate_cols)), np.float32)
    for j, col in enumerate(gate_cols):
        for part in range(3):
            sel[part * GATE_ROWS + col, j * LANES:(j + 1) * LANES] = 1.0
    return jnp.asarray(sel, BF16)


def _fox_body(q_ref, k_ref, v_ref, g3_ref, c_ref, nw_ref, y_ref, acc_scr, m_scr, l_scr, qa_scr, vt_scr):
    pair = pl.program_id(1)
    qi = pl.program_id(2)
    tq = q_ref.shape[0]
    heads = range(FOX_PAIR)
    hcols = [slice(e * HEAD_DIM, (e + 1) * HEAD_DIM) for e in heads]

    @pl.when(qi == 0)
    def _():
        for e in heads:
            for j in range(vt_scr.shape[1]):
                vt_scr[e, j] = v_ref[j * tq:(j + 1) * tq, hcols[e]].T

    lane = lax.broadcasted_iota(jnp.int32, (tq, LANES), 1)
    for e in heads:
        fox_row = FOX_GATE_ROW + FOX_PAIR * pair + e
        split_lane = (lane == fox_row) | (lane == GATE_ROWS + fox_row) | (lane == 2 * GATE_ROWS + fox_row)
        qa_scr[e, :, 0:HEAD_DIM] = (q_ref[:, hcols[e]].astype(F32) * (QK_SCALE * LOG2E)).astype(BF16)
        qa_scr[e, :, HEAD_DIM:] = jnp.where(split_lane, -1.0, 0.0).astype(BF16)
    c_q = [c_ref[e, qi] for e in heads]
    m_scr[...] = jnp.full(m_scr.shape, -jnp.inf, F32)
    l_scr[...] = jnp.zeros(l_scr.shape, F32)
    acc_scr[...] = jnp.zeros(acc_scr.shape, F32)

    def logits(kb, diagonal):
        base = pl.multiple_of(kb * tq, tq)
        gates = g3_ref[pl.ds(base, tq), :]
        zs = []
        for e in heads:
            keys = jnp.concatenate([k_ref[pl.ds(base, tq), hcols[e]], gates], axis=1)
            z = lax.dot_general(keys, qa_scr[e], NT_DIMS, preferred_element_type=F32)
            if diagonal:
                visible = (lax.broadcasted_iota(jnp.int32, (tq, tq), 0)
                           <= lax.broadcasted_iota(jnp.int32, (tq, tq), 1))
                z = jnp.where(visible, z, -jnp.inf)
            zs.append(z)
        return zs

    def update(kb, zs):
        ps, alphas = [], []
        for e in heads:
            m_prev = m_scr[e, 0:1, :]
            m_new = jnp.maximum(m_prev, jnp.max(zs[e], axis=0, keepdims=True) + c_q[e])
            alpha = jnp.exp2(m_prev - m_new)
            p = jnp.exp2(zs[e] - (m_new - c_q[e]))
            l_scr[e, 0:1, :] = alpha * l_scr[e, 0:1, :] + jnp.sum(p, axis=0, keepdims=True)
            m_scr[e, 0:1, :] = m_new
            ps.append(p.astype(BF16))
            alphas.append(alpha)
        for e in heads:
            acc_scr[e] = alphas[e] * acc_scr[e] + jnp.dot(vt_scr[e, kb], ps[e], preferred_element_type=F32)

    def blocks(first, count):
        zs = [logits(first + j, False) for j in range(count)]
        for j in range(count):
            update(first + j, zs[j])

    def group(g, carry):
        blocks(g * FOX_GROUP, FOX_GROUP)
        return carry

    lax.fori_loop(0, qi // FOX_GROUP, group, 0)
    done = (qi // FOX_GROUP) * FOX_GROUP
    size = FOX_GROUP // 2
    while size >= 1:
        take = (qi & size) != 0
        pl.when(take)(functools.partial(blocks, done, size))
        done = done + jnp.where(take, size, 0)
        size //= 2
    update(qi, logits(qi, True))
    for e in heads:
        out = (acc_scr[e] / l_scr[e, 0:1, :]).T
        y_ref[:, hcols[e]] = (_rms(out) * nw_ref[:, hcols[e]]).astype(BF16)


def _fox(proj, g3, c_rows, norm_w, *, batch, seq):
    m = proj.shape[0]
    nq = seq // FOX_BLOCK
    width = FOX_PAIR * HEAD_DIM
    q_col = (4 * MLSTM_W) // width
    k_col = q_col + FOX_W // width
    v_col = k_col + FOX_W // width
    return pl.pallas_call(
        _fox_body,
        out_shape=jax.ShapeDtypeStruct((m, FOX_W), BF16),
        grid=(batch, FOX_HEADS // FOX_PAIR, nq),
        in_specs=[
            pl.BlockSpec((FOX_BLOCK, width), lambda b, h, i: (b * nq + i, q_col + h)),
            pl.BlockSpec((seq, width), lambda b, h, i: (b, k_col + h)),
            pl.BlockSpec((seq, width), lambda b, h, i: (b, v_col + h)),
            pl.BlockSpec((seq, LANES), lambda b, h, i: (b, 0)),
            pl.BlockSpec((FOX_PAIR, nq, 1, FOX_BLOCK), lambda b, h, i: (FOX_GATE_ROW // FOX_PAIR + h, b, 0, 0)),
            pl.BlockSpec((1, width), lambda b, h, i: (0, h)),
        ],
        out_specs=pl.BlockSpec((FOX_BLOCK, width), lambda b, h, i: (b * nq + i, h)),
        scratch_shapes=[
            pltpu.VMEM((FOX_PAIR, HEAD_DIM, FOX_BLOCK), F32),
            pltpu.VMEM((FOX_PAIR, SUBLANES, FOX_BLOCK), F32),
            pltpu.VMEM((FOX_PAIR, SUBLANES, FOX_BLOCK), F32),
            pltpu.VMEM((FOX_PAIR, FOX_BLOCK, 2 * HEAD_DIM), BF16),
            pltpu.VMEM((FOX_PAIR, nq, HEAD_DIM, FOX_BLOCK), BF16),
        ],
        compiler_params=_params(("parallel", "parallel", "arbitrary")),
        name="fox",
    )(proj, proj, proj, g3, c_rows, norm_w)


def _mm(x, y):
    return jnp.dot(x.astype(BF16), y.astype(BF16), preferred_element_type=F32)


def _unit_lower_inverses(a_mats, mask_ref):
    eye = mask_ref[MASK_INCL] - mask_ref[MASK_STRICT]
    negs = [-(a * mask_ref[MASK_DIAG8]) for a in a_mats]
    invs = [eye + n for n in negs]
    for _ in range(2):
        negs = [_mm(n, n) for n in negs]
        invs = [i + _mm(i, n) for i, n in zip(invs, negs)]
    for level in range(MASK_OFF0, MASK_COUNT):
        tmps = [_mm(a * mask_ref[level], i) for a, i in zip(a_mats, invs)]
        invs = [i - _mm(i, t) for i, t in zip(invs, tmps)]
    return invs


def _gdn_init(s_scr, mask_scr):
    s_scr[...] = jnp.zeros(s_scr.shape, F32)
    ri =lax.broadcasted_iota(jnp.int32, (CHUNK, CHUNK), 0)
    ci = lax.broadcasted_iota(jnp.int32, (CHUNK, CHUNK), 1)
    mask_scr[MASK_STRICT] = jnp.where(ri > ci, 1.0, 0.0)
    mask_scr[MASK_INCL] = jnp.where(ri >= ci, 1.0, 0.0)
    mask_scr[MASK_DIAG8] = jnp.where((ri >> 3) == (ci >> 3), 1.0, 0.0)
    for bits in range(3, 3 + MASK_COUNT - MASK_OFF0):
        sibling = ((ri >> (bits + 1)) == (ci >> (bits + 1))) & ((ri >> bits) != (ci >> bits))
        mask_scr[MASK_OFF0 + bits - 3] = jnp.where(sibling, 1.0, 0.0)


def _gdn_step(q_ref, k_ref, v_ref, z_ref, gr_ref, gc3_ref, sel_ref, ones_ref, nw_ref, y_ref, s_scr, mask_scr):
    nchunk = q_ref.shape[0] // CHUNK

    def row_sum(t):
        return jnp.dot(t.astype(BF16), ones_ref[...], preferred_element_type=F32)

    def l2n(t):
        return t * lax.rsqrt(row_sum(t * t) + NORM_EPS)

    cols_rep = [jnp.dot(gc3_ref[r * CHUNK:(r + 1) * CHUNK, :], sel_ref[...], preferred_element_type=F32)
                for r in range(nchunk)]

    groups = [(h, r) for h in range(GDN_HEADS) for r in range(nchunk)]
    prep = {}
    for h, r in groups:
        rows = slice(r * CHUNK, (r + 1) * CHUNK)
        hcols = slice(h * HEAD_DIM, (h + 1) * HEAD_DIM)
        q = l2n(q_ref[rows, hcols].astype(F32)) * QK_SCALE
        k = l2n(k_ref[rows, hcols].astype(F32))
        v = v_ref[rows, hcols].astype(F32)
        g_row = gr_ref[h:h + 1, rows]
        g_col = cols_rep[r][:, (2 * h) * LANES:(2 * h + 1) * LANES]
        beta = cols_rep[r][:, (2 * h + 1) * LANES:(2 * h + 2) * LANES]
        incl = mask_scr[MASK_INCL]
        decay = jnp.exp((g_col - g_row) * incl) * incl
        kb = k * beta
        e_g = jnp.exp(g_col)
        g_last = g_row[:, CHUNK - 1:CHUNK]
        kk_qk = lax.dot_general(jnp.concatenate([kb, q], axis=0).astype(BF16), k.astype(BF16), NT_DIMS,
                                preferred_element_type=F32)
        prep[h, r] = dict(
            a=kk_qk[:CHUNK] * (decay * mask_scr[MASK_STRICT]),
            attn=kk_qk[CHUNK:] * decay,
            rhs=jnp.concatenate([v * beta, kb * e_g], axis=1),
            q_dec=q * e_g,
            k_end=k * jnp.exp(g_last - g_col),
            s_decay=jnp.exp(g_last),
        )
    invs = _unit_lower_inverses([prep[g]["a"] for g in groups], mask_scr)
    for g, inv in zip(groups, invs):
        w = _mm(inv, prep[g]["rhs"])
        prep[g]["w_val"], prep[g]["w_key"] = w[:, :HEAD_DIM], w[:, HEAD_DIM:]

    states = [s_scr[h] for h in range(GDN_HEADS)]
    for r in range(nchunk):
        rows = slice(r * CHUNK, (r + 1) * CHUNK)
        ks = [_mm(jnp.concatenate([prep[h, r]["w_key"], prep[h, r]["q_dec"]], axis=0), states[h])
              for h in range(GDN_HEADS)]
        us = [prep[h, r]["w_val"] - ks[h][:CHUNK] for h in range(GDN_HEADS)]
        outs = [ks[h][CHUNK:] + _mm(prep[h, r]["attn"], us[h]) for h in range(GDN_HEADS)]
        states = [prep[h, r]["s_decay"] * states[h]
                  + lax.dot_general(prep[h, r]["k_end"].astype(BF16), us[h].astype(BF16), TN_DIMS,
                                    preferred_element_type=F32) for h in range(GDN_HEADS)]
        for h in range(GDN_HEADS):
            cols = slice(h * HEAD_DIM, (h + 1) * HEAD_DIM)
            gate = z_ref[rows, cols].astype(F32)
            normed = outs[h] * lax.rsqrt(row_sum(outs[h] * outs[h]) * (1.0 / HEAD_DIM) + NORM_EPS)
            y_ref[rows, cols] = (normed * nw_ref[:, cols] * (gate * _sigmoid(gate))).astype(BF16)
    for h in range(GDN_HEADS):
        s_scr[h] = states[h]


N_MLSTM_INPUTS = 10


N_GDN_INPUTS = 8


def _recurrent_body(*refs, n_side):
    m_in = refs[:N_MLSTM_INPUTS]
    g_in = refs[N_MLSTM_INPUTS:N_MLSTM_INPUTS + N_GDN_INPUTS]
    side_in = refs[N_MLSTM_INPUTS + N_GDN_INPUTS:N_MLSTM_INPUTS + N_GDN_INPUTS + n_side]
    outs = refs[N_MLSTM_INPUTS + N_GDN_INPUTS + n_side:-4]
    ym_ref, yg_ref, side_out = outs[0], outs[1], outs[2:]
    ct_scr, m_scr, s_scr, mask_scr = refs[-4:]
    gc3_ref = m_in[6]

    @pl.when(pl.program_id(1) == 0)
    def _():
        _mlstm_init(ct_scr, m_scr)
        _gdn_init(s_scr, mask_scr)

    for src, dst in zip(side_in, side_out):
        dst[...] = src[...].astype(BF16)
    _mlstm_step(*m_in, ym_ref, ct_scr, m_scr)
    _gdn_step(*g_in[:5], gc3_ref, *g_in[5:], yg_ref, s_scr, mask_scr)


def _recurrent(proj, gr, gc3, mlstm_norm_w, gdn_norm_w, side_weights, layer, *, batch, seq):
    m = proj.shape[0]
    ns = seq // REC_STEP
    row = lambda b, s: b * ns + s
    side_in_specs, side_out_specs, side_out_shapes = _side_cast_specs(side_weights, layer, batch * ns, row)
    m_sel = _one_hot_columns([c for h in range(MLSTM_HEADS) for c in (28 + h, 24 + h, 4 + h)])
    mean_w = jnp.full((HEAD_DIM, HEAD_DIM), 1.0 / HEAD_DIM, BF16)
    g_sel = _one_hot_columns([c for h in range(GDN_HEADS) for c in (16 + h, 20 + h)])
    z_col, qkv_col = GDN_Z_COL // GDN_W, GDN_QKV_COL // GDN_W
    wide = lambda c: pl.BlockSpec((REC_STEP, MLSTM_W), lambda b, s: (row(b, s), c))
    gate_rows = lambda r: pl.BlockSpec((SUBLANES, REC_STEP), lambda b, s: (r, row(b, s)))
    whole = lambda shape: pl.BlockSpec(shape, lambda b, s: (0,) * len(shape))
    outs = pl.pallas_call(
        functools.partial(_recurrent_body, n_side=len(side_weights)),
        out_shape=(jax.ShapeDtypeStruct((m, MLSTM_W), BF16), jax.ShapeDtypeStruct((m, GDN_W), BF16),
                   *side_out_shapes),
        grid=(batch, ns),
        in_specs=[
            wide(0), wide(1), wide(2), wide(3), gate_rows(0), gate_rows(3),
            pl.BlockSpec((REC_STEP, LANES), lambda b, s: (row(b, s), 0)),
            whole(m_sel.shape), whole((HEAD_DIM, HEAD_DIM)), whole((1, MLSTM_W)),
            wide(qkv_col), wide(qkv_col + 1), wide(qkv_col + 2), wide(z_col), gate_rows(2),
            whole(g_sel.shape), whole((HEAD_DIM, HEAD_DIM)), whole((1, GDN_W)),
            *side_in_specs,
        ],
        out_specs=(wide(0), wide(0), *side_out_specs),
        scratch_shapes=[
            pltpu.VMEM((MLSTM_HEADS, 2 * HEAD_DIM, HEAD_DIM), F32),
            pltpu.VMEM((SUBLANES, LANES), F32),
            pltpu.VMEM((GDN_HEADS, HEAD_DIM, HEAD_DIM), F32),
            pltpu.VMEM((MASK_COUNT, CHUNK, CHUNK), F32),
        ],
        compiler_params=_params(("arbitrary", "arbitrary")),
        name="recurrent",
    )(proj, proj, proj, proj, gr, gr, gc3, m_sel, mean_w, mlstm_norm_w,
      proj, proj, proj, proj, gr, g_sel, jnp.ones((HEAD_DIM, HEAD_DIM), BF16), gdn_norm_w,
      *side_weights)
    return outs[0], outs[1], outs[2:]


def _outproj_body(ym_ref, yf_ref, yg_ref, w_ref, x_ref, *rest):
    o_ref = rest[-1] if len(rest) == 1 else rest[1]
    acc = jnp.dot(ym_ref[...], w_ref[0:MLSTM_W, :], preferred_element_type=F32)
    acc += jnp.dot(yf_ref[...], w_ref[MLSTM_W:MLSTM_W + FOX_W, :], preferred_element_type=F32)
    acc += jnp.dot(yg_ref[...], w_ref[MLSTM_W + FOX_W:, :], preferred_element_type=F32)
    o_ref[...] = x_ref[...] + acc
    if len(rest) > 1:
        w_next_ref, _, wide_ref, gate_ref = rest
        _prep_w_in_body(w_next_ref, wide_ref, gate_ref)


def _outproj(y_m, y_f, y_g, w_out, x, w_in_t=None, next_layer=None, *, tm=512):
    m = x.shape[0]
    steps = m // tm
    in_specs = [
        pl.BlockSpec((tm, MLSTM_W), lambda i: (i, 0)),
        pl.BlockSpec((tm, FOX_W), lambda i: (i, 0)),
        pl.BlockSpec((tm, GDN_W), lambda i: (i, 0)),
        pl.BlockSpec((D_MODEL, D_MODEL), lambda i: (0, 0)),
        pl.BlockSpec((tm, D_MODEL), lambda i: (i, 0)),
    ]
    out_shape = [jax.ShapeDtypeStruct((m, D_MODEL), F32)]
    out_specs = [pl.BlockSpec((tm, D_MODEL), lambda i: (i, 0))]
    operands = [y_m, y_f, y_g, w_out, x]
    if w_in_t is not None:
        _, n_in, d = w_in_t.shape
        cols = d // steps
        in_specs.append(pl.BlockSpec((None, n_in, cols), lambda i: (next_layer, 0, i)))
        out_shape += [jax.ShapeDtypeStruct((PROJ_W, d), BF16), jax.ShapeDtypeStruct((GATE_ROWS, d), BF16)]
        out_specs += [pl.BlockSpec((PROJ_W, cols), lambda i: (0, i)), pl.BlockSpec((GATE_ROWS, cols), lambda i: (0, i))]
        operands.append(w_in_t)
    outs = pl.pallas_call(
        _outproj_body,
        out_shape=tuple(out_shape),
        grid=(steps,),
        in_specs=in_specs,
        out_specs=tuple(out_specs),
        compiler_params=_params(("parallel",)),
        name="outproj",
    )(*operands)
    return outs[0], tuple(outs[1:])


def _ffn_body(x_ref, nw_ref, wg_ref, wu_ref, wd_ref, fw_ref, o_ref, h_scr, *, final_norm, row_chunk):
    j = pl.program_id(1)
    chunks = [slice(r * row_chunk, (r + 1) * row_chunk) for r in range(x_ref.shape[0] // row_chunk)]

    @pl.when(j == 0)
    def _():
        for rows in chunks:
            x = x_ref[rows, :]
            h_scr[rows, :] = (_rms(x) * nw_ref[...]).astype(BF16)
            o_ref[rows, :] = x

    h = h_scr[...]
    gate = jnp.dot(h, wg_ref[...], preferred_element_type=F32)
    up = jnp.dot(h, wu_ref[...], preferred_element_type=F32)
    act = (gate * _sigmoid(gate) * up).astype(BF16)
    o_ref[...] += jnp.dot(act, wd_ref[...], preferred_element_type=F32)

    if final_norm:
        @pl.when(j == pl.num_programs(1) - 1)
        def _():
            for rows in chunks:
                o_ref[rows, :] = _rms(o_ref[rows, :]) * fw_ref[...]


def _ffn(x, norm_w, w_gate, w_up, w_down, final_w, *, final_norm, tm=1024, tf=512):
    m = x.shape[0]
    return pl.pallas_call(
        functools.partial(_ffn_body, final_norm=final_norm, row_chunk=256),
        out_shape=jax.ShapeDtypeStruct((m, D_MODEL), F32),
        grid=(m // tm, D_FF // tf),
        in_specs=[
            pl.BlockSpec((tm, D_MODEL), lambda i, j: (i, 0)),
            pl.BlockSpec((1, D_MODEL), lambda i, j: (0, 0)),
            pl.BlockSpec((D_MODEL, tf), lambda i, j: (0, j)),
            pl.BlockSpec((D_MODEL, tf), lambda i, j: (0, j)),
            pl.BlockSpec((tf, D_MODEL), lambda i, j: (j, 0)),
            pl.BlockSpec((1, D_MODEL), lambda i, j: (0, 0)),
        ],
        out_specs=pl.BlockSpec((tm, D_MODEL), lambda i, j: (i, 0)),
        scratch_shapes=[pltpu.VMEM((tm, D_MODEL), BF16)],
        compiler_params=_params(("parallel", "arbitrary")),
        name="ffn",
    )(x, norm_w, w_gate, w_up, w_down, final_w)


def _prep_w_in_body(w_ref, wide_ref, gate_ref):
    o_mi = 4 * MLSTM_W
    o_fq = o_mi + 2 * MLSTM_HEADS
    o_ff = o_fq + 3 * FOX_W
    o_gq = o_ff + FOX_HEADS
    o_ga = o_gq + 4 * GDN_W
    wide_ref[0:o_mi, :] = w_ref[0:o_mi, :].astype(BF16)
    wide_ref[o_mi:o_mi + 3 * FOX_W, :] = w_ref[o_fq:o_ff, :].astype(BF16)
    o_gz = o_gq + 3 * GDN_W
    wide_ref[GDN_Z_COL:GDN_QKV_COL, :] = w_ref[o_gz:o_ga, :].astype(BF16)
    wide_ref[GDN_QKV_COL:, :] = w_ref[o_gq:o_gz, :].astype(BF16)
    gate = jnp.concatenate([w_ref[o_mi:o_fq, :], w_ref[o_ff:o_gq, :], w_ref[o_ga:, :],
                            jnp.zeros((SUBLANES, w_ref.shape[1]), F32)], axis=0)
    gate_ref[...] = gate.astype(BF16)


def _prep_w_in(w_in_t, layer, *, cols=256):
    _, n_in, d = w_in_t.shape
    return pl.pallas_call(
        _prep_w_in_body,
        out_shape=(jax.ShapeDtypeStruct((PROJ_W, d), BF16), jax.ShapeDtypeStruct((GATE_ROWS, d), BF16)),
        grid=(d // cols,),
        in_specs=[pl.BlockSpec((None, n_in, cols), lambda i: (layer, 0, i))],
        out_specs=(pl.BlockSpec((PROJ_W, cols), lambda i: (0, i)), pl.BlockSpec((GATE_ROWS, cols), lambda i: (0, i))),
        compiler_params=_params(("parallel",)),
        name="prep_w_in",
    )(w_in_t)


def kernel(x, mix_norm_w, w_in, mlstm_i_bias, mlstm_f_bias, fox_f_bias, gdn_conv_w, gdn_a_log, gdn_dt_bias,
           mlstm_out_norm_w, fox_out_norm_w, gdn_out_norm_w, w_out, ffn_norm_w, w_gate, w_up, w_down,
           final_norm_w):
    batch, seq, d_model = x.shape
    depth = w_in.shape[0]
    assert d_model == D_MODEL and seq % FOX_BLOCK == 0 and seq % REC_STEP == 0 and MLSTM_W == GDN_W
    xf = x.reshape(batch * seq, d_model)
    zeros4 = jnp.zeros((GDN_HEADS,), F32)
    w_in_t = jnp.swapaxes(w_in, 1, 2)
    side_weights = [w_gate, w_up, w_down, w_out]
    w_wide, w_gate_t = _prep_w_in(w_in_t, 0)
    for l in range(depth):
        bias_col = jnp.pad(jnp.concatenate([mlstm_i_bias[l], mlstm_f_bias[l], fox_f_bias[l], gdn_dt_bias[l],
                                            zeros4]), (0, 8)).reshape(GATE_ROWS, 1)
        alog_col = jnp.pad(gdn_a_log[l], (16, 12)).reshape(GATE_ROWS, 1)

        proj, gt = _inproj(xf, mix_norm_w[l].reshape(1, -1), w_wide, w_gate_t, gdn_conv_w[l], seq=seq)
        gr, gc3 = _gates(gt, bias_col, alog_col, batch=batch, seq=seq)
        c_rows = gr.reshape(GATE_ROWS, batch * (seq // FOX_BLOCK), 1, FOX_BLOCK)

        y_m, y_g, (w_gate_b, w_up_b, w_down_b, w_out_b) = _recurrent(
            proj, gr, gc3, mlstm_out_norm_w[l].reshape(1, -1), gdn_out_norm_w[l].reshape(1, -1),
            side_weights, l, batch=batch, seq=seq)
        y_f = _fox(proj, gc3, c_rows, fox_out_norm_w[l].reshape(1, -1), batch=batch, seq=seq)
        if l + 1 < depth:
            xf, (w_wide, w_gate_t) = _outproj(y_m, y_f, y_g, w_out_b, xf, w_in_t, l + 1)
        else:
            xf, _ = _outproj(y_m, y_f, y_g, w_out_b, xf)
        xf = _ffn(xf, ffn_norm_w[l].reshape(1, -1), w_gate_b, w_up_b, w_down_b, final_norm_w.reshape(1, -1),
                  final_norm=(l == depth - 1))
    return xf.reshape(batch, seq, d_model)
```
